```python
import jax
import jax.numpy as jnp
from jax import lax
import numpy as np

D_MODEL = 1024
BATCH = 1
SEQ = 16384
DEPTH = 2
DEC_BATCH = 8
DEC_SEQ = 2048
PAST_LEN = 128

N_MEM = 256
ROPE_THETA = 500000.0
NORM_EPS = 1e-6
GLA_HEADS = 4
GLA_DK = D_MODEL // (2 * GLA_HEADS)
GLA_DV = D_MODEL // GLA_HEADS
GLA_GATE_RANK = 16
GLA_TAU = 16.0
GLA_CHUNK = 64
MLA_HEADS = 8
MLA_Q_RANK = D_MODEL // 4
MLA_KV_RANK = D_MODEL // 8
MLA_NOPE = 64
MLA_ROPE = 32
MLA_QK = MLA_NOPE + MLA_ROPE
MLA_V = D_MODEL // MLA_HEADS
Q_BLOCK = 128
DIL_GROUPS = ((128, 1), (512, 4), (2048, 16))
DIL_N_GROUPS = 3
DIL_HEADS = 8
DIL_HEAD_DIM = D_MODEL // DIL_HEADS
DIL_ROT = DIL_HEAD_DIM // 4
X_HEADS = 4
X_HEAD_DIM = D_MODEL // X_HEADS
D_FF = 4 * D_MODEL

IN_SPLITS = (
    GLA_HEADS * GLA_DK, GLA_HEADS * GLA_DK, GLA_HEADS * GLA_DV, GLA_HEADS * GLA_DV,
    GLA_GATE_RANK, GLA_GATE_RANK,
    MLA_Q_RANK, MLA_KV_RANK, MLA_ROPE,
    DIL_N_GROUPS * DIL_HEADS * DIL_HEAD_DIM, DIL_N_GROUPS * DIL_HEADS * DIL_HEAD_DIM,
    DIL_N_GROUPS * DIL_HEADS * DIL_HEAD_DIM,
    3 * D_MODEL,
)
IN_COLS = sum(IN_SPLITS)

kernel_name = "hybrid_bidir_gla_mla_dilated_encoder"


def rms_norm(x, g):
    xf = x.astype(jnp.float32)
    y = xf * lax.rsqrt(jnp.mean(xf * xf, axis=-1, keepdims=True) + NORM_EPS)
    return (y * g.astype(jnp.float32)).astype(x.dtype)


def split_cols(t, sizes):
    offs = np.cumsum(sizes)[:-1].tolist()
    return jnp.split(t, offs, axis=-1)


def rope_tables(seq_len, rot_dim):
    inv_freq = 1.0 / (ROPE_THETA ** (jnp.arange(0, rot_dim, 2, dtype=jnp.float32) / rot_dim))
    ang = jnp.arange(seq_len, dtype=jnp.float32)[:, None] * inv_freq[None, :]
    return jnp.cos(ang), jnp.sin(ang)


def apply_rope(x, cos, sin):
    half = cos.shape[-1]
    bshape = (1, cos.shape[0]) + (1,) * (x.ndim - 3) + (half,)
    c, s = cos.reshape(bshape), sin.reshape(bshape)
    xf = x.astype(jnp.float32)
    x1, x2 = xf[..., :half], xf[..., half:]
    return jnp.concatenate([x1 * c - x2 * s, x1 * s + x2 * c], axis=-1).astype(x.dtype)


def partial_rope(x, cos, sin, rot):
    return jnp.concatenate([apply_rope(x[..., :rot], cos, sin), x[..., rot:]], axis=-1)


def gla_chunked(q, k, v, log_a):
    B, S, H, dk = q.shape
    dv = v.shape[-1]
    n = S // GLA_CHUNK

    def chunks(t):
        return t.astype(jnp.float32).reshape(B, n, GLA_CHUNK, H, t.shape[-1])

    q, k, v, log_a = chunks(q), chunks(k), chunks(v), chunks(log_a)
    b = jnp.cumsum(log_a, axis=2)
    b_last = b[:, :, -1:]
    q_dec = q * jnp.exp(b)
    k_inv = k * jnp.exp(-b)
    k_tail = k * jnp.exp(b_last - b)
    lower = jnp.tril(jnp.ones((GLA_CHUNK, GLA_CHUNK), dtype=bool))
    att = jnp.where(lower, jnp.einsum("bnihd,bnjhd->bnhij", q_dec, k_inv), 0.0)
    o_intra = jnp.einsum("bnhij,bnjhe->bnihe", att, v)
    d_state = jnp.einsum("bnjhd,bnjhe->nbhde", k_tail, v)
    decay = jnp.exp(b_last[:, :, 0]).transpose(1, 0, 2, 3)

    def step(state, inp):
        ds_n, a_n = inp
        return a_n[..., None] * state + ds_n, state

    _, s_prev = lax.scan(step, jnp.zeros((B, H, dk, dv), jnp.float32), (d_state, decay))
    o_inter = jnp.einsum("bnihd,nbhde->bnihe", q_dec, s_prev)
    return (o_intra + o_inter).reshape(B, S, H, dv)


def gla_branch(a_q, a_k, a_v, a_r, a_gf, a_gb, w_gf, b_gf, w_gb, b_gb, g_out):
    B, S, _ = a_q.shape

    def heads(t, d):
        return t.reshape(B, S, GLA_HEADS, d)

    q = heads(a_q, GLA_DK) * (GLA_DK ** -0.5)
    k = heads(a_k, GLA_DK)
    v = heads(a_v, GLA_DV)
    log_af = heads(jax.nn.log_sigmoid((a_gf @ w_gf + b_gf).astype(jnp.float32)) / GLA_TAU, GLA_DK)
    log_ab = heads(jax.nn.log_sigmoid((a_gb @ w_gb + b_gb).astype(jnp.float32)) / GLA_TAU, GLA_DK)

    def flip(t):
        return jnp.flip(t, axis=1)

    o_f = gla_chunked(q, k, v, log_af)
    o_b = flip(gla_chunked(flip(q), flip(k), flip(v), flip(log_ab)))
    o = rms_norm(o_f + o_b, g_out).reshape(B, S, GLA_HEADS * GLA_DV)
    return (o * jax.nn.silu(a_r.astype(jnp.float32))).astype(a_q.dtype)


def blocked_softmax_attn(q, k, v, scale):
    B, S, H, dq = q.shape
    nb = S // Q_BLOCK
    qb = q.reshape(B, nb, Q_BLOCK, H, dq).transpose(1, 0, 2, 3, 4)
    vf = v.astype(jnp.float32)

    def attend(q_blk):
        s = jnp.einsum("bqhd,bkhd->bhqk", q_blk, k).astype(jnp.float32) * scale
        p = jax.nn.softmax(s, axis=-1)
        return jnp.einsum("bhqk,bkhe->bqhe", p, vf)

    o = lax.map(attend, qb)
    return o.transpose(1, 0, 2, 3, 4).reshape(B, S, H, v.shape[-1]).astype(q.dtype)


def mla_branch(b_q, b_kv, b_kpe, q_lat_norm, w_uq, kv_lat_norm, w_ukv, q_norm, k_norm, cos, sin):
    B, S, _ = b_q.shape
    q = (rms_norm(b_q, q_lat_norm) @ w_uq).reshape(B, S, MLA_HEADS, MLA_QK)
    kv = (rms_norm(b_kv, kv_lat_norm) @ w_ukv).reshape(B, S, MLA_HEADS, MLA_NOPE + MLA_V)
    k_nope, v = kv[..., :MLA_NOPE], kv[..., MLA_NOPE:]
    k_pe = jnp.broadcast_to(b_kpe[:, :, None, :], (B, S, MLA_HEADS, MLA_ROPE))
    k = jnp.concatenate([k_nope, k_pe], axis=-1)
    q = rms_norm(q, q_norm)
    k = rms_norm(k, k_norm)
    q = jnp.concatenate([q[..., :MLA_NOPE], apply_rope(q[..., MLA_NOPE:], cos, sin)], axis=-1)
    k = jnp.concatenate([k[..., :MLA_NOPE], apply_rope(k[..., MLA_NOPE:], cos, sin)], axis=-1)
    o = blocked_softmax_attn(q, k, v, MLA_QK ** -0.5)
    return o.reshape(B, S, MLA_HEADS * MLA_V)


def banded_attn(q, k, v, radius, scale):
    N, L, H, dh = q.shape
    blk = radius
    nb = -(-L // blk)
    lp = nb * blk
    qb = jnp.pad(q, ((0, 0), (0, lp - L), (0, 0), (0, 0))).reshape(N, nb, blk, H, dh)

    def windows(t):
        tb = jnp.pad(t, ((0, 0), (blk, lp - L + blk), (0, 0), (0, 0))).reshape(N, nb + 2, blk, H, t.shape[-1])
        return jnp.concatenate([tb[:, :-2], tb[:, 1:-1], tb[:, 2:]], axis=2)

    kw = windows(k)
    vw = windows(v).astype(jnp.float32)
    q_pos = jnp.arange(lp).reshape(nb, blk)
    k_pos = (jnp.arange(nb) * blk - blk)[:, None] + jnp.arange(3 * blk)[None, :]
    kp = k_pos[:, None, :]
    valid = (jnp.abs(q_pos[:, :, None] - kp) <= radius) & (kp >= 0) & (kp < L)
    s = jnp.einsum("nbqhd,nbkhd->nbhqk", qb, kw).astype(jnp.float32) * scale
    s = jnp.where(valid[None, :, None], s, -jnp.inf)
    m = jnp.max(s, axis=-1, keepdims=True)
    p = jnp.exp(s - m)
    l = jnp.sum(p, axis=-1)
    o = jnp.einsum("nbhqk,nbkhd->nbqhd", p, vw) / jnp.transpose(l, (0, 1, 3, 2))[..., None]
    lse = (m[..., 0] + jnp.log(l)).transpose(0, 1, 3, 2)
    return o.reshape(N, lp, H, dh)[:, :L], lse.reshape(N, lp, H)[:, :L]


def dilated_group(q, k, v, dilation, radius):
    B, S, H, dh = q.shape
    L = S // dilation

    def gather(t):
        e = t.shape[-1]
        return t.reshape(B, L, dilation, H, e).transpose(0, 2, 1, 3, 4).reshape(B * dilation, L, H, e)

    def scatter(t):
        e = t.shape[-1]
        return t.reshape(B, dilation, L, H, e).transpose(0, 2, 1, 3, 4).reshape(B, S, H, e)

    o, lse = banded_attn(gather(q), gather(k), gather(v), radius, dh ** -0.5)
    return scatter(o), scatter(lse[..., None])[..., 0]


def dilated_branch(c_q, c_k, c_v, q_norm, k_norm, cos, sin):
    B, S, _ = c_q.shape

    def heads(t):
        return t.reshape(B, S, DIL_N_GROUPS, DIL_HEADS, DIL_HEAD_DIM)

    q = partial_rope(rms_norm(heads(c_q), q_norm), cos, sin, DIL_ROT)
    k = partial_rope(rms_norm(heads(c_k), k_norm), cos, sin, DIL_ROT)
    v = heads(c_v)
    outs, lses = [], []
    for g, (window, dilation) in enumerate(DIL_GROUPS):
        o_g, lse_g = dilated_group(q[:, :, g], k[:, :, g], v[:, :, g], dilation, window // (2 * dilation))
        outs.append(o_g)
        lses.append(lse_g)
    o = jnp.stack(outs, axis=0)
    wgt = jax.nn.softmax(jnp.stack(lses, axis=0), axis=0)
    return jnp.sum(wgt[..., None] * o, axis=0).reshape(B, S, DIL_HEADS * DIL_HEAD_DIM).astype(c_q.dtype)


def memory_xattn(h, mem, norm_mem, w_xq, w_xkv, xq_norm, xk_norm, w_xo):
    B, S, _ = h.shape
    q = rms_norm((h @ w_xq).reshape(B, S, X_HEADS, X_HEAD_DIM), xq_norm)
    kv = rms_norm(mem, norm_mem) @ w_xkv
    k = rms_norm(kv[..., :D_MODEL].reshape(B, N_MEM, X_HEADS, X_HEAD_DIM), xk_norm)
    v = kv[..., D_MODEL:].reshape(B, N_MEM, X_HEADS, X_HEAD_DIM)
    s = jnp.einsum("bshd,bmhd->bhsm", q, k).astype(jnp.float32) * (X_HEAD_DIM ** -0.5)
    p = jax.nn.softmax(s, axis=-1)
    o = jnp.einsum("bhsm,bmhd->bshd", p, v.astype(jnp.float32)).astype(h.dtype)
    return o.reshape(B, S, D_MODEL) @ w_xo


def encoder_layer(x, mem, cos_mla, sin_mla, cos_dil, sin_dil,
                  norm_mix, w_in,
                  gla_w_gate_f, gla_b_gate_f, gla_w_gate_b, gla_b_gate_b, gla_out_norm, w_branch_a,
                  mla_q_lat_norm, mla_w_uq, mla_kv_lat_norm, mla_w_ukv, mla_q_norm, mla_k_norm, w_branch_b,
                  dil_q_norm, dil_k_norm, w_branch_c, w_out,
                  norm_xattn, norm_mem, w_xq, w_xkv, xq_norm, xk_norm, w_xo,
                  norm_ffn, w_ff1, w_ff2):
    h = rms_norm(x, norm_mix)
    (a_q, a_k, a_v, a_r, a_gf, a_gb, b_q, b_kv, b_kpe,
     c_q, c_k, c_v, gate_logits) = split_cols(h @ w_in, IN_SPLITS)
    y_a = gla_branch(a_q, a_k, a_v, a_r, a_gf, a_gb, gla_w_gate_f, gla_b_gate_f,
                     gla_w_gate_b, gla_b_gate_b, gla_out_norm) @ w_branch_a
    y_b = mla_branch(b_q, b_kv, b_kpe, mla_q_lat_norm, mla_w_uq, mla_kv_lat_norm, mla_w_ukv,
                     mla_q_norm, mla_k_norm, cos_mla, sin_mla) @ w_branch_b
    y_c = dilated_branch(c_q, c_k, c_v, dil_q_norm, dil_k_norm, cos_dil, sin_dil) @ w_branch_c
    g_a, g_b, g_c = jnp.split(jax.nn.sigmoid(gate_logits), 3, axis=-1)
    x = x + (g_a * y_a + g_b * y_b + g_c * y_c) @ w_out
    x = x + memory_xattn(rms_norm(x, norm_xattn), mem, norm_mem, w_xq, w_xkv, xq_norm, xk_norm, w_xo)
    x = x + jnp.square(jax.nn.relu(rms_norm(x, norm_ffn) @ w_ff1)) @ w_ff2
    return x


def setup_inputs(seed: int = 0) -> dict:
    key = jax.random.key(seed)
    ks = iter(jax.random.split(key, 48))

    def nrm(shape, scale):
        return jax.random.normal(next(ks), shape, jnp.float32) * scale

    def w(shape):
        return nrm(shape, shape[-2] ** -0.5)

    def g(shape):
        return 1.0 + nrm(shape, 0.02)

    L, D = DEPTH, D_MODEL
    return {
        "x_prompt": nrm((BATCH, SEQ, D), 1.0),
        "x_sample": nrm((DEC_BATCH, DEC_SEQ, D), 1.0),
        "mem_prompt": nrm((BATCH, N_MEM, D), 1.0),
        "mem_sample": nrm((DEC_BATCH, N_MEM, D), 1.0),
        "norm_mix": g((L, D)),
        "w_in": w((L, D, IN_COLS)),
        "gla_w_gate_f": w((L, GLA_GATE_RANK, GLA_HEADS * GLA_DK)),
        "gla_b_gate_f": nrm((L, GLA_HEADS * GLA_DK), 0.1),
        "gla_w_gate_b": w((L, GLA_GATE_RANK, GLA_HEADS * GLA_DK)),
        "gla_b_gate_b": nrm((L, GLA_HEADS * GLA_DK), 0.1),
        "gla_out_norm": g((L, GLA_DV)),
        "w_branch_a": w((L, GLA_HEADS * GLA_DV, D)),
        "mla_q_lat_norm": g((L, MLA_Q_RANK)),
        "mla_w_uq": w((L, MLA_Q_RANK, MLA_HEADS * MLA_QK)),
        "mla_kv_lat_norm": g((L, MLA_KV_RANK)),
        "mla_w_ukv": w((L, MLA_KV_RANK, MLA_HEADS * (MLA_NOPE + MLA_V))),
        "mla_q_norm": g((L, MLA_QK)),
        "mla_k_norm": g((L, MLA_QK)),
        "w_branch_b": w((L, MLA_HEADS * MLA_V, D)),
        "dil_q_norm": g((L, DIL_HEAD_DIM)),
        "dil_k_norm": g((L, DIL_HEAD_DIM)),
        "w_branch_c": w((L, DIL_HEADS * DIL_HEAD_DIM, D)),
        "w_out": w((L, D, D)),
        "norm_xattn": g((L, D)),
        "norm_mem": g((L, D)),
        "w_xq": w((L, D, D)),
        "w_xkv": w((L, D, 2 * D)),
        "xq_norm": g((L, X_HEAD_DIM)),
        "xk_norm": g((L, X_HEAD_DIM)),
        "w_xo": w((L, D, D)),
        "norm_ffn": g((L, D)),
        "w_ff1": w((L, D, D_FF)),
        "w_ff2": w((L, D_FF, D)),
    }


def reference(x_prompt, x_sample, mem_prompt, mem_sample,
              norm_mix, w_in,
              gla_w_gate_f, gla_b_gate_f, gla_w_gate_b, gla_b_gate_b, gla_out_norm, w_branch_a,
              mla_q_lat_norm, mla_w_uq, mla_kv_lat_norm, mla_w_ukv, mla_q_norm, mla_k_norm, w_branch_b,
              dil_q_norm, dil_k_norm, w_branch_c, w_out,
              norm_xattn, norm_mem, w_xq, w_xkv, xq_norm, xk_norm, w_xo,
              norm_ffn, w_ff1, w_ff2):
    def trunk(x, mem):
        S = x.shape[1]
        cos_mla, sin_mla = rope_tables(S, MLA_ROPE)
        cos_dil, sin_dil = rope_tables(S, DIL_ROT)
        for l in range(DEPTH):
            x = encoder_layer(
                x, mem, cos_mla, sin_mla, cos_dil, sin_dil,
                norm_mix[l], w_in[l],
                gla_w_gate_f[l], gla_b_gate_f[l], gla_w_gate_b[l], gla_b_gate_b[l], gla_out_norm[l], w_branch_a[l],
                mla_q_lat_norm[l], mla_w_uq[l], mla_kv_lat_norm[l], mla_w_ukv[l], mla_q_norm[l], mla_k_norm[l],
                w_branch_b[l],
                dil_q_norm[l], dil_k_norm[l], w_branch_c[l], w_out[l],
                norm_xattn[l], norm_mem[l], w_xq[l], w_xkv[l], xq_norm[l], xk_norm[l], w_xo[l],
                norm_ffn[l], w_ff1[l], w_ff2[l])
        return x

    y_prompt = trunk(x_prompt, mem_prompt)
    y_sample = trunk(x_sample, mem_sample)
    return (y_prompt, y_sample)
```

```python
import functools

import jax
import jax.numpy as jnp
from jax import lax
from jax.experimental import pallas as pl
from jax.experimental.pallas import tpu as pltpu

F32 = jnp.float32
BF16 = jnp.bfloat16

D_MODEL = 1024
DEPTH = 2
N_MEM = 256
ROPE_THETA = 500000.0
NORM_EPS = 1e-6
GLA_HEADS = 4
GLA_DK = 128
GLA_DV = 256
GLA_GATE_RANK = 16
GLA_TAU = 16.0
GLA_CHUNK = 64
MLA_HEADS = 8
MLA_Q_RANK = 256
MLA_KV_RANK = 128
MLA_NOPE = 64
MLA_ROPE = 32
MLA_QK = 96
MLA_V = 128
DIL_GROUPS = ((128, 1), (512, 4), (2048, 16))
DIL_HEADS = 8
DIL_HEAD_DIM = 128
DIL_ROT = 32
DIL_RADIUS = 64
X_HEADS = 4
X_HEAD_DIM = 256
D_FF = 4096

LANES = 128
ROPE_HALF = 16
VMEM_LIMIT_BYTES = 56 * 1024 * 1024

COL_CQ, COL_CK, COL_CV, COL_GATE = 0, 3072, 6144, 9216
COL_AQ, COL_AK, COL_AV, COL_AR = 12288, 12800, 13312, 14336
COL_BQ, COL_BKV, COL_SMALL = 15360, 15616, 15744
IN_COLS_PAD = 16384
SMALL_GF, SMALL_GB, SMALL_KPE = 0, 16, 64

NT_DIMS = (((1,), (1,)), ((), ()))


def _cparams(*sem):
    return pltpu.CompilerParams(dimension_semantics=sem, vmem_limit_bytes=VMEM_LIMIT_BYTES)


def _rms(x, g, n=None):
    ss = jnp.sum(x * x, axis=-1, keepdims=True) * (1.0 / (n or x.shape[-1]))
    return x * lax.rsqrt(ss + NORM_EPS) * g


def _log_sigmoid(z):
    return jnp.minimum(z, 0.0) - jnp.log(1.0 + jnp.exp(-jnp.abs(z)))


def _sigmoid(z):
    return 1.0 / (1.0 + jnp.exp(-z))


def _norm_matmul_kernel(x_ref, g_ref, w_ref, o_ref, h_ref):
    @pl.when(pl.program_id(1) == 0)
    def _():
        h_ref[...] = _rms(x_ref[...].astype(F32), g_ref[...]).astype(BF16)

    o_ref[...] = jnp.dot(h_ref[...], w_ref[...], preferred_element_type=F32).astype(o_ref.dtype)


def _norm_matmul(x, g, w, out_dtype, tm, tn):
    m, k = x.shape
    n = w.shape[1]
    return pl.pallas_call(
        _norm_matmul_kernel,
        grid=(m // tm, n // tn),
        in_specs=[pl.BlockSpec((tm, k), lambda i, j: (i, 0)),
                  pl.BlockSpec((1, k), lambda i, j: (0, 0)),
                  pl.BlockSpec((k, tn), lambda i, j: (0, j))],
        out_specs=pl.BlockSpec((tm, tn), lambda i, j: (i, j)),
        out_shape=jax.ShapeDtypeStruct((m, n), out_dtype),
        scratch_shapes=[pltpu.VMEM((tm, k), BF16)],
        compiler_params=_cparams("parallel", "arbitrary"),
        name="norm_matmul",
    )(x, g, w)


def _gla_kernel(*refs, tb, reverse, final):
    if final:
        (q_ref, k_ref, v_ref, sm_ref, wg_ref, bg_ref, wgt_ref, bgt_ref,
         of_ref, r_ref, gn_ref, o_ref, st_ref) = refs
    else:
        q_ref, k_ref, v_ref, sm_ref, wg_ref, bg_ref, wgt_ref, bgt_ref, o_ref, st_ref = refs

    @pl.when(pl.program_id(1) == 0)
    def _():
        st_ref[...] = jnp.zeros_like(st_ref)

    ck = GLA_CHUNK
    sm = sm_ref[...]
    la = _log_sigmoid(jnp.dot(sm, wg_ref[...], preferred_element_type=F32) + bg_ref[...]) * (1.0 / GLA_TAU)
    lat = _log_sigmoid(lax.dot_general(wgt_ref[...], sm, NT_DIMS, preferred_element_type=F32)
                       + bgt_ref[...]) * (1.0 / GLA_TAU)
    kt = k_ref[...].astype(F32).T

    r = lax.broadcasted_iota(jnp.int32, (ck, ck), 0)
    c = lax.broadcasted_iota(jnp.int32, (ck, ck), 1)
    lower = r >= c
    upper = r <= c
    t_rows = jnp.where(upper if reverse else lower, 1.0, 0.0).astype(F32)
    t_cols = jnp.where(lower if reverse else upper, 1.0, 0.0).astype(F32)
    att_mask = upper if reverse else lower
    end = 0 if reverse else ck - 1
    scale = GLA_DK ** -0.5

    n_chunks = tb // ck
    order = range(n_chunks - 1, -1, -1) if reverse else range(n_chunks)
    for ci in order:
        sl = slice(ci * ck, (ci + 1) * ck)
        b = jnp.dot(t_rows, la[sl], precision=lax.Precision.HIGHEST, preferred_element_type=F32)
        bt = jnp.dot(lat[:, sl], t_cols, precision=lax.Precision.HIGHEST, preferred_element_type=F32)
        bt_end = bt[:, end:end + 1]
        qd = (q_ref[sl, :].astype(F32) * scale * jnp.exp(b)).astype(BF16)
        ktc = kt[:, sl]
        kinv_t = (ktc * jnp.exp(-bt)).astype(BF16)
        ktail_t = (ktc * jnp.exp(bt_end - bt)).astype(BF16)
        dec = jnp.exp(bt_end)
        for h in range(GLA_HEADS):
            ks = slice(h * GLA_DK, (h + 1) * GLA_DK)
            vs = slice(h * GLA_DV, (h + 1) * GLA_DV)
            qh = qd[:, ks]
            att = jnp.dot(qh, kinv_t[ks], preferred_element_type=F32)
            att = jnp.where(att_mask, att, 0.0).astype(BF16)
            vh = v_ref[sl, vs]
            st = st_ref[h]
            o = (jnp.dot(att, vh, preferred_element_type=F32)
                 + jnp.dot(qh, st.astype(BF16), preferred_element_type=F32))
            st_ref[h] = dec[ks] * st + jnp.dot(ktail_t[ks], vh, preferred_element_type=F32)
            if final:
                o = _rms(o + of_ref[sl, vs], gn_ref[...])
                rr = r_ref[sl, vs].astype(F32)
                o = o * (rr * _sigmoid(rr))
            o_ref[sl, vs] = o.astype(o_ref.dtype)


def _gla_direction(proj, wg, bg, wgt, bgt, batch, seq, reverse, o_fwd=None, g_out=None, tb=256):
    nb = seq // tb
    m = batch * seq
    final = o_fwd is not None

    def row(b, i):
        return b * nb + (nb - 1 - i if reverse else i)

    def col(block_cols, off):
        return lambda b, i: (row(b, i), off // block_cols)

    hk = GLA_HEADS * GLA_DK
    hv = GLA_HEADS * GLA_DV
    const = lambda b, i: (0, 0)
    in_specs = [pl.BlockSpec((tb, hk), col(hk, COL_AQ)),
                pl.BlockSpec((tb, hk), col(hk, COL_AK)),
                pl.BlockSpec((tb, hv), col(hv, COL_AV)),
                pl.BlockSpec((tb, LANES), col(LANES, COL_SMALL)),
                pl.BlockSpec((LANES, hk), const),
                pl.BlockSpec((1, hk), const),
                pl.BlockSpec((hk, LANES), const),
                pl.BlockSpec((hk, 1), const)]
    args = [proj, proj, proj, proj, wg, bg, wgt, bgt]
    if final:
        in_specs += [pl.BlockSpec((tb, hv), col(hv, 0)),
                     pl.BlockSpec((tb, hv), col(hv, COL_AR)),
                     pl.BlockSpec((1, GLA_DV), const)]
        args += [o_fwd, proj, g_out]
    return pl.pallas_call(
        functools.partial(_gla_kernel, tb=tb, reverse=reverse, final=final),
        grid=(batch, nb),
        in_specs=in_specs,
        out_specs=pl.BlockSpec((tb, hv), col(hv, 0)),
        out_shape=jax.ShapeDtypeStruct((m, hv), BF16 if final else F32),
        scratch_shapes=[pltpu.VMEM((GLA_HEADS, GLA_DK, GLA_DV), F32)],
        compiler_params=_cparams("parallel", "arbitrary"),
        name="gla_bwd" if reverse else "gla_fwd",
    )(*args)


def _rope(y, cos, sin_signed, lane, base):
    up = pltpu.roll(y, LANES - ROPE_HALF, 1)
    down = pltpu.roll(y, ROPE_HALF, 1)
    partner = jnp.where(lane < base + ROPE_HALF, up, down)
    return y * cos + partner * sin_signed


def _mla_prep_kernel(bq_ref, bkv_ref, sm_ref, qln_ref, wuq_ref, kvln_ref, wuk_ref, wuv_ref,
                     qn_ref, kn_ref, cos_ref, sin_ref, q_out, k_out, v_out):
    hq = _rms(bq_ref[...].astype(F32), qln_ref[...]).astype(BF16)
    hkv = _rms(bkv_ref[...].astype(F32), kvln_ref[...]).astype(BF16)
    q = jnp.dot(hq, wuq_ref[...], preferred_element_type=F32)
    kn = jnp.dot(hkv, wuk_ref[...], preferred_element_type=F32)
    v_out[...] = jnp.dot(hkv, wuv_ref[...], preferred_element_type=F32).astype(v_out.dtype)
    tm = q.shape[0]
    lane = lax.broadcasted_iota(jnp.int32, (tm, LANES), 1)
    sm = sm_ref[...].astype(F32)
    kpe = jnp.where(lane >= SMALL_KPE, jnp.where(lane < SMALL_KPE + MLA_ROPE, sm, 0.0), 0.0)
    cos = cos_ref[...]
    sin = sin_ref[...]
    scale = MLA_QK ** -0.5
    for h in range(MLA_HEADS):
        hs = slice(h * LANES, (h + 1) * LANES)
        qh = _rope(_rms(q[:, hs], qn_ref[...], MLA_QK), cos, sin, lane, MLA_NOPE)
        q_out[:, hs] = (qh * scale).astype(q_out.dtype)
        kh = _rope(_rms(kn[:, hs] + kpe, kn_ref[...], MLA_QK), cos, sin, lane, MLA_NOPE)
        k_out[:, hs] = kh.astype(k_out.dtype)


def _mla_prep(proj, p, cos, sin, seq, tm=512):
    m = proj.shape[0]
    ns = seq // tm
    const = lambda i: (0, 0)
    wide = MLA_HEADS * LANES
    out = jax.ShapeDtypeStruct((m, wide), BF16)
    return pl.pallas_call(
        _mla_prep_kernel,
        grid=(m // tm,),
        in_specs=[pl.BlockSpec((tm, MLA_Q_RANK), lambda i: (i, COL_BQ // MLA_Q_RANK)),
                  pl.BlockSpec((tm, MLA_KV_RANK), lambda i: (i, COL_BKV // MLA_KV_RANK)),
                  pl.BlockSpec((tm, LANES), lambda i: (i, COL_SMALL // LANES)),
                  pl.BlockSpec((1, MLA_Q_RANK), const),
                  pl.BlockSpec((MLA_Q_RANK, wide), const),
                  pl.BlockSpec((1, MLA_KV_RANK), const),
                  pl.BlockSpec((MLA_KV_RANK, wide), const),
                  pl.BlockSpec((MLA_KV_RANK, wide), const),
                  pl.BlockSpec((1, LANES), const),
                  pl.BlockSpec((1, LANES), const),
                  pl.BlockSpec((tm, LANES), lambda i: (i % ns, 0)),
                  pl.BlockSpec((tm, LANES), lambda i: (i % ns, 0))],
        out_specs=[pl.BlockSpec((tm, wide), lambda i: (i, 0))] * 3,
        out_shape=[out, out, out],
        compiler_params=_cparams("parallel"),
        name="mla_prep",
    )(proj, proj, proj, p["mla_q_lat_norm"], p["mla_w_uq"], p["mla_kv_lat_norm"], p["mla_w_uk"],
      p["mla_w_uv"], p["mla_q_norm"], p["mla_k_norm"], cos, sin)


def _mla_attn_kernel(q_ref, k_ref, v_ref, o_ref, *, tk):
    q = q_ref[...]
    tq = q.shape[0]
    nk = k_ref.shape[0] // tk

    def body(t, carry):
        m, l, acc = carry
        start = pl.multiple_of(t * tk, tk)
        ks = k_ref[pl.ds(start, tk), :]
        vs = v_ref[pl.ds(start, tk), :]
        s = lax.dot_general(q, ks, NT_DIMS, preferred_element_type=F32)
        m_new = jnp.maximum(m, jnp.max(s, axis=-1, keepdims=True))
        alpha = jnp.exp(m - m_new)
        p = jnp.exp(s - m_new)
        l = alpha * l + jnp.sum(p, axis=-1, keepdims=True)
        acc = alpha * acc + jnp.dot(p.astype(BF16), vs, preferred_element_type=F32)
        return m_new, l, acc

    init = (jnp.full((tq, 1), -jnp.inf, F32), jnp.zeros((tq, 1), F32), jnp.zeros((tq, LANES), F32))
    _, l, acc = lax.fori_loop(0, nk, body, init)
    o_ref[...] = (acc / l).astype(o_ref.dtype)


def _mla_attn(q, k, v, batch, seq, tq=256, tk=512):
    nq = seq // tq
    return pl.pallas_call(
        functools.partial(_mla_attn_kernel, tk=tk),
        grid=(batch, MLA_HEADS, nq),
        in_specs=[pl.BlockSpec((tq, LANES), lambda b, h, i: (b * nq + i, h)),
                  pl.BlockSpec((seq, LANES), lambda b, h, i: (b, h)),
                  pl.BlockSpec((seq, LANES), lambda b, h, i: (b, h))],
        out_specs=pl.BlockSpec((tq, LANES), lambda b, h, i: (b * nq + i, h)),
        out_shape=jax.ShapeDtypeStruct(q.shape, BF16),
        compiler_params=_cparams("parallel", "parallel", "arbitrary"),
        name="mla_attn",
    )(q, k, v)


def _dil_prep_kernel(x_ref, nw_ref, cos_ref, sin_ref, o_ref, *, scale_q):
    tm = x_ref.shape[0]
    lane = lax.broadcasted_iota(jnp.int32, (tm, LANES), 1)
    cos = cos_ref[...]
    sin = sin_ref[...]
    nw = nw_ref[0]
    is_q = pl.program_id(1) < len(DIL_GROUPS)
    scale = jnp.where(is_q, scale_q, 1.0).astype(F32)
    for h in range(DIL_HEADS):
        hs = slice(h * LANES, (h + 1) * LANES)
        y = _rope(_rms(x_ref[:, hs].astype(F32), nw), cos, sin, lane, 0)
        o_ref[:, hs] = (y * scale).astype(o_ref.dtype)


def _dil_prep(proj, qk_norm, cos, sin, seq, tm=512):
    m = proj.shape[0]
    ns = seq // tm
    wide = DIL_HEADS * DIL_HEAD_DIM
    ng = len(DIL_GROUPS)
    return pl.pallas_call(
        functools.partial(_dil_prep_kernel, scale_q=DIL_HEAD_DIM ** -0.5),
        grid=(m // tm, 2 * ng),
        in_specs=[pl.BlockSpec((tm, wide), lambda i, j: (i, j)),
                  pl.BlockSpec((1, 1, LANES), lambda i, j: (j // ng, 0, 0)),
                  pl.BlockSpec((tm, LANES), lambda i, j: (i % ns, 0)),
                  pl.BlockSpec((tm, LANES), lambda i, j: (i % ns, 0))],
        out_specs=pl.BlockSpec((tm, wide), lambda i, j: (i, j)),
        out_shape=jax.ShapeDtypeStruct((m, 2 * ng * wide), BF16),
        compiler_params=_cparams("parallel", "arbitrary"),
        name="dil_prep",
    )(proj, qk_norm, cos, sin)


def _dil_attn_kernel(q_ref, kl_ref, kc_ref, kr_ref, vl_ref, vc_ref, vr_ref, o_ref, lse_ref, *, sub_len):
    tl = q_ref.shape[0]
    nkeys = tl + 2 * DIL_RADIUS
    row = lax.broadcasted_iota(jnp.int32, (tl, nkeys), 0)
    col = lax.broadcasted_iota(jnp.int32, (tl, nkeys), 1)
    rel = col - row
    pos = pl.program_id(2) * tl - DIL_RADIUS + col
    valid = (rel >= 0) & (rel <= 2 * DIL_RADIUS) & (pos >= 0) & (pos < sub_len)
    lane = lax.broadcasted_iota(jnp.int32, (tl, LANES), 1)
    lse_all = jnp.zeros((tl, LANES), F32)
    for h in range(DIL_HEADS):
        hs = slice(h * LANES, (h + 1) * LANES)
        kk = jnp.concatenate([kl_ref[:, hs], kc_ref[:, hs], kr_ref[:, hs]], axis=0)
        vv = jnp.concatenate([vl_ref[:, hs], vc_ref[:, hs], vr_ref[:, hs]], axis=0)
        s = lax.dot_general(q_ref[:, hs], kk, NT_DIMS, preferred_element_type=F32)
        s = jnp.where(valid, s, -jnp.inf)
        m = jnp.max(s, axis=-1, keepdims=True)
        p = jnp.exp(s - m)
        l = jnp.sum(p, axis=-1, keepdims=True)
        o_ref[:, hs] = jnp.dot(p.astype(BF16), vv, preferred_element_type=F32) / l
        lse_all = jnp.where(lane == h, m + jnp.log(l), lse_all)
    lse_ref[...] = lse_all


def _dil_attn(qk, proj, group, batch, seq, tl=128):
    _, dil = DIL_GROUPS[group]
    ng = len(DIL_GROUPS)
    sub_len = seq // dil
    nl = sub_len // tl
    rows = batch * sub_len
    wide = DIL_HEADS * DIL_HEAD_DIM
    halo = DIL_RADIUS
    per = tl // halo
    n_halo_blocks = rows // halo
    qk_v = qk.reshape(rows, dil * 2 * ng * wide)
    pj_v = proj.reshape(rows, dil * IN_COLS_PAD)
    qk_blocks = 2 * ng
    pj_blocks = IN_COLS_PAD // wide
    cv_block = COL_CV // wide + group

    def cur(cb):
        return lambda b, r, i: (b * nl + i, cb(r))

    def left(cb):
        return lambda b, r, i: (jnp.maximum((b * nl + i) * per - 1, 0), cb(r))

    def right(cb):
        return lambda b, r, i: (jnp.minimum((b * nl + i + 1) * per, n_halo_blocks - 1), cb(r))

    qc = lambda r: r * qk_blocks + group
    kc = lambda r: r * qk_blocks + ng + group
    vc = lambda r: r * pj_blocks + cv_block
    o, lse = pl.pallas_call(
        functools.partial(_dil_attn_kernel, sub_len=sub_len),
        grid=(batch, dil, nl),
        in_specs=[pl.BlockSpec((tl, wide), cur(qc)),
                  pl.BlockSpec((halo, wide), left(kc)),
                  pl.BlockSpec((tl, wide), cur(kc)),
                  pl.BlockSpec((halo, wide), right(kc)),
                  pl.BlockSpec((halo, wide), left(vc)),
                  pl.BlockSpec((tl, wide), cur(vc)),
                  pl.BlockSpec((halo, wide), right(vc))],
        out_specs=[pl.BlockSpec((tl, wide), lambda b, r, i: (b * nl + i, r)),
                   pl.BlockSpec((tl, LANES), lambda b, r, i: (b * nl + i, r))],
        out_shape=[jax.ShapeDtypeStruct((rows, dil * wide), F32),
                   jax.ShapeDtypeStruct((rows, dil * LANES), F32)],
        compiler_params=_cparams("parallel", "parallel", "arbitrary"),
        name=f"dil_attn_g{group}",
    )(qk_v, qk_v, qk_v, qk_v, pj_v, pj_v, pj_v)
    return o.reshape(batch * seq, wide), lse.reshape(batch * seq, LANES)


def _combine_kernel(x_ref, a_ref, b_ref, o0_ref, o1_ref, o2_ref, l0_ref, l1_ref, l2_ref, gate_ref,
                    wa_ref, wb_ref, wc_ref, wo_ref, out_ref, c_ref):
    l0, l1, l2 = l0_ref[...], l1_ref[...], l2_ref[...]
    mx = jnp.maximum(jnp.maximum(l0, l1), l2)
    e0, e1, e2 = jnp.exp(l0 - mx), jnp.exp(l1 - mx), jnp.exp(l2 - mx)
    inv = 1.0 / (e0 + e1 + e2)
    w0, w1, w2 = e0 * inv, e1 * inv, e2 * inv
    for h in range(DIL_HEADS):
        hs = slice(h * LANES, (h + 1) * LANES)
        c = (w0[:, h:h + 1] * o0_ref[:, hs] + w1[:, h:h + 1] * o1_ref[:, hs] + w2[:, h:h + 1] * o2_ref[:, hs])
        c_ref[:, hs] = c.astype(BF16)
    ya = jnp.dot(a_ref[...], wa_ref[...], preferred_element_type=F32)
    yb = jnp.dot(b_ref[...], wb_ref[...], preferred_element_type=F32)
    yc = jnp.dot(c_ref[...], wc_ref[...], preferred_element_type=F32)
    d = D_MODEL
    mix = (_sigmoid(gate_ref[:, 0:d].astype(F32)) * ya
           + _sigmoid(gate_ref[:, d:2 * d].astype(F32)) * yb
           + _sigmoid(gate_ref[:, 2 * d:3 * d].astype(F32)) * yc)
    out_ref[...] = x_ref[...] + jnp.dot(mix.astype(BF16), wo_ref[...], preferred_element_type=F32)


def _combine(x, gla_o, mla_o, dil_o, dil_lse, proj, p, tm=256):
    m = x.shape[0]
    d = D_MODEL
    rowblk = lambda i: (i, 0)
    const = lambda i: (0, 0)
    wspec = pl.BlockSpec((d, d), const)
    return pl.pallas_call(
        _combine_kernel,
        grid=(m // tm,),
        in_specs=[pl.BlockSpec((tm, d), rowblk), pl.BlockSpec((tm, d), rowblk), pl.BlockSpec((tm, d), rowblk),
                  pl.BlockSpec((tm, d), rowblk), pl.BlockSpec((tm, d), rowblk), pl.BlockSpec((tm, d), rowblk),
                  pl.BlockSpec((tm, LANES), rowblk), pl.BlockSpec((tm, LANES), rowblk),
                  pl.BlockSpec((tm, LANES), rowblk),
                  pl.BlockSpec((tm, 3 * d), lambda i: (i, COL_GATE // (3 * d))),
                  wspec, wspec, wspec, wspec],
        out_specs=pl.BlockSpec((tm, d), rowblk),
        out_shape=jax.ShapeDtypeStruct((m, d), F32),
        scratch_shapes=[pltpu.VMEM((tm, d), BF16)],
        compiler_params=_cparams("parallel"),
        name="combine",
    )(x, gla_o, mla_o, dil_o[0], dil_o[1], dil_o[2], dil_lse[0], dil_lse[1], dil_lse[2], proj,
      p["w_branch_a"], p["w_branch_b"], p["w_branch_c"], p["w_out"])


def _xattn_kernel(x_ref, g_ref, wq_ref, qn_ref, kv_ref, kn_ref, wo_ref, out_ref, o_ref):
    x = x_ref[...]
    h = _rms(x, g_ref[...]).astype(BF16)
    q = jnp.dot(h, wq_ref[...], preferred_element_type=F32)
    scale = X_HEAD_DIM ** -0.5
    for hd in range(X_HEADS):
        hs = slice(hd * X_HEAD_DIM, (hd + 1) * X_HEAD_DIM)
        vs = slice(D_MODEL + hd * X_HEAD_DIM, D_MODEL + (hd + 1) * X_HEAD_DIM)
        qh = (_rms(q[:, hs], qn_ref[...]) * scale).astype(BF16)
        kh = _rms(kv_ref[:, hs].astype(F32), kn_ref[...]).astype(BF16)
        s = lax.dot_general(qh, kh, NT_DIMS, preferred_element_type=F32)
        p = jnp.exp(s - jnp.max(s, axis=-1, keepdims=True))
        l = jnp.sum(p, axis=-1, keepdims=True)
        o = jnp.dot(p.astype(BF16), kv_ref[:, vs], preferred_element_type=F32) / l
        o_ref[:, hs] = o.astype(BF16)
    out_ref[...] = x + jnp.dot(o_ref[...], wo_ref[...], preferred_element_type=F32)


def _xattn(x, kv, p, seq, tm=512):
    m = x.shape[0]
    d = D_MODEL
    per_seq = seq // tm
    const = lambda i: (0, 0)
    return pl.pallas_call(
        _xattn_kernel,
        grid=(m // tm,),
        in_specs=[pl.BlockSpec((tm, d), lambda i: (i, 0)),
                  pl.BlockSpec((1, d), const),
                  pl.BlockSpec((d, d), const),
                  pl.BlockSpec((1, X_HEAD_DIM), const),
                  pl.BlockSpec((N_MEM, 2 * d), lambda i: (i // per_seq, 0)),
                  pl.BlockSpec((1, X_HEAD_DIM), const),
                  pl.BlockSpec((d, d), const)],
        out_specs=pl.BlockSpec((tm, d), lambda i: (i, 0)),
        out_shape=jax.ShapeDtypeStruct((m, d), F32),
        scratch_shapes=[pltpu.VMEM((tm, d), BF16)],
        compiler_params=_cparams("parallel"),
        name="xattn",
    )(x, p["norm_xattn"], p["w_xq"], p["xq_norm"], kv, p["xk_norm"], p["w_xo"])


def _ffn_kernel(x_ref, g_ref, w1_ref, w2_ref, out_ref, h_ref, acc_ref):
    j = pl.program_id(1)

    @pl.when(j == 0)
    def _():
        h_ref[...] = _rms(x_ref[...], g_ref[...]).astype(BF16)
        acc_ref[...] = x_ref[...]

    u = jnp.maximum(jnp.dot(h_ref[...], w1_ref[...], preferred_element_type=F32), 0.0)
    acc_ref[...] += jnp.dot((u * u).astype(BF16), w2_ref[...], preferred_element_type=F32)

    @pl.when(j == pl.num_programs(1) - 1)
    def _():
        out_ref[...] = acc_ref[...]


def _ffn(x, p, tm=512, tf=1024):
    m = x.shape[0]
    d = D_MODEL
    return pl.pallas_call(
        _ffn_kernel,
        grid=(m // tm, D_FF // tf),
        in_specs=[pl.BlockSpec((tm, d), lambda i, j: (i, 0)),
                  pl.BlockSpec((1, d), lambda i, j: (0, 0)),
                  pl.BlockSpec((d, tf), lambda i, j: (0, j)),
                  pl.BlockSpec((tf, d), lambda i, j: (j, 0))],
        out_specs=pl.BlockSpec((tm, d), lambda i, j: (i, 0)),
        out_shape=jax.ShapeDtypeStruct((m, d), F32),
        scratch_shapes=[pltpu.VMEM((tm, d), BF16), pltpu.VMEM((tm, d), F32)],
        compiler_params=_cparams("parallel", "arbitrary"),
        name="ffn",
    )(x, p["norm_ffn"], p["w_ff1"], p["w_ff2"])


def _rope_tables(seq, base):
    inv_freq = 1.0 / (ROPE_THETA ** (jnp.arange(0, 2 * ROPE_HALF, 2, dtype=F32) / (2 * ROPE_HALF)))
    ang = jnp.arange(seq, dtype=F32)[:, None] * inv_freq[None, :]
    cos, sin = jnp.cos(ang), jnp.sin(ang)
    cos_t = jnp.ones((seq, LANES), F32).at[:, base:base + 2 * ROPE_HALF].set(jnp.concatenate([cos, cos], axis=1))
    sin_t = jnp.zeros((seq, LANES), F32).at[:, base:base + 2 * ROPE_HALF].set(jnp.concatenate([-sin, sin], axis=1))
    return cos_t, sin_t


def _pad_lanes(v, n=LANES):
    return jnp.pad(v, (0, n - v.shape[0])).reshape(1, n)


def _prep_layer(w):
    d = D_MODEL
    hk = GLA_HEADS * GLA_DK
    hv = GLA_HEADS * GLA_DV
    dil_w = len(DIL_GROUPS) * DIL_HEADS * DIL_HEAD_DIM
    sizes = (hk, hk, hv, hv, GLA_GATE_RANK, GLA_GATE_RANK, MLA_Q_RANK, MLA_KV_RANK, MLA_ROPE,
             dil_w, dil_w, dil_w, 3 * d)
    offs = [0]
    for s in sizes:
        offs.append(offs[-1] + s)
    (a_q, a_k, a_v, a_r, a_gf, a_gb, b_q, b_kv, b_kpe, c_q, c_k, c_v, gates) = [
        w["w_in"][:, offs[i]:offs[i + 1]] for i in range(len(sizes))]
    z = lambda n: jnp.zeros((d, n), F32)
    small = jnp.concatenate([a_gf, a_gb, z(SMALL_KPE - 2 * GLA_GATE_RANK), b_kpe,
                             z(LANES - SMALL_KPE - MLA_ROPE)], axis=1)
    w_in = jnp.concatenate([c_q, c_k, c_v, gates, a_q, a_k, a_v, a_r, b_q, b_kv, small,
                            z(IN_COLS_PAD - COL_SMALL - LANES)], axis=1).astype(BF16)

    def gate_w(wg, lane0):
        full = jnp.zeros((LANES, hk), F32).at[lane0:lane0 + GLA_GATE_RANK].set(wg)
        return full.astype(BF16), full.T.astype(BF16)

    wgf, wgf_t = gate_w(w["gla_w_gate_f"], SMALL_GF)
    wgb, wgb_t = gate_w(w["gla_w_gate_b"], SMALL_GB)

    w_uq = w["mla_w_uq"].reshape(MLA_Q_RANK, MLA_HEADS, MLA_QK)
    w_uq = jnp.pad(w_uq, ((0, 0), (0, 0), (0, LANES - MLA_QK))).reshape(MLA_Q_RANK, MLA_HEADS * LANES)
    w_ukv = w["mla_w_ukv"].reshape(MLA_KV_RANK, MLA_HEADS, MLA_NOPE + MLA_V)
    w_uk = jnp.pad(w_ukv[:, :, :MLA_NOPE], ((0, 0), (0, 0), (0, LANES - MLA_NOPE)))
    w_uk = w_uk.reshape(MLA_KV_RANK, MLA_HEADS * LANES)
    w_uv = w_ukv[:, :, MLA_NOPE:].reshape(MLA_KV_RANK, MLA_HEADS * MLA_V)
    row = lambda v: v.reshape(1, -1).astype(F32)
    return {
        "norm_mix": row(w["norm_mix"]), "w_in": w_in,
        "gla_wgf": wgf, "gla_wgf_t": wgf_t, "gla_bgf": row(w["gla_b_gate_f"]),
        "gla_bgf_t": w["gla_b_gate_f"].reshape(-1, 1),
        "gla_wgb": wgb, "gla_wgb_t": wgb_t, "gla_bgb": row(w["gla_b_gate_b"]),
        "gla_bgb_t": w["gla_b_gate_b"].reshape(-1, 1),
        "gla_out_norm": row(w["gla_out_norm"]), "w_branch_a": w["w_branch_a"].astype(BF16),
        "mla_q_lat_norm": row(w["mla_q_lat_norm"]), "mla_w_uq": w_uq.astype(BF16),
        "mla_kv_lat_norm": row(w["mla_kv_lat_norm"]), "mla_w_uk": w_uk.astype(BF16),
        "mla_w_uv": w_uv.astype(BF16),
        "mla_q_norm": _pad_lanes(w["mla_q_norm"]), "mla_k_norm": _pad_lanes(w["mla_k_norm"]),
        "w_branch_b": w["w_branch_b"].astype(BF16),
        "dil_qk_norm": jnp.stack([w["dil_q_norm"], w["dil_k_norm"]]).reshape(2, 1, LANES),
        "w_branch_c": w["w_branch_c"].astype(BF16), "w_out": w["w_out"].astype(BF16),
        "norm_xattn": row(w["norm_xattn"]), "norm_mem": row(w["norm_mem"]),
        "w_xq": w["w_xq"].astype(BF16), "w_xkv": w["w_xkv"].astype(BF16),
        "xq_norm": row(w["xq_norm"]), "xk_norm": row(w["xk_norm"]), "w_xo": w["w_xo"].astype(BF16),
        "norm_ffn": row(w["norm_ffn"]), "w_ff1": w["w_ff1"].astype(BF16), "w_ff2": w["w_ff2"].astype(BF16),
    }


def _layer(x, mem, p, tables, batch, seq):
    cos_m, sin_m, cos_d, sin_d = tables
    proj = _norm_matmul(x, p["norm_mix"], p["w_in"], BF16, tm=512, tn=1024)
    o_fwd = _gla_direction(proj, p["gla_wgf"], p["gla_bgf"], p["gla_wgf_t"], p["gla_bgf_t"], batch, seq, False)
    gla_o = _gla_direction(proj, p["gla_wgb"], p["gla_bgb"], p["gla_wgb_t"], p["gla_bgb_t"], batch, seq, True,
                           o_fwd=o_fwd, g_out=p["gla_out_norm"])
    q_m, k_m, v_m = _mla_prep(proj, p, cos_m, sin_m, seq)
    mla_o = _mla_attn(q_m, k_m, v_m, batch, seq)
    qk_d = _dil_prep(proj, p["dil_qk_norm"], cos_d, sin_d, seq)
    dil = [_dil_attn(qk_d, proj, g, batch, seq) for g in range(len(DIL_GROUPS))]
    x = _combine(x, gla_o, mla_o, [o for o, _ in dil], [l for _, l in dil], proj, p)
    kv = _norm_matmul(mem, p["norm_mem"], p["w_xkv"], BF16, tm=N_MEM, tn=1024)
    x = _xattn(x, kv, p, seq)
    return _ffn(x, p)


def _trunk(x, mem, layers):
    batch, seq, d = x.shape
    tables = _rope_tables(seq, MLA_NOPE) + _rope_tables(seq, 0)
    xf = x.reshape(batch * seq, d)
    memf = mem.reshape(batch * mem.shape[1], d)
    for p in layers:
        xf = _layer(xf, memf, p, tables, batch, seq)
    return xf.reshape(batch, seq, d)


def kernel(x_prompt, x_sample, mem_prompt, mem_sample, norm_mix, w_in, gla_w_gate_f, gla_b_gate_f, gla_w_gate_b, gla_b_gate_b, gla_out_norm, w_branch_a, mla_q_lat_norm, mla_w_uq, mla_kv_lat_norm, mla_w_ukv, mla_q_norm, mla_k_norm, w_branch_b, dil_q_norm, dil_k_norm, w_branch_c, w_out, norm_xattn, norm_mem, w_xq, w_xkv, xq_norm, xk_norm, w_xo, norm_ffn, w_ff1, w_ff2):
    stacked = dict(norm_mix=norm_mix, w_in=w_in, gla_w_gate_f=gla_w_gate_f, gla_b_gate_f=gla_b_gate_f,
                   gla_w_gate_b=gla_w_gate_b, gla_b_gate_b=gla_b_gate_b, gla_out_norm=gla_out_norm,
                   w_branch_a=w_branch_a, mla_q_lat_norm=mla_q_lat_norm, mla_w_uq=mla_w_uq,
                   mla_kv_lat_norm=mla_kv_lat_norm, mla_w_ukv=mla_w_ukv, mla_q_norm=mla_q_norm,
                   mla_k_norm=mla_k_norm, w_branch_b=w_branch_b, dil_q_norm=dil_q_norm, dil_k_norm=dil_k_norm,
                   w_branch_c=w_branch_c, w_out=w_out, norm_xattn=norm_xattn, norm_mem=norm_mem, w_xq=w_xq,
                   w_xkv=w_xkv, xq_norm=xq_norm, xk_norm=xk_norm, w_xo=w_xo, norm_ffn=norm_ffn,
                   w_ff1=w_ff1, w_ff2=w_ff2)
    layers = [_prep_layer({k: v[l] for k, v in stacked.items()}) for l in range(DEPTH)]
    return (_trunk(x_prompt, mem_prompt, layers), _trunk(x_sample, mem_sample, layers))
```

```python
import functools

import jax
import jax.numpy as jnp
from jax import lax
from jax.experimental import pallas as pl
from jax.experimental.pallas import tpu as pltpu

F32 = jnp.float32
BF16 = jnp.bfloat16

D_MODEL = 1024
DEPTH = 2
N_MEM = 256
ROPE_THETA = 500000.0
NORM_EPS = 1e-6
GLA_HEADS = 4
GLA_DK = 128
GLA_DV = 256
GLA_GATE_RANK = 16
GLA_TAU = 16.0
GLA_CHUNK = 64
MLA_HEADS = 8
MLA_Q_RANK = 256
MLA_KV_RANK = 128
MLA_NOPE = 64
MLA_ROPE = 32
MLA_QK = 96
MLA_V = 128
DIL_GROUPS = ((128, 1), (512, 4), (2048, 16))
DIL_HEADS = 8
DIL_HEAD_DIM = 128
DIL_ROT = 32
DIL_RADIUS = 64
X_HEADS = 4
X_HEAD_DIM = 256
D_FF = 4096

LANES = 128
ROPE_HALF = 16
VMEM_LIMIT_BYTES = 56 * 1024 * 1024

COL_CQ, COL_CK, COL_CV, COL_GATE = 0, 3072, 6144, 9216
COL_AQ, COL_AK, COL_AV, COL_AR = 12288, 12800, 13312, 14336
COL_BQ, COL_BKV, COL_SMALL = 15360, 15616, 15744
IN_COLS_PAD = 16384
SMALL_GF, SMALL_GB, SMALL_KPE = 0, 16, 64

NT_DIMS = (((1,), (1,)), ((), ()))
LOG2_E = 1.4426950408889634


def _cparams(*sem):
    return pltpu.CompilerParams(dimension_semantics=sem, vmem_limit_bytes=VMEM_LIMIT_BYTES)


def _rms(x, g, n=None):
    ss = jnp.sum(x * x, axis=-1, keepdims=True) * (1.0 / (n or x.shape[-1]))
    return x * lax.rsqrt(ss + NORM_EPS) * g


def _log_sigmoid(z):
    return jnp.minimum(z, 0.0) - jnp.log(1.0 + jnp.exp(-jnp.abs(z)))


def _sigmoid(z):
    return 1.0 / (1.0 + jnp.exp(-z))


def _norm_matmul_kernel(x_ref, g_ref, w_ref, o_ref, h_ref):
    @pl.when(pl.program_id(1) == 0)
    def _():
        h_ref[...] = _rms(x_ref[...].astype(F32), g_ref[...]).astype(BF16)

    o_ref[...] = jnp.dot(h_ref[...], w_ref[...], preferred_element_type=F32).astype(o_ref.dtype)


def _norm_matmul(x, g, w, out_dtype, tm, tn):
    m, k = x.shape
    n = w.shape[1]
    return pl.pallas_call(
        _norm_matmul_kernel,
        grid=(m // tm, n // tn),
        in_specs=[pl.BlockSpec((tm, k), lambda i, j: (i, 0)),
                  pl.BlockSpec((1, k), lambda i, j: (0, 0)),
                  pl.BlockSpec((k, tn), lambda i, j: (0, j))],
        out_specs=pl.BlockSpec((tm, tn), lambda i, j: (i, j)),
        out_shape=jax.ShapeDtypeStruct((m, n), out_dtype),
        scratch_shapes=[pltpu.VMEM((tm, k), BF16)],
        compiler_params=_cparams("parallel", "arbitrary"),
        name="norm_matmul",
    )(x, g, w)


def _gla_kernel(*refs, tb, reverse, final):
    if final:
        (q_ref, k_ref, v_ref, sm_ref, wg_ref, bg_ref, wgt_ref, bgt_ref,
         of_ref, r_ref, gn_ref, o_ref, st_ref) = refs
    else:
        q_ref, k_ref, v_ref, sm_ref, wg_ref, bg_ref, wgt_ref, bgt_ref, o_ref, st_ref = refs

    @pl.when(pl.program_id(1) == 0)
    def _():
        st_ref[...] = jnp.zeros_like(st_ref)

    ck = GLA_CHUNK
    sm = sm_ref[...]
    la = _log_sigmoid(jnp.dot(sm, wg_ref[...], preferred_element_type=F32) + bg_ref[...]) * (1.0 / GLA_TAU)
    lat = _log_sigmoid(lax.dot_general(wgt_ref[...], sm, NT_DIMS, preferred_element_type=F32)
                       + bgt_ref[...]) * (1.0 / GLA_TAU)
    kt = k_ref[...].astype(F32).T

    r = lax.broadcasted_iota(jnp.int32, (ck, ck), 0)
    c = lax.broadcasted_iota(jnp.int32, (ck, ck), 1)
    lower = r >= c
    upper = r <= c
    t_rows = jnp.where(upper if reverse else lower, 1.0, 0.0).astype(F32)
    t_cols = jnp.where(lower if reverse else upper, 1.0, 0.0).astype(F32)
    att_mask = upper if reverse else lower
    end = 0 if reverse else ck - 1
    scale = GLA_DK ** -0.5

    n_chunks = tb // ck
    order = range(n_chunks - 1, -1, -1) if reverse else range(n_chunks)
    for ci in order:
        sl = slice(ci * ck, (ci + 1) * ck)
        b = jnp.dot(t_rows, la[sl], precision=lax.Precision.HIGHEST, preferred_element_type=F32)
        bt = jnp.dot(lat[:, sl], t_cols, precision=lax.Precision.HIGHEST, preferred_element_type=F32)
        bt_end = bt[:, end:end + 1]
        qd = (q_ref[sl, :].astype(F32) * scale * jnp.exp(b)).astype(BF16)
        ktc = kt[:, sl]
        kinv_t = (ktc * jnp.exp(-bt)).astype(BF16)
        ktail_t = (ktc * jnp.exp(bt_end - bt)).astype(BF16)
        dec = jnp.exp(bt_end)
        for h in range(GLA_HEADS):
            ks = slice(h * GLA_DK, (h + 1) * GLA_DK)
            vs = slice(h * GLA_DV, (h + 1) * GLA_DV)
            qh = qd[:, ks]
            att = jnp.dot(qh, kinv_t[ks], preferred_element_type=F32)
            att = jnp.where(att_mask, att, 0.0).astype(BF16)
            vh = v_ref[sl, vs]
            st = st_ref[h]
            o = (jnp.dot(att, vh, preferred_element_type=F32)
                 + jnp.dot(qh, st.astype(BF16), preferred_element_type=F32))
            st_ref[h] = dec[ks] * st + jnp.dot(ktail_t[ks], vh, preferred_element_type=F32)
            if final:
                o = _rms(o + of_ref[sl, vs], gn_ref[...])
                rr = r_ref[sl, vs].astype(F32)
                o = o * (rr * _sigmoid(rr))
            o_ref[sl, vs] = o.astype(o_ref.dtype)


def _gla_direction(proj, wg, bg, wgt, bgt, batch, seq, reverse, o_fwd=None, g_out=None, tb=256):
    nb = seq // tb
    m = batch * seq
    final = o_fwd is not None

    def row(b, i):
        return b * nb + (nb - 1 - i if reverse else i)

    def col(block_cols, off):
        return lambda b, i: (row(b, i), off // block_cols)

    hk = GLA_HEADS * GLA_DK
    hv = GLA_HEADS * GLA_DV
    const = lambda b, i: (0, 0)
    in_specs = [pl.BlockSpec((tb, hk), col(hk, COL_AQ)),
                pl.BlockSpec((tb, hk), col(hk, COL_AK)),
                pl.BlockSpec((tb, hv), col(hv, COL_AV)),
                pl.BlockSpec((tb, LANES), col(LANES, COL_SMALL)),
                pl.BlockSpec((LANES, hk), const),
                pl.BlockSpec((1, hk), const),
                pl.BlockSpec((hk, LANES), const),
                pl.BlockSpec((hk, 1), const)]
    args = [proj, proj, proj, proj, wg, bg, wgt, bgt]
    if final:
        in_specs += [pl.BlockSpec((tb, hv), col(hv, 0)),
                     pl.BlockSpec((tb, hv), col(hv, COL_AR)),
                     pl.BlockSpec((1, GLA_DV), const)]
        args += [o_fwd, proj, g_out]
    return pl.pallas_call(
        functools.partial(_gla_kernel, tb=tb, reverse=reverse, final=final),
        grid=(batch, nb),
        in_specs=in_specs,
        out_specs=pl.BlockSpec((tb, hv), col(hv, 0)),
        out_shape=jax.ShapeDtypeStruct((m, hv), BF16 if final else F32),
        scratch_shapes=[pltpu.VMEM((GLA_HEADS, GLA_DK, GLA_DV), F32)],
        compiler_params=_cparams("parallel", "arbitrary"),
        name="gla_bwd" if reverse else "gla_fwd",
    )(*args)


def _rope(y, cos, sin_signed, lane, base):
    up = pltpu.roll(y, LANES - ROPE_HALF, 1)
    down = pltpu.roll(y, ROPE_HALF, 1)
    partner = jnp.where(lane < base + ROPE_HALF, up, down)
    return y * cos + partner * sin_signed


def _mla_prep_kernel(bq_ref, bkv_ref, sm_ref, qln_ref, wuq_ref, kvln_ref, wuk_ref, wuv_ref,
                     qn_ref, kn_ref, cos_ref, sin_ref, q_out, k_out, v_out):
    hq = _rms(bq_ref[...].astype(F32), qln_ref[...]).astype(BF16)
    hkv = _rms(bkv_ref[...].astype(F32), kvln_ref[...]).astype(BF16)
    q = jnp.dot(hq, wuq_ref[...], preferred_element_type=F32)
    kn = jnp.dot(hkv, wuk_ref[...], preferred_element_type=F32)
    v = jnp.dot(hkv, wuv_ref[...], preferred_element_type=F32)
    tm = q.shape[0]
    lane = lax.broadcasted_iota(jnp.int32, (tm, LANES), 1)
    sm = sm_ref[...].astype(F32)
    kpe = jnp.where(lane >= SMALL_KPE, jnp.where(lane < SMALL_KPE + MLA_ROPE, sm, 0.0), 0.0)
    cos = cos_ref[...]
    sin = sin_ref[...]
    scale = MLA_QK ** -0.5 * LOG2_E
    ones = jnp.ones((tm, LANES), v_out.dtype)
    for h in range(MLA_HEADS):
        hs = slice(h * LANES, (h + 1) * LANES)
        qh = _rope(_rms(q[:, hs], qn_ref[...], MLA_QK), cos, sin, lane, MLA_NOPE)
        q_out[:, hs] = (qh * scale).astype(q_out.dtype)
        kh = _rope(_rms(kn[:, hs] + kpe, kn_ref[...], MLA_QK), cos, sin, lane, MLA_NOPE)
        k_out[:, hs] = kh.astype(k_out.dtype)
        v_out[:, 2 * h * LANES:(2 * h + 1) * LANES] = v[:, hs].astype(v_out.dtype)
        v_out[:, (2 * h + 1) * LANES:(2 * h + 2) * LANES] = ones


def _mla_prep(proj, p, cos, sin, seq, tm=512):
    m = proj.shape[0]
    ns = seq // tm
    const = lambda i: (0, 0)
    wide = MLA_HEADS * LANES
    out = jax.ShapeDtypeStruct((m, wide), BF16)
    return pl.pallas_call(
        _mla_prep_kernel,
        grid=(m // tm,),
        in_specs=[pl.BlockSpec((tm, MLA_Q_RANK), lambda i: (i, COL_BQ // MLA_Q_RANK)),
                  pl.BlockSpec((tm, MLA_KV_RANK), lambda i: (i, COL_BKV // MLA_KV_RANK)),
                  pl.BlockSpec((tm, LANES), lambda i: (i, COL_SMALL // LANES)),
                  pl.BlockSpec((1, MLA_Q_RANK), const),
                  pl.BlockSpec((MLA_Q_RANK, wide), const),
                  pl.BlockSpec((1, MLA_KV_RANK), const),
                  pl.BlockSpec((MLA_KV_RANK, wide), const),
                  pl.BlockSpec((MLA_KV_RANK, wide), const),
                  pl.BlockSpec((1, LANES), const),
                  pl.BlockSpec((1, LANES), const),
                  pl.BlockSpec((tm, LANES), lambda i: (i % ns, 0)),
                  pl.BlockSpec((tm, LANES), lambda i: (i % ns, 0))],
        out_specs=[pl.BlockSpec((tm, wide), lambda i: (i, 0)), pl.BlockSpec((tm, wide), lambda i: (i, 0)),
                   pl.BlockSpec((tm, 2 * wide), lambda i: (i, 0))],
        out_shape=[out, out, jax.ShapeDtypeStruct((m, 2 * wide), BF16)],
        compiler_params=_cparams("parallel"),
        name="mla_prep",
    )(proj, proj, proj, p["mla_q_lat_norm"], p["mla_w_uq"], p["mla_kv_lat_norm"], p["mla_w_uk"],
      p["mla_w_uv"], p["mla_q_norm"], p["mla_k_norm"], cos, sin)


def _mla_attn_kernel(q_ref, k_ref, v_ref, o_ref, *, tk, unroll):
    q = q_ref[...]
    tq = q.shape[0]
    nk = k_ref.shape[0] // tk

    def body(t, carry):
        m, acc = carry
        start = pl.multiple_of(t * tk, tk)
        s = lax.dot_general(q, k_ref[pl.ds(start, tk), :], NT_DIMS, preferred_element_type=F32)
        m_new = jnp.maximum(m, jnp.max(s, axis=-1, keepdims=True))
        p = jnp.exp2(s - m_new).astype(BF16)
        acc = jnp.exp2(m - m_new) * acc + jnp.dot(p, v_ref[pl.ds(start, tk), :], preferred_element_type=F32)
        return m_new, acc

    init = (jnp.full((tq, 1), -jnp.inf, F32), jnp.zeros((tq, 2 * LANES), F32))
    _, acc = lax.fori_loop(0, nk, body, init, unroll=unroll)
    o_ref[...] = (acc[:, :LANES] / acc[:, LANES:]).astype(o_ref.dtype)


def _mla_attn(q, k, v, batch, seq, tq=256, tk=512, max_unroll=8):
    nq = seq // tq
    return pl.pallas_call(
        functools.partial(_mla_attn_kernel, tk=tk, unroll=min(max_unroll, seq // tk)),
        grid=(batch, MLA_HEADS, nq),
        in_specs=[pl.BlockSpec((tq, LANES), lambda b, h, i: (b * nq + i, h)),
                  pl.BlockSpec((seq, LANES), lambda b, h, i: (b, h)),
                  pl.BlockSpec((seq, 2 * LANES), lambda b, h, i: (b, h))],
        out_specs=pl.BlockSpec((tq, LANES), lambda b, h, i: (b * nq + i, h)),
        out_shape=jax.ShapeDtypeStruct(q.shape, BF16),
        compiler_params=_cparams("parallel", "parallel", "arbitrary"),
        name="mla_attn",
    )(q, k, v)


def _dil_prep_kernel(x_ref, nw_ref, cos_ref, sin_ref, o_ref, y_ref, *, dil, scale_q):
    tm = x_ref.shape[0]
    comp = pl.program_id(1)

    @pl.when(comp < 2)
    def _():
        lane = lax.broadcasted_iota(jnp.int32, (tm, LANES), 1)
        cos = cos_ref[...]
        sin = sin_ref[...]
        nw = nw_ref[0]
        scale = jnp.where(comp == 0, scale_q, 1.0).astype(F32)
        for h in range(DIL_HEADS):
            hs = slice(h * LANES, (h + 1) * LANES)
            y_ref[h] = _rope(_rms(x_ref[:, hs].astype(F32), nw), cos, sin, lane, 0) * scale

    @pl.when(comp == 2)
    def _():
        for h in range(DIL_HEADS):
            y_ref[h] = x_ref[:, h * LANES:(h + 1) * LANES].astype(F32)

    rows = tm // dil
    for r in range(dil):
        for h in range(DIL_HEADS):
            src = y_ref[h] if dil == 1 else y_ref[h, pl.ds(r, rows, stride=dil), :]
            o_ref[0, 0, r, :, h * LANES:(h + 1) * LANES] = src.astype(o_ref.dtype)


def _dil_prep(proj, qk_norm, cos, sin, group, batch, seq, tm=512):
    _, dil = DIL_GROUPS[group]
    ns = seq // tm
    wide = DIL_HEADS * DIL_HEAD_DIM
    ng = len(DIL_GROUPS)
    sub_len = seq // dil
    return pl.pallas_call(
        functools.partial(_dil_prep_kernel, dil=dil, scale_q=DIL_HEAD_DIM ** -0.5),
        grid=(batch * ns, 3),
        in_specs=[pl.BlockSpec((tm, wide), lambda i, j: (i, j * ng + group)),
                  pl.BlockSpec((1, 1, LANES), lambda i, j: (jnp.minimum(j, 1), 0, 0)),
                  pl.BlockSpec((tm, LANES), lambda i, j: (i % ns, 0)),
                  pl.BlockSpec((tm, LANES), lambda i, j: (i % ns, 0))],
        out_specs=pl.BlockSpec((1, 1, dil, tm // dil, wide), lambda i, j: (j, i // ns, 0, i % ns, 0)),
        out_shape=jax.ShapeDtypeStruct((3, batch, dil, sub_len, wide), BF16),
        scratch_shapes=[pltpu.VMEM((DIL_HEADS, tm, LANES), F32)],
        compiler_params=_cparams("parallel", "arbitrary"),
        name=f"dil_prep_g{group}",
    )(proj, qk_norm, cos, sin)


def _dil_attn_kernel(q_ref, kl_ref, kc_ref, kr_ref, vl_ref, vc_ref, vr_ref, o_ref, lse_ref, *, sub_len):
    tl = q_ref.shape[0]
    nkeys = tl + 2 * DIL_RADIUS
    row = lax.broadcasted_iota(jnp.int32, (tl, nkeys), 0)
    col = lax.broadcasted_iota(jnp.int32, (tl, nkeys), 1)
    rel = col - row
    pos = pl.program_id(2) * tl - DIL_RADIUS + col
    valid = (rel >= 0) & (rel <= 2 * DIL_RADIUS) & (pos >= 0) & (pos < sub_len)
    lane = lax.broadcasted_iota(jnp.int32, (tl, LANES), 1)
    lse_all = jnp.zeros((tl, LANES), F32)
    for h in range(DIL_HEADS):
        hs = slice(h * LANES, (h + 1) * LANES)
        kk = jnp.concatenate([kl_ref[:, hs], kc_ref[:, hs], kr_ref[:, hs]], axis=0)
        vv = jnp.concatenate([vl_ref[:, hs], vc_ref[:, hs], vr_ref[:, hs]], axis=0)
        s = lax.dot_general(q_ref[:, hs], kk, NT_DIMS, preferred_element_type=F32)
        s = jnp.where(valid, s, -jnp.inf)
        m = jnp.max(s, axis=-1, keepdims=True)
        p = jnp.exp(s - m)
        l = jnp.sum(p, axis=-1, keepdims=True)
        o_ref[:, hs] = jnp.dot(p.astype(BF16), vv, preferred_element_type=F32) / l
        lse_all = jnp.where(lane == h, m + jnp.log(l), lse_all)
    lse_ref[...] = lse_all


def _dil_attn(qkv, group, batch, seq, tl=128):
    _, dil = DIL_GROUPS[group]
    sub_len = seq // dil
    nl = sub_len // tl
    n_sub = batch * dil
    wide = DIL_HEADS * DIL_HEAD_DIM
    halo = DIL_RADIUS
    per = tl // halo
    flat = qkv.reshape(3 * n_sub * sub_len, wide)
    n_halo_blocks = flat.shape[0] // halo

    def tile(comp):
        return lambda b, r, i: (((comp * batch + b) * dil + r) * nl + i, 0)

    def left(comp):
        return lambda b, r, i: (jnp.maximum(tile(comp)(b, r, i)[0] * per - 1, 0), 0)

    def right(comp):
        return lambda b, r, i: (jnp.minimum((tile(comp)(b, r, i)[0] + 1) * per, n_halo_blocks - 1), 0)

    return pl.pallas_call(
        functools.partial(_dil_attn_kernel, sub_len=sub_len),
        grid=(batch, dil, nl),
        in_specs=[pl.BlockSpec((tl, wide), tile(0)),
                  pl.BlockSpec((halo, wide), left(1)),
                  pl.BlockSpec((tl, wide), tile(1)),
                  pl.BlockSpec((halo, wide), right(1)),
                  pl.BlockSpec((halo, wide), left(2)),
                  pl.BlockSpec((tl, wide), tile(2)),
                  pl.BlockSpec((halo, wide), right(2))],
        out_specs=[pl.BlockSpec((tl, wide), tile(0)),
                   pl.BlockSpec((tl, LANES), tile(0))],
        out_shape=[jax.ShapeDtypeStruct((n_sub * sub_len, wide), F32),
                   jax.ShapeDtypeStruct((n_sub * sub_len, LANES), F32)],
        compiler_params=_cparams("parallel", "parallel", "arbitrary"),
        name=f"dil_attn_g{group}",
    )(flat, flat, flat, flat, flat, flat, flat)


def _combine_kernel(x_ref, a_ref, b_ref, o0_ref, o1_ref, o2_ref, l0_ref, l1_ref, l2_ref, gate_ref,
                    wa_ref, wb_ref, wc_ref, wo_ref, out_ref, c_ref, on_refs, ln_refs):
    tm = x_ref.shape[0]
    o_tok, l_tok = [], []
    for g, (o_ref, l_ref) in enumerate(((o0_ref, l0_ref), (o1_ref, l1_ref), (o2_ref, l2_ref))):
        dil = DIL_GROUPS[g][1]
        if dil == 1:
            o_tok.append(lambda h, o_ref=o_ref: o_ref[0, 0, :, h * LANES:(h + 1) * LANES])
            l_tok.append(l_ref[0, 0])
            continue
        for r in range(dil):
            rows = pl.ds(r, tm // dil, stride=dil)
            ln_refs[g - 1, rows, :] = l_ref[0, r]
            for h in range(DIL_HEADS):
                on_refs[g - 1, h, rows, :] = o_ref[0, r, :, h * LANES:(h + 1) * LANES]
        o_tok.append(lambda h, g=g: on_refs[g - 1, h])
        l_tok.append(ln_refs[g - 1])
    l0, l1, l2 = l_tok
    mx = jnp.maximum(jnp.maximum(l0, l1), l2)
    e0, e1, e2 = jnp.exp(l0 - mx), jnp.exp(l1 - mx), jnp.exp(l2 - mx)
    inv = 1.0 / (e0 + e1 + e2)
    w0, w1, w2 = e0 * inv, e1 * inv, e2 * inv
    for h in range(DIL_HEADS):
        hs = slice(h * LANES, (h + 1) * LANES)
        c = w0[:, h:h + 1] * o_tok[0](h) + w1[:, h:h + 1] * o_tok[1](h) + w2[:, h:h + 1] * o_tok[2](h)
        c_ref[:, hs] = c.astype(BF16)
    ya = jnp.dot(a_ref[...], wa_ref[...], preferred_element_type=F32)
    yb = jnp.dot(b_ref[...], wb_ref[...], preferred_element_type=F32)
    yc = jnp.dot(c_ref[...], wc_ref[...], preferred_element_type=F32)
    d = D_MODEL
    mix = (_sigmoid(gate_ref[:, 0:d].astype(F32)) * ya
           + _sigmoid(gate_ref[:, d:2 * d].astype(F32)) * yb
           + _sigmoid(gate_ref[:, 2 * d:3 * d].astype(F32)) * yc)
    out_ref[...] = x_ref[...] + jnp.dot(mix.astype(BF16), wo_ref[...], preferred_element_type=F32)


def _combine(x, gla_o, mla_o, dil_o, dil_lse, proj, p, batch, seq, tm=256):
    m = x.shape[0]
    d = D_MODEL
    ns = seq // tm
    rowblk = lambda i: (i, 0)
    const = lambda i: (0, 0)
    wspec = pl.BlockSpec((d, d), const)

    def residue_major(arr, group, width):
        dil = DIL_GROUPS[group][1]
        view = arr.reshape(batch, dil, seq // dil, width)
        return view, pl.BlockSpec((1, dil, tm // dil, width), lambda i: (i // ns, 0, i % ns, 0))

    o_views, o_specs = zip(*[residue_major(dil_o[g], g, d) for g in range(len(DIL_GROUPS))])
    l_views, l_specs = zip(*[residue_major(dil_lse[g], g, LANES) for g in range(len(DIL_GROUPS))])
    n_perm = len(DIL_GROUPS) - 1
    return pl.pallas_call(
        _combine_kernel,
        grid=(m // tm,),
        in_specs=[pl.BlockSpec((tm, d), rowblk), pl.BlockSpec((tm, d), rowblk), pl.BlockSpec((tm, d), rowblk),
                  *o_specs, *l_specs,
                  pl.BlockSpec((tm, 3 * d), lambda i: (i, COL_GATE // (3 * d))),
                  wspec, wspec, wspec, wspec],
        out_specs=pl.BlockSpec((tm, d), rowblk),
        out_shape=jax.ShapeDtypeStruct((m, d), F32),
        scratch_shapes=[pltpu.VMEM((tm, d), BF16), pltpu.VMEM((n_perm, DIL_HEADS, tm, LANES), F32),
                        pltpu.VMEM((n_perm, tm, LANES), F32)],
        compiler_params=_cparams("parallel"),
        name="combine",
    )(x, gla_o, mla_o, *o_views, *l_views, proj,
      p["w_branch_a"], p["w_branch_b"], p["w_branch_c"], p["w_out"])


def _xattn_kernel(x_ref, g_ref, wq_ref, qn_ref, kv_ref, kn_ref, wo_ref, out_ref, o_ref):
    x = x_ref[...]
    h = _rms(x, g_ref[...]).astype(BF16)
    q = jnp.dot(h, wq_ref[...], preferred_element_type=F32)
    scale = X_HEAD_DIM ** -0.5
    for hd in range(X_HEADS):
        hs = slice(hd * X_HEAD_DIM, (hd + 1) * X_HEAD_DIM)
        vs = slice(D_MODEL + hd * X_HEAD_DIM, D_MODEL + (hd + 1) * X_HEAD_DIM)
        qh = (_rms(q[:, hs], qn_ref[...]) * scale).astype(BF16)
        kh = _rms(kv_ref[:, hs].astype(F32), kn_ref[...]).astype(BF16)
        s = lax.dot_general(qh, kh, NT_DIMS, preferred_element_type=F32)
        p = jnp.exp(s - jnp.max(s, axis=-1, keepdims=True))
        l = jnp.sum(p, axis=-1, keepdims=True)
        o = jnp.dot(p.astype(BF16), kv_ref[:, vs], preferred_element_type=F32) / l
        o_ref[:, hs] = o.astype(BF16)
    out_ref[...] = x + jnp.dot(o_ref[...], wo_ref[...], preferred_element_type=F32)


def _xattn(x, kv, p, seq, tm=512):
    m = x.shape[0]
    d = D_MODEL
    per_seq = seq // tm
    const = lambda i: (0, 0)
    return pl.pallas_call(
        _xattn_kernel,
        grid=(m // tm,),
        in_specs=[pl.BlockSpec((tm, d), lambda i: (i, 0)),
                  pl.BlockSpec((1, d), const),
                  pl.BlockSpec((d, d), const),
                  pl.BlockSpec((1, X_HEAD_DIM), const),
                  pl.BlockSpec((N_MEM, 2 * d), lambda i: (i // per_seq, 0)),
                  pl.BlockSpec((1, X_HEAD_DIM), const),
                  pl.BlockSpec((d, d), const)],
        out_specs=pl.BlockSpec((tm, d), lambda i: (i, 0)),
        out_shape=jax.ShapeDtypeStruct((m, d), F32),
        scratch_shapes=[pltpu.VMEM((tm, d), BF16)],
        compiler_params=_cparams("parallel"),
        name="xattn",
    )(x, p["norm_xattn"], p["w_xq"], p["xq_norm"], kv, p["xk_norm"], p["w_xo"])


def _ffn_kernel(x_ref, g_ref, w1_ref, w2_ref, out_ref, h_ref, acc_ref):
    j = pl.program_id(1)

    @pl.when(j == 0)
    def _():
        h_ref[...] = _rms(x_ref[...], g_ref[...]).astype(BF16)
        acc_ref[...] = x_ref[...]

    u = jnp.maximum(jnp.dot(h_ref[...], w1_ref[...], preferred_element_type=F32), 0.0)
    acc_ref[...] += jnp.dot((u * u).astype(BF16), w2_ref[...], preferred_element_type=F32)

    @pl.when(j == pl.num_programs(1) - 1)
    def _():
        out_ref[...] = acc_ref[...]


def _ffn(x, p, tm=1024, tf=1024):
    m = x.shape[0]
    d = D_MODEL
    return pl.pallas_call(
        _ffn_kernel,
        grid=(m // tm, D_FF // tf),
        in_specs=[pl.BlockSpec((tm, d), lambda i, j: (i, 0)),
                  pl.BlockSpec((1, d), lambda i, j: (0, 0)),
                  pl.BlockSpec((d, tf), lambda i, j: (0, j)),
                  pl.BlockSpec((tf, d), lambda i, j: (j, 0))],
        out_specs=pl.BlockSpec((tm, d), lambda i, j: (i, 0)),
        out_shape=jax.ShapeDtypeStruct((m, d), F32),
        scratch_shapes=[pltpu.VMEM((tm, d), BF16), pltpu.VMEM((tm, d), F32)],
        compiler_params=_cparams("parallel", "arbitrary"),
        name="ffn",
    )(x, p["norm_ffn"], p["w_ff1"], p["w_ff2"])


def _rope_tables(seq, base):
    inv_freq = 1.0 / (ROPE_THETA ** (jnp.arange(0, 2 * ROPE_HALF, 2, dtype=F32) / (2 * ROPE_HALF)))
    ang = jnp.arange(seq, dtype=F32)[:, None] * inv_freq[None, :]
    cos, sin = jnp.cos(ang), jnp.sin(ang)
    cos_t = jnp.ones((seq, LANES), F32).at[:, base:base + 2 * ROPE_HALF].set(jnp.concatenate([cos, cos], axis=1))
    sin_t = jnp.zeros((seq, LANES), F32).at[:, base:base + 2 * ROPE_HALF].set(jnp.concatenate([-sin, sin], axis=1))
    return cos_t, sin_t


def _pad_lanes(v, n=LANES):
    return jnp.pad(v, (0, n - v.shape[0])).reshape(1, n)


def _prep_layer(w):
    d = D_MODEL
    hk = GLA_HEADS * GLA_DK
    hv = GLA_HEADS * GLA_DV
    dil_w = len(DIL_GROUPS) * DIL_HEADS * DIL_HEAD_DIM
    sizes = (hk, hk, hv, hv, GLA_GATE_RANK, GLA_GATE_RANK, MLA_Q_RANK, MLA_KV_RANK, MLA_ROPE,
             dil_w, dil_w, dil_w, 3 * d)
    offs = [0]
    for s in sizes:
        offs.append(offs[-1] + s)
    (a_q, a_k, a_v, a_r, a_gf, a_gb, b_q, b_kv, b_kpe, c_q, c_k, c_v, gates) = [
        w["w_in"][:, offs[i]:offs[i + 1]] for i in range(len(sizes))]
    z = lambda n: jnp.zeros((d, n), F32)
    small = jnp.concatenate([a_gf, a_gb, z(SMALL_KPE - 2 * GLA_GATE_RANK), b_kpe,
                             z(LANES - SMALL_KPE - MLA_ROPE)], axis=1)
    w_in = jnp.concatenate([c_q, c_k, c_v, gates, a_q, a_k, a_v, a_r, b_q, b_kv, small,
                            z(IN_COLS_PAD - COL_SMALL - LANES)], axis=1).astype(BF16)

    def gate_w(wg, lane0):
        full = jnp.zeros((LANES, hk), F32).at[lane0:lane0 + GLA_GATE_RANK].set(wg)
        return full.astype(BF16), full.T.astype(BF16)

    wgf, wgf_t = gate_w(w["gla_w_gate_f"], SMALL_GF)
    wgb, wgb_t = gate_w(w["gla_w_gate_b"], SMALL_GB)

    w_uq = w["mla_w_uq"].reshape(MLA_Q_RANK, MLA_HEADS, MLA_QK)
    w_uq = jnp.pad(w_uq, ((0, 0), (0, 0), (0, LANES - MLA_QK))).reshape(MLA_Q_RANK, MLA_HEADS * LANES)
    w_ukv = w["mla_w_ukv"].reshape(MLA_KV_RANK, MLA_HEADS, MLA_NOPE + MLA_V)
    w_uk = jnp.pad(w_ukv[:, :, :MLA_NOPE], ((0, 0), (0, 0), (0, LANES - MLA_NOPE)))
    w_uk = w_uk.reshape(MLA_KV_RANK, MLA_HEADS * LANES)
    w_uv = w_ukv[:, :, MLA_NOPE:].reshape(MLA_KV_RANK, MLA_HEADS * MLA_V)
    row = lambda v: v.reshape(1, -1).astype(F32)
    return {
        "norm_mix": row(w["norm_mix"]), "w_in": w_in,
        "gla_wgf": wgf, "gla_wgf_t": wgf_t, "gla_bgf": row(w["gla_b_gate_f"]),
        "gla_bgf_t": w["gla_b_gate_f"].reshape(-1, 1),
        "gla_wgb": wgb, "gla_wgb_t": wgb_t, "gla_bgb": row(w["gla_b_gate_b"]),
        "gla_bgb_t": w["gla_b_gate_b"].reshape(-1, 1),
        "gla_out_norm": row(w["gla_out_norm"]), "w_branch_a": w["w_branch_a"].astype(BF16),
        "mla_q_lat_norm": row(w["mla_q_lat_norm"]), "mla_w_uq": w_uq.astype(BF16),
        "mla_kv_lat_norm": row(w["mla_kv_lat_norm"]), "mla_w_uk": w_uk.astype(BF16),
        "mla_w_uv": w_uv.astype(BF16),
        "mla_q_norm": _pad_lanes(w["mla_q_norm"]), "mla_k_norm": _pad_lanes(w["mla_k_norm"]),
        "w_branch_b": w["w_branch_b"].astype(BF16),
        "dil_qk_norm": jnp.stack([w["dil_q_norm"], w["dil_k_norm"]]).reshape(2, 1, LANES),
        "w_branch_c": w["w_branch_c"].astype(BF16), "w_out": w["w_out"].astype(BF16),
        "norm_xattn": row(w["norm_xattn"]), "norm_mem": row(w["norm_mem"]),
        "w_xq": w["w_xq"].astype(BF16), "w_xkv": w["w_xkv"].astype(BF16),
        "xq_norm": row(w["xq_norm"]), "xk_norm": row(w["xk_norm"]), "w_xo": w["w_xo"].astype(BF16),
        "norm_ffn": row(w["norm_ffn"]), "w_ff1": w["w_ff1"].astype(BF16), "w_ff2": w["w_ff2"].astype(BF16),
    }


def _layer(x, mem, p, tables, batch, seq):
    cos_m, sin_m, cos_d, sin_d = tables
    proj = _norm_matmul(x, p["norm_mix"], p["w_in"], BF16, tm=1024, tn=1024)
    o_fwd = _gla_direction(proj, p["gla_wgf"], p["gla_bgf"], p["gla_wgf_t"], p["gla_bgf_t"], batch, seq, False)
    gla_o = _gla_direction(proj, p["gla_wgb"], p["gla_bgb"], p["gla_wgb_t"], p["gla_bgb_t"], batch, seq, True,
                           o_fwd=o_fwd, g_out=p["gla_out_norm"])
    q_m, k_m, v_m = _mla_prep(proj, p, cos_m, sin_m, seq)
    mla_o = _mla_attn(q_m, k_m, v_m, batch, seq)
    dil = [_dil_attn(_dil_prep(proj, p["dil_qk_norm"], cos_d, sin_d, g, batch, seq), g, batch, seq)
           for g in range(len(DIL_GROUPS))]
    x = _combine(x, gla_o, mla_o, [o for o, _ in dil], [l for _, l in dil], proj, p, batch, seq)
    kv = _norm_matmul(mem, p["norm_mem"], p["w_xkv"], BF16, tm=N_MEM, tn=1024)
    x = _xattn(x, kv, p, seq)
    return _ffn(x, p)


def _trunk(x, mem, layers):
    batch, seq, d = x.shape
    tables = _rope_tables(seq, MLA_NOPE) + _rope_tables(seq, 0)
    xf = x.reshape(batch * seq, d)
    memf = mem.reshape(batch * mem.shape[1], d)
    for p in layers:
        xf = _layer(xf, memf, p, tables, batch, seq)
    return xf.reshape(batch, seq, d)


def kernel(x_prompt, x_sample, mem_prompt, mem_sample, norm_mix, w_in, gla_w_gate_f, gla_b_gate_f, gla_w_gate_b, gla_b_gate_b, gla_out_norm, w_branch_a, mla_q_lat_norm, mla_w_uq, mla_kv_lat_norm, mla_w_ukv, mla_q_norm, mla_k_norm, w_branch_b, dil_q_norm, dil_k_norm, w_branch_c, w_out, norm_xattn, norm_mem, w_xq, w_xkv, xq_norm, xk_norm, w_xo, norm_ffn, w_ff1, w_ff2):
    stacked = dict(norm_mix=norm_mix, w_in=w_in, gla_w_gate_f=gla_w_gate_f, gla_b_gate_f=gla_b_gate_f,
                   gla_w_gate_b=gla_w_gate_b, gla_b_gate_b=gla_b_gate_b, gla_out_norm=gla_out_norm,
                   w_branch_a=w_branch_a, mla_q_lat_norm=mla_q_lat_norm, mla_w_uq=mla_w_uq,
                   mla_kv_lat_norm=mla_kv_lat_norm, mla_w_ukv=mla_w_ukv, mla_q_norm=mla_q_norm,
                   mla_k_norm=mla_k_norm, w_branch_b=w_branch_b, dil_q_norm=dil_q_norm, dil_k_norm=dil_k_norm,
                   w_branch_c=w_branch_c, w_out=w_out, norm_xattn=norm_xattn, norm_mem=norm_mem, w_xq=w_xq,
                   w_xkv=w_xkv, xq_norm=xq_norm, xk_norm=xk_norm, w_xo=w_xo, norm_ffn=norm_ffn,
                   w_ff1=w_ff1, w_ff2=w_ff2)
    layers = [_prep_layer({k: v[l] for k, v in stacked.items()}) for l in range(DEPTH)]
    return (_trunk(x_prompt, mem_prompt, layers), _trunk(x_sample, mem_sample, layers))
```

```python
import functools

import jax
import jax.numpy as jnp
import numpy as np
from jax import lax
from jax.experimental import pallas as pl
from jax.experimental.pallas import tpu as pltpu

F32 = jnp.float32
BF16 = jnp.bfloat16

D_MODEL = 1024
DEPTH = 2
N_MEM = 256
ROPE_THETA = 500000.0
NORM_EPS = 1e-6
GLA_HEADS = 4
GLA_DK = 128
GLA_DV = 256
GLA_GATE_RANK = 16
GLA_TAU = 16.0
GLA_CHUNK = 64
MLA_HEADS = 8
MLA_Q_RANK = 256
MLA_KV_RANK = 128
MLA_NOPE = 64
MLA_ROPE = 32
MLA_QK = 96
MLA_V = 128
DIL_GROUPS = ((128, 1), (512, 4), (2048, 16))
DIL_HEADS = 8
DIL_HEAD_DIM = 128
DIL_ROT = 32
DIL_RADIUS = 64
X_HEADS = 4
X_HEAD_DIM = 256
D_FF = 4096

LANES = 128
ROPE_HALF = 16
VMEM_LIMIT_BYTES = 56 * 1024 * 1024

COL_CQ, COL_CK, COL_CV, COL_GATE = 0, 3072, 6144, 9216
COL_AQ, COL_AK, COL_AV, COL_AR = 12288, 12800, 13312, 14336
COL_BQ, COL_BKV, COL_SMALL = 15360, 15616, 15744
IN_COLS_PAD = 16384
SMALL_GF, SMALL_GB = 16, 32
ROPE_LO, ROPE_HI = 0, 64

NT_DIMS = (((1,), (1,)), ((), ()))
LOG2_E = 1.4426950408889634


def _cparams(*sem):
    return pltpu.CompilerParams(dimension_semantics=sem, vmem_limit_bytes=VMEM_LIMIT_BYTES)


def _rms(x, g, n=None):
    ss = jnp.sum(x * x, axis=-1, keepdims=True) * (1.0 / (n or x.shape[-1]))
    return x * lax.rsqrt(ss + NORM_EPS) * g


def _log_sigmoid(z):
    return jnp.minimum(z, 0.0) - jnp.log(1.0 + jnp.exp(-jnp.abs(z)))


def _sigmoid(z):
    return 1.0 / (1.0 + jnp.exp(-z))


def _norm_matmul_kernel(x_ref, g_ref, w_ref, o_ref, h_ref):
    @pl.when(pl.program_id(1) == 0)
    def _():
        h_ref[...] = _rms(x_ref[...].astype(F32), g_ref[...]).astype(BF16)

    o_ref[...] = jnp.dot(h_ref[...], w_ref[...], preferred_element_type=F32).astype(o_ref.dtype)


def _norm_matmul(x, g, w, out_dtype, tm, tn):
    m, k = x.shape
    n = w.shape[1]
    return pl.pallas_call(
        _norm_matmul_kernel,
        grid=(m // tm, n // tn),
        in_specs=[pl.BlockSpec((tm, k), lambda i, j: (i, 0)),
                  pl.BlockSpec((1, k), lambda i, j: (0, 0)),
                  pl.BlockSpec((k, tn), lambda i, j: (0, j))],
        out_specs=pl.BlockSpec((tm, tn), lambda i, j: (i, j)),
        out_shape=jax.ShapeDtypeStruct((m, n), out_dtype),
        scratch_shapes=[pltpu.VMEM((tm, k), BF16)],
        compiler_params=_cparams("parallel", "arbitrary"),
        name="norm_matmul",
    )(x, g, w)


TN_DIMS = (((0,), (0,)), ((), ()))


def _gla_kernel(*refs, tb, reverse, final):
    if final:
        q_ref, k_ref, v_ref, sm_ref, wg_ref, bg_ref, of_ref, r_ref, gn_ref, o_ref, st_ref = refs
    else:
        q_ref, k_ref, v_ref, sm_ref, wg_ref, bg_ref, o_ref, st_ref = refs

    @pl.when(pl.program_id(1) == 0)
    def _():
        st_ref[...] = jnp.zeros_like(st_ref)

    ck = GLA_CHUNK
    la = _log_sigmoid(jnp.dot(sm_ref[...], wg_ref[...], preferred_element_type=F32) + bg_ref[...]) * (1.0 / GLA_TAU)

    r = lax.broadcasted_iota(jnp.int32, (ck, ck), 0)
    c = lax.broadcasted_iota(jnp.int32, (ck, ck), 1)
    tri_mask = (r <= c) if reverse else (r >= c)
    tri = jnp.where(tri_mask, 1.0, 0.0).astype(BF16)
    end = 0 if reverse else ck - 1
    scale = GLA_DK ** -0.5

    n_chunks = tb // ck
    order = range(n_chunks - 1, -1, -1) if reverse else range(n_chunks)
    for ci in order:
        sl = slice(ci * ck, (ci + 1) * ck)
        la_c = la[sl]
        hi = la_c.astype(BF16)
        lo = (la_c - hi.astype(F32)).astype(BF16)
        b = jnp.dot(tri, hi, preferred_element_type=F32) + jnp.dot(tri, lo, preferred_element_type=F32)
        b_end = b[end:end + 1, :]
        kc = k_ref[sl, :].astype(F32)
        qd = (q_ref[sl, :].astype(F32) * scale * jnp.exp(b)).astype(BF16)
        kinv = (kc * jnp.exp(-b)).astype(BF16)
        ktail = (kc * jnp.exp(b_end - b)).astype(BF16)
        dec = jnp.exp(b_end)
        for h in range(GLA_HEADS):
            ks = slice(h * GLA_DK, (h + 1) * GLA_DK)
            vs = slice(h * GLA_DV, (h + 1) * GLA_DV)
            qh = qd[:, ks]
            att = lax.dot_general(qh, kinv[:, ks], NT_DIMS, preferred_element_type=F32)
            att = jnp.where(tri_mask, att, 0.0).astype(BF16)
            vh = v_ref[sl, vs]
            st = st_ref[h]
            o = (jnp.dot(att, vh, preferred_element_type=F32)
                 + lax.dot_general(qh, st.astype(BF16), NT_DIMS, preferred_element_type=F32))
            st_ref[h] = dec[:, ks] * st + lax.dot_general(vh, ktail[:, ks], TN_DIMS, preferred_element_type=F32)
            if final:
                o = _rms(o + of_ref[sl, vs], gn_ref[...])
                rr = r_ref[sl, vs].astype(F32)
                o = o * (rr * _sigmoid(rr))
            o_ref[sl, vs] = o.astype(o_ref.dtype)


def _gla_direction(proj, wg, bg, batch, seq, reverse, o_fwd=None, g_out=None, tb=256):
    nb = seq // tb
    m = batch * seq
    final = o_fwd is not None

    def row(b, i):
        return b * nb + (nb - 1 - i if reverse else i)

    def col(block_cols, off):
        return lambda b, i: (row(b, i), off // block_cols)

    hk = GLA_HEADS * GLA_DK
    hv = GLA_HEADS * GLA_DV
    const = lambda b, i: (0, 0)
    in_specs = [pl.BlockSpec((tb, hk), col(hk, COL_AQ)),
                pl.BlockSpec((tb, hk), col(hk, COL_AK)),
                pl.BlockSpec((tb, hv), col(hv, COL_AV)),
                pl.BlockSpec((tb, LANES), col(LANES, COL_SMALL)),
                pl.BlockSpec((LANES, hk), const),
                pl.BlockSpec((1, hk), const)]
    args = [proj, proj, proj, proj, wg, bg]
    if final:
        in_specs += [pl.BlockSpec((tb, hv), col(hv, 0)),
                     pl.BlockSpec((tb, hv), col(hv, COL_AR)),
                     pl.BlockSpec((1, GLA_DV), const)]
        args += [o_fwd, proj, g_out]
    return pl.pallas_call(
        functools.partial(_gla_kernel, tb=tb, reverse=reverse, final=final),
        grid=(batch, nb),
        in_specs=in_specs,
        out_specs=pl.BlockSpec((tb, hv), col(hv, 0)),
        out_shape=jax.ShapeDtypeStruct((m, hv), BF16 if final else F32),
        scratch_shapes=[pltpu.VMEM((GLA_HEADS, GLA_DV, GLA_DK), F32)],
        compiler_params=_cparams("parallel", "arbitrary"),
        name="gla_bwd" if reverse else "gla_fwd",
    )(*args)


def _rope(y, cos, sin_signed):
    return y * cos + pltpu.roll(y, LANES // 2, 1) * sin_signed


def _mla_prep_kernel(bq_ref, bkv_ref, sm_ref, qln_ref, wuq_ref, kvln_ref, wuk_ref, wuv_ref,
                     qn_ref, kn_ref, cos_ref, sin_ref, q_out, k_out, v_out):
    hq = _rms(bq_ref[...].astype(F32), qln_ref[...]).astype(BF16)
    hkv = _rms(bkv_ref[...].astype(F32), kvln_ref[...]).astype(BF16)
    q = jnp.dot(hq, wuq_ref[...], preferred_element_type=F32)
    kn = jnp.dot(hkv, wuk_ref[...], preferred_element_type=F32)
    v = jnp.dot(hkv, wuv_ref[...], preferred_element_type=F32)
    tm = q.shape[0]
    lane = lax.broadcasted_iota(jnp.int32, (tm, LANES), 1)
    sm = sm_ref[...].astype(F32)
    in_rope = (lane < ROPE_LO + ROPE_HALF) | ((lane >= ROPE_HI) & (lane < ROPE_HI + ROPE_HALF))
    kpe = jnp.where(in_rope, sm, 0.0)
    cos = cos_ref[...]
    sin = sin_ref[...]
    scale = MLA_QK ** -0.5 * LOG2_E
    ones = jnp.ones((tm, LANES), v_out.dtype)
    for h in range(MLA_HEADS):
        hs = slice(h * LANES, (h + 1) * LANES)
        qh = _rope(_rms(q[:, hs], qn_ref[...], MLA_QK), cos, sin)
        q_out[:, hs] = (qh * scale).astype(q_out.dtype)
        kh = _rope(_rms(kn[:, hs] + kpe, kn_ref[...], MLA_QK), cos, sin)
        k_out[:, hs] = kh.astype(k_out.dtype)
        v_out[:, 2 * h * LANES:(2 * h + 1) * LANES] = v[:, hs].astype(v_out.dtype)
        v_out[:, (2 * h + 1) * LANES:(2 * h + 2) * LANES] = ones


def _mla_prep(proj, p, cos, sin, seq, tm=512):
    m = proj.shape[0]
    ns = seq // tm
    const = lambda i: (0, 0)
    wide = MLA_HEADS * LANES
    out = jax.ShapeDtypeStruct((m, wide), BF16)
    return pl.pallas_call(
        _mla_prep_kernel,
        grid=(m // tm,),
        in_specs=[pl.BlockSpec((tm, MLA_Q_RANK), lambda i: (i, COL_BQ // MLA_Q_RANK)),
                  pl.BlockSpec((tm, MLA_KV_RANK), lambda i: (i, COL_BKV // MLA_KV_RANK)),
                  pl.BlockSpec((tm, LANES), lambda i: (i, COL_SMALL // LANES)),
                  pl.BlockSpec((1, MLA_Q_RANK), const),
                  pl.BlockSpec((MLA_Q_RANK, wide), const),
                  pl.BlockSpec((1, MLA_KV_RANK), const),
                  pl.BlockSpec((MLA_KV_RANK, wide), const),
                  pl.BlockSpec((MLA_KV_RANK, wide), const),
                  pl.BlockSpec((1, LANES), const),
                  pl.BlockSpec((1, LANES), const),
                  pl.BlockSpec((tm, LANES), lambda i: (i % ns, 0)),
                  pl.BlockSpec((tm, LANES), lambda i: (i % ns, 0))],
        out_specs=[pl.BlockSpec((tm, wide), lambda i: (i, 0)), pl.BlockSpec((tm, wide), lambda i: (i, 0)),
                   pl.BlockSpec((tm, 2 * wide), lambda i: (i, 0))],
        out_shape=[out, out, jax.ShapeDtypeStruct((m, 2 * wide), BF16)],
        compiler_params=_cparams("parallel"),
        name="mla_prep",
    )(proj, proj, proj, p["mla_q_lat_norm"], p["mla_w_uq"], p["mla_kv_lat_norm"], p["mla_w_uk"],
      p["mla_w_uv"], p["mla_q_norm"], p["mla_k_norm"], cos, sin)


def _mla_attn_kernel(q_ref, k_ref, v_ref, o_ref, *, tk, unroll):
    q = q_ref[...]
    tq = q.shape[0]
    nk = k_ref.shape[0] // tk

    def body(t, carry):
        m, acc = carry
        start = pl.multiple_of(t * tk, tk)
        s = lax.dot_general(q, k_ref[pl.ds(start, tk), :], NT_DIMS, preferred_element_type=F32)
        m_new = jnp.maximum(m, jnp.max(s, axis=-1, keepdims=True))
        p = jnp.exp2(s - m_new).astype(BF16)
        acc = jnp.exp2(m - m_new) * acc + jnp.dot(p, v_ref[pl.ds(start, tk), :], preferred_element_type=F32)
        return m_new, acc

    init = (jnp.full((tq, 1), -jnp.inf, F32), jnp.zeros((tq, 2 * LANES), F32))
    _, acc = lax.fori_loop(0, nk, body, init, unroll=unroll)
    o_ref[...] = (acc[:, :LANES] / acc[:, LANES:]).astype(o_ref.dtype)


def _mla_attn(q, k, v, batch, seq, tq=256, tk=512, max_unroll=16):
    nq = seq // tq
    return pl.pallas_call(
        functools.partial(_mla_attn_kernel, tk=tk, unroll=min(max_unroll, seq // tk)),
        grid=(batch, MLA_HEADS, nq),
        in_specs=[pl.BlockSpec((tq, LANES), lambda b, h, i: (b * nq + i, h)),
                  pl.BlockSpec((seq, LANES), lambda b, h, i: (b, h)),
                  pl.BlockSpec((seq, 2 * LANES), lambda b, h, i: (b, h))],
        out_specs=pl.BlockSpec((tq, LANES), lambda b, h, i: (b * nq + i, h)),
        out_shape=jax.ShapeDtypeStruct(q.shape, BF16),
        compiler_params=_cparams("parallel", "parallel", "arbitrary"),
        name="mla_attn",
    )(q, k, v)


def _dil_prep_kernel(x_ref, nw_ref, cos_ref, sin_ref, o_ref, y_ref, *, dil, scale_q):
    tm = x_ref.shape[0]
    comp = pl.program_id(1)

    @pl.when(comp < 2)
    def _():
        cos = cos_ref[...]
        sin = sin_ref[...]
        nw = nw_ref[0]
        scale = jnp.where(comp == 0, scale_q, 1.0).astype(F32)
        for h in range(DIL_HEADS):
            hs = slice(h * LANES, (h + 1) * LANES)
            y_ref[h] = _rope(_rms(x_ref[:, hs].astype(F32), nw), cos, sin) * scale

    @pl.when(comp == 2)
    def _():
        for h in range(DIL_HEADS):
            y_ref[h] = x_ref[:, h * LANES:(h + 1) * LANES].astype(F32)

    rows = tm // dil
    for r in range(dil):
        for h in range(DIL_HEADS):
            src = y_ref[h] if dil == 1 else y_ref[h, pl.ds(r, rows, stride=dil), :]
            o_ref[0, 0, r, :, h * LANES:(h + 1) * LANES] = src.astype(o_ref.dtype)


def _dil_prep(proj, qk_norm, cos, sin, group, batch, seq, tm=512):
    _, dil = DIL_GROUPS[group]
    ns = seq // tm
    wide = DIL_HEADS * DIL_HEAD_DIM
    ng = len(DIL_GROUPS)
    sub_len = seq // dil
    return pl.pallas_call(
        functools.partial(_dil_prep_kernel, dil=dil, scale_q=DIL_HEAD_DIM ** -0.5),
        grid=(batch * ns, 3),
        in_specs=[pl.BlockSpec((tm, wide), lambda i, j: (i, j * ng + group)),
                  pl.BlockSpec((1, 1, LANES), lambda i, j: (jnp.minimum(j, 1), 0, 0)),
                  pl.BlockSpec((tm, LANES), lambda i, j: (i % ns, 0)),
                  pl.BlockSpec((tm, LANES), lambda i, j: (i % ns, 0))],
        out_specs=pl.BlockSpec((1, 1, dil, tm // dil, wide), lambda i, j: (j, i // ns, 0, i % ns, 0)),
        out_shape=jax.ShapeDtypeStruct((3, batch, dil, sub_len, wide), BF16),
        scratch_shapes=[pltpu.VMEM((DIL_HEADS, tm, LANES), F32)],
        compiler_params=_cparams("parallel", "arbitrary"),
        name=f"dil_prep_g{group}",
    )(proj, qk_norm, cos, sin)


def _dil_attn_kernel(q_ref, kl_ref, kc_ref, kr_ref, vl_ref, vc_ref, vr_ref, o_ref, lse_ref, *, sub_len):
    tl = q_ref.shape[0]
    nkeys = tl + 2 * DIL_RADIUS
    row = lax.broadcasted_iota(jnp.int32, (tl, nkeys), 0)
    col = lax.broadcasted_iota(jnp.int32, (tl, nkeys), 1)
    rel = col - row
    pos = pl.program_id(2) * tl - DIL_RADIUS + col
    valid = (rel >= 0) & (rel <= 2 * DIL_RADIUS) & (pos >= 0) & (pos < sub_len)
    lane = lax.broadcasted_iota(jnp.int32, (tl, LANES), 1)
    lse_all = jnp.zeros((tl, LANES), F32)
    for h in range(DIL_HEADS):
        hs = slice(h * LANES, (h + 1) * LANES)
        kk = jnp.concatenate([kl_ref[:, hs], kc_ref[:, hs], kr_ref[:, hs]], axis=0)
        vv = jnp.concatenate([vl_ref[:, hs], vc_ref[:, hs], vr_ref[:, hs]], axis=0)
        s = lax.dot_general(q_ref[:, hs], kk, NT_DIMS, preferred_element_type=F32)
        s = jnp.where(valid, s, -jnp.inf)
        m = jnp.max(s, axis=-1, keepdims=True)
        p = jnp.exp(s - m)
        l = jnp.sum(p, axis=-1, keepdims=True)
        o_ref[:, hs] = jnp.dot(p.astype(BF16), vv, preferred_element_type=F32) / l
        lse_all = jnp.where(lane == h, m + jnp.log(l), lse_all)
    lse_ref[...] = lse_all


def _dil_attn(qkv, group, batch, seq, tl=128):
    _, dil = DIL_GROUPS[group]
    sub_len = seq // dil
    nl = sub_len // tl
    n_sub = batch * dil
    wide = DIL_HEADS * DIL_HEAD_DIM
    halo = DIL_RADIUS
    per = tl // halo
    flat = qkv.reshape(3 * n_sub * sub_len, wide)
    n_halo_blocks = flat.shape[0] // halo

    def tile(comp):
        return lambda b, r, i: (((comp * batch + b) * dil + r) * nl + i, 0)

    def left(comp):
        return lambda b, r, i: (jnp.maximum(tile(comp)(b, r, i)[0] * per - 1, 0), 0)

    def right(comp):
        return lambda b, r, i: (jnp.minimum((tile(comp)(b, r, i)[0] + 1) * per, n_halo_blocks - 1), 0)

    return pl.pallas_call(
        functools.partial(_dil_attn_kernel, sub_len=sub_len),
        grid=(batch, dil, nl),
        in_specs=[pl.BlockSpec((tl, wide), tile(0)),
                  pl.BlockSpec((halo, wide), left(1)),
                  pl.BlockSpec((tl, wide), tile(1)),
                  pl.BlockSpec((halo, wide), right(1)),
                  pl.BlockSpec((halo, wide), left(2)),
                  pl.BlockSpec((tl, wide), tile(2)),
                  pl.BlockSpec((halo, wide), right(2))],
        out_specs=[pl.BlockSpec((tl, wide), tile(0)),
                   pl.BlockSpec((tl, LANES), tile(0))],
        out_shape=[jax.ShapeDtypeStruct((n_sub * sub_len, wide), F32),
                   jax.ShapeDtypeStruct((n_sub * sub_len, LANES), F32)],
        compiler_params=_cparams("parallel", "parallel", "arbitrary"),
        name=f"dil_attn_g{group}",
    )(flat, flat, flat, flat, flat, flat, flat)


def _combine_kernel(x_ref, a_ref, b_ref, o0_ref, o1_ref, o2_ref, l0_ref, l1_ref, l2_ref, gate_ref,
                    wa_ref, wb_ref, wc_ref, wo_ref, out_ref, c_ref, on_refs, ln_refs):
    tm = x_ref.shape[0]
    o_tok, l_tok = [], []
    for g, (o_ref, l_ref) in enumerate(((o0_ref, l0_ref), (o1_ref, l1_ref), (o2_ref, l2_ref))):
        dil = DIL_GROUPS[g][1]
        if dil == 1:
            o_tok.append(lambda h, o_ref=o_ref: o_ref[0, 0, :, h * LANES:(h + 1) * LANES])
            l_tok.append(l_ref[0, 0])
            continue
        for r in range(dil):
            rows = pl.ds(r, tm // dil, stride=dil)
            ln_refs[g - 1, rows, :] = l_ref[0, r]
            for h in range(DIL_HEADS):
                on_refs[g - 1, h, rows, :] = o_ref[0, r, :, h * LANES:(h + 1) * LANES]
        o_tok.append(lambda h, g=g: on_refs[g - 1, h])
        l_tok.append(ln_refs[g - 1])
    l0, l1, l2 = l_tok
    mx = jnp.maximum(jnp.maximum(l0, l1), l2)
    e0, e1, e2 = jnp.exp(l0 - mx), jnp.exp(l1 - mx), jnp.exp(l2 - mx)
    inv = 1.0 / (e0 + e1 + e2)
    w0, w1, w2 = e0 * inv, e1 * inv, e2 * inv
    for h in range(DIL_HEADS):
        hs = slice(h * LANES, (h + 1) * LANES)
        c = w0[:, h:h + 1] * o_tok[0](h) + w1[:, h:h + 1] * o_tok[1](h) + w2[:, h:h + 1] * o_tok[2](h)
        c_ref[:, hs] = c.astype(BF16)
    ya = jnp.dot(a_ref[...], wa_ref[...], preferred_element_type=F32)
    yb = jnp.dot(b_ref[...], wb_ref[...], preferred_element_type=F32)
    yc = jnp.dot(c_ref[...], wc_ref[...], preferred_element_type=F32)
    d = D_MODEL
    mix = (_sigmoid(gate_ref[:, 0:d].astype(F32)) * ya
           + _sigmoid(gate_ref[:, d:2 * d].astype(F32)) * yb
           + _sigmoid(gate_ref[:, 2 * d:3 * d].astype(F32)) * yc)
    out_ref[...] = x_ref[...] + jnp.dot(mix.astype(BF16), wo_ref[...], preferred_element_type=F32)


def _combine(x, gla_o, mla_o, dil_o, dil_lse, proj, p, batch, seq, tm=256):
    m = x.shape[0]
    d = D_MODEL
    ns = seq // tm
    rowblk = lambda i: (i, 0)
    const = lambda i: (0, 0)
    wspec = pl.BlockSpec((d, d), const)

    def residue_major(arr, group, width):
        dil = DIL_GROUPS[group][1]
        view = arr.reshape(batch, dil, seq // dil, width)
        return view, pl.BlockSpec((1, dil, tm // dil, width), lambda i: (i // ns, 0, i % ns, 0))

    o_views, o_specs = zip(*[residue_major(dil_o[g], g, d) for g in range(len(DIL_GROUPS))])
    l_views, l_specs = zip(*[residue_major(dil_lse[g], g, LANES) for g in range(len(DIL_GROUPS))])
    n_perm = len(DIL_GROUPS) - 1
    return pl.pallas_call(
        _combine_kernel,
        grid=(m // tm,),
        in_specs=[pl.BlockSpec((tm, d), rowblk), pl.BlockSpec((tm, d), rowblk), pl.BlockSpec((tm, d), rowblk),
                  *o_specs, *l_specs,
                  pl.BlockSpec((tm, 3 * d), lambda i: (i, COL_GATE // (3 * d))),
                  wspec, wspec, wspec, wspec],
        out_specs=pl.BlockSpec((tm, d), rowblk),
        out_shape=jax.ShapeDtypeStruct((m, d), F32),
        scratch_shapes=[pltpu.VMEM((tm, d), BF16), pltpu.VMEM((n_perm, DIL_HEADS, tm, LANES), F32),
                        pltpu.VMEM((n_perm, tm, LANES), F32)],
        compiler_params=_cparams("parallel"),
        name="combine",
    )(x, gla_o, mla_o, *o_views, *l_views, proj,
      p["w_branch_a"], p["w_branch_b"], p["w_branch_c"], p["w_out"])


def _xattn_kernel(x_ref, g_ref, wq_ref, qn_ref, kv_ref, kn_ref, wo_ref, out_ref, o_ref):
    x = x_ref[...]
    h = _rms(x, g_ref[...]).astype(BF16)
    q = jnp.dot(h, wq_ref[...], preferred_element_type=F32)
    scale = X_HEAD_DIM ** -0.5
    for hd in range(X_HEADS):
        hs = slice(hd * X_HEAD_DIM, (hd + 1) * X_HEAD_DIM)
        vs = slice(D_MODEL + hd * X_HEAD_DIM, D_MODEL + (hd + 1) * X_HEAD_DIM)
        qh = (_rms(q[:, hs], qn_ref[...]) * scale).astype(BF16)
        kh = _rms(kv_ref[:, hs].astype(F32), kn_ref[...]).astype(BF16)
        s = lax.dot_general(qh, kh, NT_DIMS, preferred_element_type=F32)
        p = jnp.exp(s - jnp.max(s, axis=-1, keepdims=True))
        l = jnp.sum(p, axis=-1, keepdims=True)
        o = jnp.dot(p.astype(BF16), kv_ref[:, vs], preferred_element_type=F32) / l
        o_ref[:, hs] = o.astype(BF16)
    out_ref[...] = x + jnp.dot(o_ref[...], wo_ref[...], preferred_element_type=F32)


def _xattn(x, kv, p, seq, tm=512):
    m = x.shape[0]
    d = D_MODEL
    per_seq = seq // tm
    const = lambda i: (0, 0)
    return pl.pallas_call(
        _xattn_kernel,
        grid=(m // tm,),
        in_specs=[pl.BlockSpec((tm, d), lambda i: (i, 0)),
                  pl.BlockSpec((1, d), const),
                  pl.BlockSpec((d, d), const),
                  pl.BlockSpec((1, X_HEAD_DIM), const),
                  pl.BlockSpec((N_MEM, 2 * d), lambda i: (i // per_seq, 0)),
                  pl.BlockSpec((1, X_HEAD_DIM), const),
                  pl.BlockSpec((d, d), const)],
        out_specs=pl.BlockSpec((tm, d), lambda i: (i, 0)),
        out_shape=jax.ShapeDtypeStruct((m, d), F32),
        scratch_shapes=[pltpu.VMEM((tm, d), BF16)],
        compiler_params=_cparams("parallel"),
        name="xattn",
    )(x, p["norm_xattn"], p["w_xq"], p["xq_norm"], kv, p["xk_norm"], p["w_xo"])


def _ffn_kernel(x_ref, g_ref, w1_ref, w2_ref, out_ref, h_ref, acc_ref):
    j = pl.program_id(1)

    @pl.when(j == 0)
    def _():
        h_ref[...] = _rms(x_ref[...], g_ref[...]).astype(BF16)
        acc_ref[...] = x_ref[...]

    u = jnp.maximum(jnp.dot(h_ref[...], w1_ref[...], preferred_element_type=F32), 0.0)
    acc_ref[...] += jnp.dot((u * u).astype(BF16), w2_ref[...], preferred_element_type=F32)

    @pl.when(j == pl.num_programs(1) - 1)
    def _():
        out_ref[...] = acc_ref[...]


def _ffn(x, p, tm=1024, tf=1024):
    m = x.shape[0]
    d = D_MODEL
    return pl.pallas_call(
        _ffn_kernel,
        grid=(m // tm, D_FF // tf),
        in_specs=[pl.BlockSpec((tm, d), lambda i, j: (i, 0)),
                  pl.BlockSpec((1, d), lambda i, j: (0, 0)),
                  pl.BlockSpec((d, tf), lambda i, j: (0, j)),
                  pl.BlockSpec((tf, d), lambda i, j: (j, 0))],
        out_specs=pl.BlockSpec((tm, d), lambda i, j: (i, 0)),
        out_shape=jax.ShapeDtypeStruct((m, d), F32),
        scratch_shapes=[pltpu.VMEM((tm, d), BF16), pltpu.VMEM((tm, d), F32)],
        compiler_params=_cparams("parallel", "arbitrary"),
        name="ffn",
    )(x, p["norm_ffn"], p["w_ff1"], p["w_ff2"])


def _rope_tables(seq):
    inv_freq = 1.0 / (ROPE_THETA ** (jnp.arange(0, 2 * ROPE_HALF, 2, dtype=F32) / (2 * ROPE_HALF)))
    ang = jnp.arange(seq, dtype=F32)[:, None] * inv_freq[None, :]
    cos, sin = jnp.cos(ang), jnp.sin(ang)
    lo = slice(ROPE_LO, ROPE_LO + ROPE_HALF)
    hi = slice(ROPE_HI, ROPE_HI + ROPE_HALF)
    cos_t = jnp.ones((seq, LANES), F32).at[:, lo].set(cos).at[:, hi].set(cos)
    sin_t = jnp.zeros((seq, LANES), F32).at[:, lo].set(-sin).at[:, hi].set(sin)
    return cos_t, sin_t


def _lanes_after(*taken):
    used = set(int(i) for t in taken for i in t)
    return [i for i in range(LANES) if i not in used]


_ROPE_LANES = list(range(ROPE_LO, ROPE_LO + ROPE_HALF)) + list(range(ROPE_HI, ROPE_HI + ROPE_HALF))
_MLA_LANE_OF_DIM = np.array(_lanes_after(_ROPE_LANES)[:MLA_NOPE] + _ROPE_LANES, np.int32)
_DIL_LANE_OF_DIM = np.array(_ROPE_LANES + _lanes_after(_ROPE_LANES), np.int32)


def _to_lanes(arr, lane_of_dim):
    return jnp.zeros(arr.shape[:-1] + (LANES,), arr.dtype).at[..., lane_of_dim].set(arr)


def _prep_layer(w):
    d = D_MODEL
    hk = GLA_HEADS * GLA_DK
    hv = GLA_HEADS * GLA_DV
    dil_w = len(DIL_GROUPS) * DIL_HEADS * DIL_HEAD_DIM
    sizes = (hk, hk, hv, hv, GLA_GATE_RANK, GLA_GATE_RANK, MLA_Q_RANK, MLA_KV_RANK, MLA_ROPE,
             dil_w, dil_w, dil_w, 3 * d)
    offs = [0]
    for s in sizes:
        offs.append(offs[-1] + s)
    (a_q, a_k, a_v, a_r, a_gf, a_gb, b_q, b_kv, b_kpe, c_q, c_k, c_v, gates) = [
        w["w_in"][:, offs[i]:offs[i + 1]] for i in range(len(sizes))]
    z = lambda n: jnp.zeros((d, n), F32)
    small = (_to_lanes(b_kpe, _MLA_LANE_OF_DIM[MLA_NOPE:])
             .at[:, SMALL_GF:SMALL_GF + GLA_GATE_RANK].set(a_gf)
             .at[:, SMALL_GB:SMALL_GB + GLA_GATE_RANK].set(a_gb))

    def dil_heads(cols):
        per_head = cols.reshape(d, len(DIL_GROUPS) * DIL_HEADS, DIL_HEAD_DIM)
        return _to_lanes(per_head, _DIL_LANE_OF_DIM).reshape(d, dil_w)

    w_in = jnp.concatenate([dil_heads(c_q), dil_heads(c_k), c_v, gates, a_q, a_k, a_v, a_r, b_q, b_kv, small,
                            z(IN_COLS_PAD - COL_SMALL - LANES)], axis=1).astype(BF16)

    def gate_w(wg, lane0):
        return jnp.zeros((LANES, hk), F32).at[lane0:lane0 + GLA_GATE_RANK].set(wg).astype(BF16)

    wgf = gate_w(w["gla_w_gate_f"], SMALL_GF)
    wgb = gate_w(w["gla_w_gate_b"], SMALL_GB)

    w_uq = _to_lanes(w["mla_w_uq"].reshape(MLA_Q_RANK, MLA_HEADS, MLA_QK), _MLA_LANE_OF_DIM)
    w_uq = w_uq.reshape(MLA_Q_RANK, MLA_HEADS * LANES)
    w_ukv = w["mla_w_ukv"].reshape(MLA_KV_RANK, MLA_HEADS, MLA_NOPE + MLA_V)
    w_uk = _to_lanes(w_ukv[:, :, :MLA_NOPE], _MLA_LANE_OF_DIM[:MLA_NOPE])
    w_uk = w_uk.reshape(MLA_KV_RANK, MLA_HEADS * LANES)
    w_uv = w_ukv[:, :, MLA_NOPE:].reshape(MLA_KV_RANK, MLA_HEADS * MLA_V)
    row = lambda v: v.reshape(1, -1).astype(F32)
    return {
        "norm_mix": row(w["norm_mix"]), "w_in": w_in,
        "gla_wgf": wgf, "gla_bgf": row(w["gla_b_gate_f"]),
        "gla_wgb": wgb, "gla_bgb": row(w["gla_b_gate_b"]),
        "gla_out_norm": row(w["gla_out_norm"]), "w_branch_a": w["w_branch_a"].astype(BF16),
        "mla_q_lat_norm": row(w["mla_q_lat_norm"]), "mla_w_uq": w_uq.astype(BF16),
        "mla_kv_lat_norm": row(w["mla_kv_lat_norm"]), "mla_w_uk": w_uk.astype(BF16),
        "mla_w_uv": w_uv.astype(BF16),
        "mla_q_norm": _to_lanes(w["mla_q_norm"], _MLA_LANE_OF_DIM).reshape(1, LANES),
        "mla_k_norm": _to_lanes(w["mla_k_norm"], _MLA_LANE_OF_DIM).reshape(1, LANES),
        "w_branch_b": w["w_branch_b"].astype(BF16),
        "dil_qk_norm": _to_lanes(jnp.stack([w["dil_q_norm"], w["dil_k_norm"]]),
                                 _DIL_LANE_OF_DIM).reshape(2, 1, LANES),
        "w_branch_c": w["w_branch_c"].astype(BF16), "w_out": w["w_out"].astype(BF16),
        "norm_xattn": row(w["norm_xattn"]), "norm_mem": row(w["norm_mem"]),
        "w_xq": w["w_xq"].astype(BF16), "w_xkv": w["w_xkv"].astype(BF16),
        "xq_norm": row(w["xq_norm"]), "xk_norm": row(w["xk_norm"]), "w_xo": w["w_xo"].astype(BF16),
        "norm_ffn": row(w["norm_ffn"]), "w_ff1": w["w_ff1"].astype(BF16), "w_ff2": w["w_ff2"].astype(BF16),
    }


def _layer(x, mem, p, tables, batch, seq):
    cos_m, sin_m = cos_d, sin_d = tables
    proj = _norm_matmul(x, p["norm_mix"], p["w_in"], BF16, tm=1024, tn=1024)
    o_fwd = _gla_direction(proj, p["gla_wgf"], p["gla_bgf"], batch, seq, False)
    gla_o = _gla_direction(proj, p["gla_wgb"], p["gla_bgb"], batch, seq, True,
                           o_fwd=o_fwd, g_out=p["gla_out_norm"])
    q_m, k_m, v_m = _mla_prep(proj, p, cos_m, sin_m, seq)
    mla_o = _mla_attn(q_m, k_m, v_m, batch, seq)
    dil = [_dil_attn(_dil_prep(proj, p["dil_qk_norm"], cos_d, sin_d, g, batch, seq), g, batch, seq)
           for g in range(len(DIL_GROUPS))]
    x = _combine(x, gla_o, mla_o, [o for o, _ in dil], [l for _, l in dil], proj, p, batch, seq)
    kv = _norm_matmul(mem, p["norm_mem"], p["w_xkv"], BF16, tm=N_MEM, tn=1024)
    x = _xattn(x, kv, p, seq)
    return _ffn(x, p)


def _trunk(x, mem, layers):
    batch, seq, d = x.shape
    tables = _rope_tables(seq)
    xf = x.reshape(batch * seq, d)
    memf = mem.reshape(batch * mem.shape[1], d)
    for p in layers:
        xf = _layer(xf, memf, p, tables, batch, seq)
    return xf.reshape(batch, seq, d)


def kernel(x_prompt, x_sample, mem_prompt, mem_sample, norm_mix, w_in, gla_w_gate_f, gla_b_gate_f, gla_w_gate_b, gla_b_gate_b, gla_out_norm, w_branch_a, mla_q_lat_norm, mla_w_uq, mla_kv_lat_norm, mla_w_ukv, mla_q_norm, mla_k_norm, w_branch_b, dil_q_norm, dil_k_norm, w_branch_c, w_out, norm_xattn, norm_mem, w_xq, w_xkv, xq_norm, xk_norm, w_xo, norm_ffn, w_ff1, w_ff2):
    stacked = dict(norm_mix=norm_mix, w_in=w_in, gla_w_gate_f=gla_w_gate_f, gla_b_gate_f=gla_b_gate_f,
                   gla_w_gate_b=gla_w_gate_b, gla_b_gate_b=gla_b_gate_b, gla_out_norm=gla_out_norm,
                   w_branch_a=w_branch_a, mla_q_lat_norm=mla_q_lat_norm, mla_w_uq=mla_w_uq,
                   mla_kv_lat_norm=mla_kv_lat_norm, mla_w_ukv=mla_w_ukv, mla_q_norm=mla_q_norm,
                   mla_k_norm=mla_k_norm, w_branch_b=w_branch_b, dil_q_norm=dil_q_norm, dil_k_norm=dil_k_norm,
                   w_branch_c=w_branch_c, w_out=w_out, norm_xattn=norm_xattn, norm_mem=norm_mem, w_xq=w_xq,
                   w_xkv=w_xkv, xq_norm=xq_norm, xk_norm=xk_norm, w_xo=w_xo, norm_ffn=norm_ffn,
                   w_ff1=w_ff1, w_ff2=w_ff2)
    layers = [_prep_layer({k: v[l] for k, v in stacked.items()}) for l in range(DEPTH)]
    return (_trunk(x_prompt, mem_prompt, layers), _trunk(x_sample, mem_sample, layers))
```

```python
import functools

import jax
import jax.numpy as jnp
import numpy as np
from jax import lax
from jax.experimental import pallas as pl
from jax.experimental.pallas import tpu as pltpu

F32 = jnp.float32
BF16 = jnp.bfloat16

D_MODEL = 1024
DEPTH = 2
N_MEM = 256
ROPE_THETA = 500000.0
NORM_EPS = 1e-6
GLA_HEADS = 4
GLA_DK = 128
GLA_DV = 256
GLA_GATE_RANK = 16
GLA_TAU = 16.0
GLA_CHUNK = 64
MLA_HEADS = 8
MLA_Q_RANK = 256
MLA_KV_RANK = 128
MLA_NOPE = 64
MLA_ROPE = 32
MLA_QK = 96
MLA_V = 128
DIL_GROUPS = ((128, 1), (512, 4), (2048, 16))
DIL_HEADS = 8
DIL_HEAD_DIM = 128
DIL_ROT = 32
DIL_RADIUS = 64
X_HEADS = 4
X_HEAD_DIM = 256
D_FF = 4096

LANES = 128
ROPE_HALF = 16
VMEM_LIMIT_BYTES = 56 * 1024 * 1024

COL_CQ, COL_CK, COL_CV, COL_GATE = 0, 3072, 6144, 9216
COL_AQ, COL_AK, COL_AV, COL_AR = 12288, 12800, 13312, 14336
COL_BQ, COL_BKV, COL_SMALL = 15360, 15616, 15744
IN_COLS_PAD = 16384
SMALL_GF, SMALL_GB = 16, 32
ROPE_LO, ROPE_HI = 0, 64

NT_DIMS = (((1,), (1,)), ((), ()))
LOG2_E = 1.4426950408889634


def _cparams(*sem):
    return pltpu.CompilerParams(dimension_semantics=sem, vmem_limit_bytes=VMEM_LIMIT_BYTES)


def _rms(x, g, n=None):
    ss = jnp.sum(x * x, axis=-1, keepdims=True) * (1.0 / (n or x.shape[-1]))
    return x * lax.rsqrt(ss + NORM_EPS) * g


def _log_sigmoid(z):
    return jnp.minimum(z, 0.0) - jnp.log(1.0 + jnp.exp(-jnp.abs(z)))


def _sigmoid(z):
    return 1.0 / (1.0 + jnp.exp(-z))


def _norm_matmul_kernel(x_ref, g_ref, w_ref, o_ref, h_ref):
    @pl.when(pl.program_id(1) == 0)
    def _():
        h_ref[...] = _rms(x_ref[...].astype(F32), g_ref[...]).astype(BF16)

    o_ref[...] = jnp.dot(h_ref[...], w_ref[...], preferred_element_type=F32).astype(o_ref.dtype)


def _norm_matmul(x, g, w, out_dtype, tm, tn):
    m, k = x.shape
    n = w.shape[1]
    return pl.pallas_call(
        _norm_matmul_kernel,
        grid=(m // tm, n // tn),
        in_specs=[pl.BlockSpec((tm, k), lambda i, j: (i, 0)),
                  pl.BlockSpec((1, k), lambda i, j: (0, 0)),
                  pl.BlockSpec((k, tn), lambda i, j: (0, j))],
        out_specs=pl.BlockSpec((tm, tn), lambda i, j: (i, j)),
        out_shape=jax.ShapeDtypeStruct((m, n), out_dtype),
        scratch_shapes=[pltpu.VMEM((tm, k), BF16)],
        compiler_params=_cparams("parallel", "arbitrary"),
        name="norm_matmul",
    )(x, g, w)


TN_DIMS = (((0,), (0,)), ((), ()))


def _gla_kernel(*refs, tb, reverse, final):
    if final:
        q_ref, k_ref, v_ref, sm_ref, wg_ref, bg_ref, of_ref, r_ref, gn_ref, o_ref, st_ref = refs
    else:
        q_ref, k_ref, v_ref, sm_ref, wg_ref, bg_ref, o_ref, st_ref = refs

    @pl.when(pl.program_id(1) == 0)
    def _():
        st_ref[...] = jnp.zeros_like(st_ref)

    ck = GLA_CHUNK
    la = _log_sigmoid(jnp.dot(sm_ref[...], wg_ref[...], preferred_element_type=F32) + bg_ref[...]) * (1.0 / GLA_TAU)

    r = lax.broadcasted_iota(jnp.int32, (ck, ck), 0)
    c = lax.broadcasted_iota(jnp.int32, (ck, ck), 1)
    tri_mask = (r <= c) if reverse else (r >= c)
    tri = jnp.where(tri_mask, 1.0, 0.0).astype(BF16)
    end = 0 if reverse else ck - 1
    scale = GLA_DK ** -0.5

    n_chunks = tb // ck
    order = range(n_chunks - 1, -1, -1) if reverse else range(n_chunks)
    for ci in order:
        sl = slice(ci * ck, (ci + 1) * ck)
        la_c = la[sl]
        hi = la_c.astype(BF16)
        lo = (la_c - hi.astype(F32)).astype(BF16)
        b = jnp.dot(tri, hi, preferred_element_type=F32) + jnp.dot(tri, lo, preferred_element_type=F32)
        b_end = b[end:end + 1, :]
        kc = k_ref[sl, :].astype(F32)
        qd = (q_ref[sl, :].astype(F32) * scale * jnp.exp(b)).astype(BF16)
        kinv = (kc * jnp.exp(-b)).astype(BF16)
        ktail = (kc * jnp.exp(b_end - b)).astype(BF16)
        dec = jnp.exp(b_end)
        for h in range(GLA_HEADS):
            ks = slice(h * GLA_DK, (h + 1) * GLA_DK)
            vs = slice(h * GLA_DV, (h + 1) * GLA_DV)
            qh = qd[:, ks]
            att = lax.dot_general(qh, kinv[:, ks], NT_DIMS, preferred_element_type=F32)
            att = jnp.where(tri_mask, att, 0.0).astype(BF16)
            vh = v_ref[sl, vs]
            st = st_ref[h]
            o = (jnp.dot(att, vh, preferred_element_type=F32)
                 + lax.dot_general(qh, st.astype(BF16), NT_DIMS, preferred_element_type=F32))
            st_ref[h] = dec[:, ks] * st + lax.dot_general(vh, ktail[:, ks], TN_DIMS, preferred_element_type=F32)
            if final:
                o = _rms(o + of_ref[sl, vs], gn_ref[...])
                rr = r_ref[sl, vs].astype(F32)
                o = o * (rr * _sigmoid(rr))
            o_ref[sl, vs] = o.astype(o_ref.dtype)


def _gla_direction(proj, wg, bg, batch, seq, reverse, o_fwd=None, g_out=None, tb=256):
    nb = seq // tb
    m = batch * seq
    final = o_fwd is not None

    def row(b, i):
        return b * nb + (nb - 1 - i if reverse else i)

    def col(block_cols, off):
        return lambda b, i: (row(b, i), off // block_cols)

    hk = GLA_HEADS * GLA_DK
    hv = GLA_HEADS * GLA_DV
    const = lambda b, i: (0, 0)
    in_specs = [pl.BlockSpec((tb, hk), col(hk, COL_AQ)),
                pl.BlockSpec((tb, hk), col(hk, COL_AK)),
                pl.BlockSpec((tb, hv), col(hv, COL_AV)),
                pl.BlockSpec((tb, LANES), col(LANES, COL_SMALL)),
                pl.BlockSpec((LANES, hk), const),
                pl.BlockSpec((1, hk), const)]
    args = [proj, proj, proj, proj, wg, bg]
    if final:
        in_specs += [pl.BlockSpec((tb, hv), col(hv, 0)),
                     pl.BlockSpec((tb, hv), col(hv, COL_AR)),
                     pl.BlockSpec((1, GLA_DV), const)]
        args += [o_fwd, proj, g_out]
    return pl.pallas_call(
        functools.partial(_gla_kernel, tb=tb, reverse=reverse, final=final),
        grid=(batch, nb),
        in_specs=in_specs,
        out_specs=pl.BlockSpec((tb, hv), col(hv, 0)),
        out_shape=jax.ShapeDtypeStruct((m, hv), BF16 if final else F32),
        scratch_shapes=[pltpu.VMEM((GLA_HEADS, GLA_DV, GLA_DK), F32)],
        compiler_params=_cparams("parallel", "arbitrary"),
        name="gla_bwd" if reverse else "gla_fwd",
    )(*args)


def _rope(y, cos, sin_signed):
    return y * cos + pltpu.roll(y, LANES // 2, 1) * sin_signed


def _mla_prep_kernel(bq_ref, bkv_ref, sm_ref, qln_ref, wuq_ref, kvln_ref, wuk_ref, wuv_ref,
                     qn_ref, kn_ref, cos_ref, sin_ref, q_out, k_out, v_out):
    hq = _rms(bq_ref[...].astype(F32), qln_ref[...]).astype(BF16)
    hkv = _rms(bkv_ref[...].astype(F32), kvln_ref[...]).astype(BF16)
    q = jnp.dot(hq, wuq_ref[...], preferred_element_type=F32)
    kn = jnp.dot(hkv, wuk_ref[...], preferred_element_type=F32)
    v = jnp.dot(hkv, wuv_ref[...], preferred_element_type=F32)
    tm = q.shape[0]
    lane = lax.broadcasted_iota(jnp.int32, (tm, LANES), 1)
    sm = sm_ref[...].astype(F32)
    in_rope = (lane < ROPE_LO + ROPE_HALF) | ((lane >= ROPE_HI) & (lane < ROPE_HI + ROPE_HALF))
    kpe = jnp.where(in_rope, sm, 0.0)
    cos = cos_ref[...]
    sin = sin_ref[...]
    scale = MLA_QK ** -0.5 * LOG2_E
    ones = jnp.ones((tm, LANES), v_out.dtype)
    for h in range(MLA_HEADS):
        hs = slice(h * LANES, (h + 1) * LANES)
        qh = _rope(_rms(q[:, hs], qn_ref[...], MLA_QK), cos, sin)
        q_out[:, hs] = (qh * scale).astype(q_out.dtype)
        kh = _rope(_rms(kn[:, hs] + kpe, kn_ref[...], MLA_QK), cos, sin)
        k_out[:, hs] = kh.astype(k_out.dtype)
        v_out[:, 2 * h * LANES:(2 * h + 1) * LANES] = v[:, hs].astype(v_out.dtype)
        v_out[:, (2 * h + 1) * LANES:(2 * h + 2) * LANES] = ones


def _mla_prep(proj, p, cos, sin, seq, tm=512):
    m = proj.shape[0]
    ns = seq // tm
    const = lambda i: (0, 0)
    wide = MLA_HEADS * LANES
    out = jax.ShapeDtypeStruct((m, wide), BF16)
    return pl.pallas_call(
        _mla_prep_kernel,
        grid=(m // tm,),
        in_specs=[pl.BlockSpec((tm, MLA_Q_RANK), lambda i: (i, COL_BQ // MLA_Q_RANK)),
                  pl.BlockSpec((tm, MLA_KV_RANK), lambda i: (i, COL_BKV // MLA_KV_RANK)),
                  pl.BlockSpec((tm, LANES), lambda i: (i, COL_SMALL // LANES)),
                  pl.BlockSpec((1, MLA_Q_RANK), const),
                  pl.BlockSpec((MLA_Q_RANK, wide), const),
                  pl.BlockSpec((1, MLA_KV_RANK), const),
                  pl.BlockSpec((MLA_KV_RANK, wide), const),
                  pl.BlockSpec((MLA_KV_RANK, wide), const),
                  pl.BlockSpec((1, LANES), const),
                  pl.BlockSpec((1, LANES), const),
                  pl.BlockSpec((tm, LANES), lambda i: (i % ns, 0)),
                  pl.BlockSpec((tm, LANES), lambda i: (i % ns, 0))],
        out_specs=[pl.BlockSpec((tm, wide), lambda i: (i, 0)), pl.BlockSpec((tm, wide), lambda i: (i, 0)),
                   pl.BlockSpec((tm, 2 * wide), lambda i: (i, 0))],
        out_shape=[out, out, jax.ShapeDtypeStruct((m, 2 * wide), BF16)],
        compiler_params=_cparams("parallel"),
        name="mla_prep",
    )(proj, proj, proj, p["mla_q_lat_norm"], p["mla_w_uq"], p["mla_kv_lat_norm"], p["mla_w_uk"],
      p["mla_w_uv"], p["mla_q_norm"], p["mla_k_norm"], cos, sin)


def _mla_attn_kernel(q_ref, k_ref, v_ref, o_ref, *, tk, unroll):
    q = q_ref[...]
    tq = q.shape[0]
    nk = k_ref.shape[0] // tk

    def body(t, carry):
        m, acc = carry
        start = pl.multiple_of(t * tk, tk)
        s = lax.dot_general(q, k_ref[pl.ds(start, tk), :], NT_DIMS, preferred_element_type=F32)
        m_new = jnp.maximum(m, jnp.max(s, axis=-1, keepdims=True))
        p = jnp.exp2(s - m_new).astype(BF16)
        acc = jnp.exp2(m - m_new) * acc + jnp.dot(p, v_ref[pl.ds(start, tk), :], preferred_element_type=F32)
        return m_new, acc

    init = (jnp.full((tq, 1), -jnp.inf, F32), jnp.zeros((tq, 2 * LANES), F32))
    _, acc = lax.fori_loop(0, nk, body, init, unroll=unroll)
    o_ref[...] = (acc[:, :LANES] / acc[:, LANES:]).astype(o_ref.dtype)


def _mla_attn(q, k, v, batch, seq, tk=512, max_unroll=16):
    tq = 512 if seq // tk <= max_unroll // 4 else 256
    nq = seq // tq
    return pl.pallas_call(
        functools.partial(_mla_attn_kernel, tk=tk, unroll=min(max_unroll, seq // tk)),
        grid=(batch, MLA_HEADS, nq),
        in_specs=[pl.BlockSpec((tq, LANES), lambda b, h, i: (b * nq + i, h)),
                  pl.BlockSpec((seq, LANES), lambda b, h, i: (b, h)),
                  pl.BlockSpec((seq, 2 * LANES), lambda b, h, i: (b, h))],
        out_specs=pl.BlockSpec((tq, LANES), lambda b, h, i: (b * nq + i, h)),
        out_shape=jax.ShapeDtypeStruct(q.shape, BF16),
        compiler_params=_cparams("parallel", "parallel", "arbitrary"),
        name="mla_attn",
    )(q, k, v)


def _dil_prep_kernel(x_ref, nw_ref, cos_ref, sin_ref, o_ref, y_ref, *, dil, scale_q):
    tm = x_ref.shape[0]
    comp = pl.program_id(1)

    @pl.when(comp < 2)
    def _():
        cos = cos_ref[...]
        sin = sin_ref[...]
        nw = nw_ref[0]
        scale = jnp.where(comp == 0, scale_q, 1.0).astype(F32)
        for h in range(DIL_HEADS):
            hs = slice(h * LANES, (h + 1) * LANES)
            y_ref[h] = _rope(_rms(x_ref[:, hs].astype(F32), nw), cos, sin) * scale

    @pl.when(comp == 2)
    def _():
        for h in range(DIL_HEADS):
            y_ref[h] = x_ref[:, h * LANES:(h + 1) * LANES].astype(F32)

    rows = tm // dil
    for r in range(dil):
        for h in range(DIL_HEADS):
            src = y_ref[h] if dil == 1 else y_ref[h, pl.ds(r, rows, stride=dil), :]
            o_ref[0, 0, r, :, h * LANES:(h + 1) * LANES] = src.astype(o_ref.dtype)


def _dil_prep(proj, qk_norm, cos, sin, group, batch, seq, tm=512):
    _, dil = DIL_GROUPS[group]
    ns = seq // tm
    wide = DIL_HEADS * DIL_HEAD_DIM
    ng = len(DIL_GROUPS)
    sub_len = seq // dil
    return pl.pallas_call(
        functools.partial(_dil_prep_kernel, dil=dil, scale_q=DIL_HEAD_DIM ** -0.5),
        grid=(batch * ns, 3),
        in_specs=[pl.BlockSpec((tm, wide), lambda i, j: (i, j * ng + group)),
                  pl.BlockSpec((1, 1, LANES), lambda i, j: (jnp.minimum(j, 1), 0, 0)),
                  pl.BlockSpec((tm, LANES), lambda i, j: (i % ns, 0)),
                  pl.BlockSpec((tm, LANES), lambda i, j: (i % ns, 0))],
        out_specs=pl.BlockSpec((1, 1, dil, tm // dil, wide), lambda i, j: (j, i // ns, 0, i % ns, 0)),
        out_shape=jax.ShapeDtypeStruct((3, batch, dil, sub_len, wide), BF16),
        scratch_shapes=[pltpu.VMEM((DIL_HEADS, tm, LANES), F32)],
        compiler_params=_cparams("parallel", "arbitrary"),
        name=f"dil_prep_g{group}",
    )(proj, qk_norm, cos, sin)


def _dil_attn_kernel(q_ref, kl_ref, kc_ref, kr_ref, vl_ref, vc_ref, vr_ref, o_ref, lse_ref, *, sub_len):
    tl = q_ref.shape[0]
    nkeys = tl + 2 * DIL_RADIUS
    row = lax.broadcasted_iota(jnp.int32, (tl, nkeys), 0)
    col = lax.broadcasted_iota(jnp.int32, (tl, nkeys), 1)
    rel = col - row
    pos = pl.program_id(2) * tl - DIL_RADIUS + col
    valid = (rel >= 0) & (rel <= 2 * DIL_RADIUS) & (pos >= 0) & (pos < sub_len)
    lane = lax.broadcasted_iota(jnp.int32, (tl, LANES), 1)
    lse_all = jnp.zeros((tl, LANES), F32)
    for h in range(DIL_HEADS):
        hs = slice(h * LANES, (h + 1) * LANES)
        kk = jnp.concatenate([kl_ref[:, hs], kc_ref[:, hs], kr_ref[:, hs]], axis=0)
        vv = jnp.concatenate([vl_ref[:, hs], vc_ref[:, hs], vr_ref[:, hs]], axis=0)
        s = lax.dot_general(q_ref[:, hs], kk, NT_DIMS, preferred_element_type=F32)
        s = jnp.where(valid, s, -jnp.inf)
        m = jnp.max(s, axis=-1, keepdims=True)
        p = jnp.exp(s - m)
        l = jnp.sum(p, axis=-1, keepdims=True)
        o_ref[:, hs] = jnp.dot(p.astype(BF16), vv, preferred_element_type=F32) / l
        lse_all = jnp.where(lane == h, m + jnp.log(l), lse_all)
    lse_ref[...] = lse_all


def _dil_attn(qkv, group, batch, seq, tl=128):
    _, dil = DIL_GROUPS[group]
    sub_len = seq // dil
    nl = sub_len // tl
    n_sub = batch * dil
    wide = DIL_HEADS * DIL_HEAD_DIM
    halo = DIL_RADIUS
    per = tl // halo
    flat = qkv.reshape(3 * n_sub * sub_len, wide)
    n_halo_blocks = flat.shape[0] // halo

    def tile(comp):
        return lambda b, r, i: (((comp * batch + b) * dil + r) * nl + i, 0)

    def left(comp):
        return lambda b, r, i: (jnp.maximum(tile(comp)(b, r, i)[0] * per - 1, 0), 0)

    def right(comp):
        return lambda b, r, i: (jnp.minimum((tile(comp)(b, r, i)[0] + 1) * per, n_halo_blocks - 1), 0)

    return pl.pallas_call(
        functools.partial(_dil_attn_kernel, sub_len=sub_len),
        grid=(batch, dil, nl),
        in_specs=[pl.BlockSpec((tl, wide), tile(0)),
                  pl.BlockSpec((halo, wide), left(1)),
                  pl.BlockSpec((tl, wide), tile(1)),
                  pl.BlockSpec((halo, wide), right(1)),
                  pl.BlockSpec((halo, wide), left(2)),
                  pl.BlockSpec((tl, wide), tile(2)),
                  pl.BlockSpec((halo, wide), right(2))],
        out_specs=[pl.BlockSpec((tl, wide), tile(0)),
                   pl.BlockSpec((tl, LANES), tile(0))],
        out_shape=[jax.ShapeDtypeStruct((n_sub * sub_len, wide), F32),
                   jax.ShapeDtypeStruct((n_sub * sub_len, LANES), F32)],
        compiler_params=_cparams("parallel", "parallel", "arbitrary"),
        name=f"dil_attn_g{group}",
    )(flat, flat, flat, flat, flat, flat, flat)


def _combine_kernel(x_ref, a_ref, b_ref, o0_ref, o1_ref, o2_ref, l0_ref, l1_ref, l2_ref, gate_ref,
                    wa_ref, wb_ref, wc_ref, wo_ref, out_ref, c_ref, on_refs, ln_refs):
    tm = x_ref.shape[0]
    o_tok, l_tok = [], []
    for g, (o_ref, l_ref) in enumerate(((o0_ref, l0_ref), (o1_ref, l1_ref), (o2_ref, l2_ref))):
        dil = DIL_GROUPS[g][1]
        if dil == 1:
            o_tok.append(lambda h, o_ref=o_ref: o_ref[0, 0, :, h * LANES:(h + 1) * LANES])
            l_tok.append(l_ref[0, 0])
            continue
        for r in range(dil):
            rows = pl.ds(r, tm // dil, stride=dil)
            ln_refs[g - 1, rows, :] = l_ref[0, r]
            for h in range(DIL_HEADS):
                on_refs[g - 1, h, rows, :] = o_ref[0, r, :, h * LANES:(h + 1) * LANES]
        o_tok.append(lambda h, g=g: on_refs[g - 1, h])
        l_tok.append(ln_refs[g - 1])
    l0, l1, l2 = l_tok
    mx = jnp.maximum(jnp.maximum(l0, l1), l2)
    e0, e1, e2 = jnp.exp(l0 - mx), jnp.exp(l1 - mx), jnp.exp(l2 - mx)
    inv = 1.0 / (e0 + e1 + e2)
    w0, w1, w2 = e0 * inv, e1 * inv, e2 * inv
    for h in range(DIL_HEADS):
        hs = slice(h * LANES, (h + 1) * LANES)
        c = w0[:, h:h + 1] * o_tok[0](h) + w1[:, h:h + 1] * o_tok[1](h) + w2[:, h:h + 1] * o_tok[2](h)
        c_ref[:, hs] = c.astype(BF16)
    ya = jnp.dot(a_ref[...], wa_ref[...], preferred_element_type=F32)
    yb = jnp.dot(b_ref[...], wb_ref[...], preferred_element_type=F32)
    yc = jnp.dot(c_ref[...], wc_ref[...], preferred_element_type=F32)
    d = D_MODEL
    mix = (_sigmoid(gate_ref[:, 0:d].astype(F32)) * ya
           + _sigmoid(gate_ref[:, d:2 * d].astype(F32)) * yb
           + _sigmoid(gate_ref[:, 2 * d:3 * d].astype(F32)) * yc)
    out_ref[...] = x_ref[...] + jnp.dot(mix.astype(BF16), wo_ref[...], preferred_element_type=F32)


def _combine(x, gla_o, mla_o, dil_o, dil_lse, proj, p, batch, seq, tm=256):
    m = x.shape[0]
    d = D_MODEL
    ns = seq // tm
    rowblk = lambda i: (i, 0)
    const = lambda i: (0, 0)
    wspec = pl.BlockSpec((d, d), const)

    def residue_major(arr, group, width):
        dil = DIL_GROUPS[group][1]
        view = arr.reshape(batch, dil, seq // dil, width)
        return view, pl.BlockSpec((1, dil, tm // dil, width), lambda i: (i // ns, 0, i % ns, 0))

    o_views, o_specs = zip(*[residue_major(dil_o[g], g, d) for g in range(len(DIL_GROUPS))])
    l_views, l_specs = zip(*[residue_major(dil_lse[g], g, LANES) for g in range(len(DIL_GROUPS))])
    n_perm = len(DIL_GROUPS) - 1
    return pl.pallas_call(
        _combine_kernel,
        grid=(m // tm,),
        in_specs=[pl.BlockSpec((tm, d), rowblk), pl.BlockSpec((tm, d), rowblk), pl.BlockSpec((tm, d), rowblk),
                  *o_specs, *l_specs,
                  pl.BlockSpec((tm, 3 * d), lambda i: (i, COL_GATE // (3 * d))),
                  wspec, wspec, wspec, wspec],
        out_specs=pl.BlockSpec((tm, d), rowblk),
        out_shape=jax.ShapeDtypeStruct((m, d), F32),
        scratch_shapes=[pltpu.VMEM((tm, d), BF16), pltpu.VMEM((n_perm, DIL_HEADS, tm, LANES), F32),
                        pltpu.VMEM((n_perm, tm, LANES), F32)],
        compiler_params=_cparams("parallel"),
        name="combine",
    )(x, gla_o, mla_o, *o_views, *l_views, proj,
      p["w_branch_a"], p["w_branch_b"], p["w_branch_c"], p["w_out"])


def _xattn_kernel(x_ref, g_ref, wq_ref, qn_ref, kv_ref, kn_ref, wo_ref, out_ref, o_ref):
    x = x_ref[...]
    h = _rms(x, g_ref[...]).astype(BF16)
    q = jnp.dot(h, wq_ref[...], preferred_element_type=F32)
    scale = X_HEAD_DIM ** -0.5
    for hd in range(X_HEADS):
        hs = slice(hd * X_HEAD_DIM, (hd + 1) * X_HEAD_DIM)
        vs = slice(D_MODEL + hd * X_HEAD_DIM, D_MODEL + (hd + 1) * X_HEAD_DIM)
        qh = (_rms(q[:, hs], qn_ref[...]) * scale).astype(BF16)
        kh = _rms(kv_ref[:, hs].astype(F32), kn_ref[...]).astype(BF16)
        s = lax.dot_general(qh, kh, NT_DIMS, preferred_element_type=F32)
        p = jnp.exp(s - jnp.max(s, axis=-1, keepdims=True))
        l = jnp.sum(p, axis=-1, keepdims=True)
        o = jnp.dot(p.astype(BF16), kv_ref[:, vs], preferred_element_type=F32) / l
        o_ref[:, hs] = o.astype(BF16)
    out_ref[...] = x + jnp.dot(o_ref[...], wo_ref[...], preferred_element_type=F32)


def _xattn(x, kv, p, seq, tm=512):
    m = x.shape[0]
    d = D_MODEL
    per_seq = seq // tm
    const = lambda i: (0, 0)
    return pl.pallas_call(
        _xattn_kernel,
        grid=(m // tm,),
        in_specs=[pl.BlockSpec((tm, d), lambda i: (i, 0)),
                  pl.BlockSpec((1, d), const),
                  pl.BlockSpec((d, d), const),
                  pl.BlockSpec((1, X_HEAD_DIM), const),
                  pl.BlockSpec((N_MEM, 2 * d), lambda i: (i // per_seq, 0)),
                  pl.BlockSpec((1, X_HEAD_DIM), const),
                  pl.BlockSpec((d, d), const)],
        out_specs=pl.BlockSpec((tm, d), lambda i: (i, 0)),
        out_shape=jax.ShapeDtypeStruct((m, d), F32),
        scratch_shapes=[pltpu.VMEM((tm, d), BF16)],
        compiler_params=_cparams("parallel"),
        name="xattn",
    )(x, p["norm_xattn"], p["w_xq"], p["xq_norm"], kv, p["xk_norm"], p["w_xo"])


def _ffn_kernel(x_ref, g_ref, w1_ref, w2_ref, out_ref, h_ref, acc_ref):
    j = pl.program_id(1)

    @pl.when(j == 0)
    def _():
        h_ref[...] = _rms(x_ref[...], g_ref[...]).astype(BF16)
        acc_ref[...] = x_ref[...]

    u = jnp.maximum(jnp.dot(h_ref[...], w1_ref[...], preferred_element_type=F32), 0.0)
    acc_ref[...] += jnp.dot((u * u).astype(BF16), w2_ref[...], preferred_element_type=F32)

    @pl.when(j == pl.num_programs(1) - 1)
    def _():
        out_ref[...] = acc_ref[...]


def _ffn(x, p, tm=1024, tf=1024):
    m = x.shape[0]
    d = D_MODEL
    return pl.pallas_call(
        _ffn_kernel,
        grid=(m // tm, D_FF // tf),
        in_specs=[pl.BlockSpec((tm, d), lambda i, j: (i, 0)),
                  pl.BlockSpec((1, d), lambda i, j: (0, 0)),
                  pl.BlockSpec((d, tf), lambda i, j: (0, j)),
                  pl.BlockSpec((tf, d), lambda i, j: (j, 0))],
        out_specs=pl.BlockSpec((tm, d), lambda i, j: (i, 0)),
        out_shape=jax.ShapeDtypeStruct((m, d), F32),
        scratch_shapes=[pltpu.VMEM((tm, d), BF16), pltpu.VMEM((tm, d), F32)],
        compiler_params=_cparams("parallel", "arbitrary"),
        name="ffn",
    )(x, p["norm_ffn"], p["w_ff1"], p["w_ff2"])


def _rope_tables(seq):
    inv_freq = 1.0 / (ROPE_THETA ** (jnp.arange(0, 2 * ROPE_HALF, 2, dtype=F32) / (2 * ROPE_HALF)))
    ang = jnp.arange(seq, dtype=F32)[:, None] * inv_freq[None, :]
    cos, sin = jnp.cos(ang), jnp.sin(ang)
    gap = ROPE_HI - ROPE_LO - ROPE_HALF
    tail = LANES - ROPE_HI - ROPE_HALF
    fill = lambda n, v: jnp.full((seq, n), v, F32)
    cos_t = jnp.concatenate([fill(ROPE_LO, 1.0), cos, fill(gap, 1.0), cos, fill(tail, 1.0)], axis=1)
    sin_t = jnp.concatenate([fill(ROPE_LO, 0.0), -sin, fill(gap, 0.0), sin, fill(tail, 0.0)], axis=1)
    return cos_t, sin_t


def _lanes_after(*taken):
    used = set(int(i) for t in taken for i in t)
    return [i for i in range(LANES) if i not in used]


_ROPE_LANES = list(range(ROPE_LO, ROPE_LO + ROPE_HALF)) + list(range(ROPE_HI, ROPE_HI + ROPE_HALF))
_MLA_LANE_OF_DIM = np.array(_lanes_after(_ROPE_LANES)[:MLA_NOPE] + _ROPE_LANES, np.int32)
_DIL_LANE_OF_DIM = np.array(_ROPE_LANES + _lanes_after(_ROPE_LANES), np.int32)


def _to_lanes(arr, lane_of_dim):
    dim_of_lane = np.full(LANES, -1)
    dim_of_lane[lane_of_dim] = np.arange(len(lane_of_dim))
    pieces, lane = [], 0
    while lane < LANES:
        run = 1
        while (lane + run < LANES and
               (dim_of_lane[lane + run] == dim_of_lane[lane] + run if dim_of_lane[lane] >= 0
                else dim_of_lane[lane + run] < 0)):
            run += 1
        start = int(dim_of_lane[lane])
        pieces.append(arr[..., start:start + run] if start >= 0
                      else jnp.zeros(arr.shape[:-1] + (run,), arr.dtype))
        lane += run
    return jnp.concatenate(pieces, axis=-1)


def _prep_layer(w):
    d = D_MODEL
    hk = GLA_HEADS * GLA_DK
    hv = GLA_HEADS * GLA_DV
    dil_w = len(DIL_GROUPS) * DIL_HEADS * DIL_HEAD_DIM
    sizes = (hk, hk, hv, hv, GLA_GATE_RANK, GLA_GATE_RANK, MLA_Q_RANK, MLA_KV_RANK, MLA_ROPE,
             dil_w, dil_w, dil_w, 3 * d)
    offs = [0]
    for s in sizes:
        offs.append(offs[-1] + s)
    (a_q, a_k, a_v, a_r, a_gf, a_gb, b_q, b_kv, b_kpe, c_q, c_k, c_v, gates) = [
        w["w_in"][:, offs[i]:offs[i + 1]] for i in range(len(sizes))]
    z = lambda n: jnp.zeros((d, n), F32)
    small_lane_of_dim = np.concatenate([_MLA_LANE_OF_DIM[MLA_NOPE:], SMALL_GF + np.arange(GLA_GATE_RANK),
                                        SMALL_GB + np.arange(GLA_GATE_RANK)])
    small = _to_lanes(jnp.concatenate([b_kpe, a_gf, a_gb], axis=1), small_lane_of_dim)

    def dil_heads(cols):
        per_head = cols.reshape(d, len(DIL_GROUPS) * DIL_HEADS, DIL_HEAD_DIM)
        return _to_lanes(per_head, _DIL_LANE_OF_DIM).reshape(d, dil_w)

    w_in = jnp.concatenate([dil_heads(c_q), dil_heads(c_k), c_v, gates, a_q, a_k, a_v, a_r, b_q, b_kv, small,
                            z(IN_COLS_PAD - COL_SMALL - LANES)], axis=1).astype(BF16)

    def gate_w(wg, lane0):
        rows = lambda n: jnp.zeros((n, hk), F32)
        return jnp.concatenate([rows(lane0), wg, rows(LANES - lane0 - GLA_GATE_RANK)], axis=0).astype(BF16)

    wgf = gate_w(w["gla_w_gate_f"], SMALL_GF)
    wgb = gate_w(w["gla_w_gate_b"], SMALL_GB)

    w_uq = _to_lanes(w["mla_w_uq"].reshape(MLA_Q_RANK, MLA_HEADS, MLA_QK), _MLA_LANE_OF_DIM)
    w_uq = w_uq.reshape(MLA_Q_RANK, MLA_HEADS * LANES)
    w_ukv = w["mla_w_ukv"].reshape(MLA_KV_RANK, MLA_HEADS, MLA_NOPE + MLA_V)
    w_uk = _to_lanes(w_ukv[:, :, :MLA_NOPE], _MLA_LANE_OF_DIM[:MLA_NOPE])
    w_uk = w_uk.reshape(MLA_KV_RANK, MLA_HEADS * LANES)
    w_uv = w_ukv[:, :, MLA_NOPE:].reshape(MLA_KV_RANK, MLA_HEADS * MLA_V)
    row = lambda v: v.reshape(1, -1).astype(F32)
    return {
        "norm_mix": row(w["norm_mix"]), "w_in": w_in,
        "gla_wgf": wgf, "gla_bgf": row(w["gla_b_gate_f"]),
        "gla_wgb": wgb, "gla_bgb": row(w["gla_b_gate_b"]),
        "gla_out_norm": row(w["gla_out_norm"]), "w_branch_a": w["w_branch_a"].astype(BF16),
        "mla_q_lat_norm": row(w["mla_q_lat_norm"]), "mla_w_uq": w_uq.astype(BF16),
        "mla_kv_lat_norm": row(w["mla_kv_lat_norm"]), "mla_w_uk": w_uk.astype(BF16),
        "mla_w_uv": w_uv.astype(BF16),
        "mla_q_norm": _to_lanes(w["mla_q_norm"], _MLA_LANE_OF_DIM).reshape(1, LANES),
        "mla_k_norm": _to_lanes(w["mla_k_norm"], _MLA_LANE_OF_DIM).reshape(1, LANES),
        "w_branch_b": w["w_branch_b"].astype(BF16),
        "dil_qk_norm": _to_lanes(jnp.stack([w["dil_q_norm"], w["dil_k_norm"]]),
                                 _DIL_LANE_OF_DIM).reshape(2, 1, LANES),
        "w_branch_c": w["w_branch_c"].astype(BF16), "w_out": w["w_out"].astype(BF16),
        "norm_xattn": row(w["norm_xattn"]), "norm_mem": row(w["norm_mem"]),
        "w_xq": w["w_xq"].astype(BF16), "w_xkv": w["w_xkv"].astype(BF16),
        "xq_norm": row(w["xq_norm"]), "xk_norm": row(w["xk_norm"]), "w_xo": w["w_xo"].astype(BF16),
        "norm_ffn": row(w["norm_ffn"]), "w_ff1": w["w_ff1"].astype(BF16), "w_ff2": w["w_ff2"].astype(BF16),
    }


def _layer(x, mem, p, tables, batch, seq):
    cos_m, sin_m = cos_d, sin_d = tables
    proj = _norm_matmul(x, p["norm_mix"], p["w_in"], BF16, tm=1024, tn=1024)
    o_fwd = _gla_direction(proj, p["gla_wgf"], p["gla_bgf"], batch, seq, False)
    gla_o = _gla_direction(proj, p["gla_wgb"], p["gla_bgb"], batch, seq, True,
                           o_fwd=o_fwd, g_out=p["gla_out_norm"])
    q_m, k_m, v_m = _mla_prep(proj, p, cos_m, sin_m, seq)
    mla_o = _mla_attn(q_m, k_m, v_m, batch, seq)
    dil = [_dil_attn(_dil_prep(proj, p["dil_qk_norm"], cos_d, sin_d, g, batch, seq), g, batch, seq)
           for g in range(len(DIL_GROUPS))]
    x = _combine(x, gla_o, mla_o, [o for o, _ in dil], [l for _, l in dil], proj, p, batch, seq)
    kv = _norm_matmul(mem, p["norm_mem"], p["w_xkv"], BF16, tm=N_MEM, tn=1024)
    x = _xattn(x, kv, p, seq)
    return _ffn(x, p)


def _trunk(x, mem, layers):
    batch, seq, d = x.shape
    tables = _rope_tables(seq)
    xf = x.reshape(batch * seq, d)
    memf = mem.reshape(batch * mem.shape[1], d)
    for p in layers:
        xf = _layer(xf, memf, p, tables, batch, seq)
    return xf.reshape(batch, seq, d)


def kernel(x_prompt, x_sample, mem_prompt, mem_sample, norm_mix, w_in, gla_w_gate_f, gla_b_gate_f, gla_w_gate_b, gla_b_gate_b, gla_out_norm, w_branch_a, mla_q_lat_norm, mla_w_uq, mla_kv_lat_norm, mla_w_ukv, mla_q_norm, mla_k_norm, w_branch_b, dil_q_norm, dil_k_norm, w_branch_c, w_out, norm_xattn, norm_mem, w_xq, w_xkv, xq_norm, xk_norm, w_xo, norm_ffn, w_ff1, w_ff2):
    stacked = dict(norm_mix=norm_mix, w_in=w_in, gla_w_gate_f=gla_w_gate_f, gla_b_gate_f=gla_b_gate_f,
                   gla_w_gate_b=gla_w_gate_b, gla_b_gate_b=gla_b_gate_b, gla_out_norm=gla_out_norm,
                   w_branch_a=w_branch_a, mla_q_lat_norm=mla_q_lat_norm, mla_w_uq=mla_w_uq,
                   mla_kv_lat_norm=mla_kv_lat_norm, mla_w_ukv=mla_w_ukv, mla_q_norm=mla_q_norm,
                   mla_k_norm=mla_k_norm, w_branch_b=w_branch_b, dil_q_norm=dil_q_norm, dil_k_norm=dil_k_norm,
                   w_branch_c=w_branch_c, w_out=w_out, norm_xattn=norm_xattn, norm_mem=norm_mem, w_xq=w_xq,
                   w_xkv=w_xkv, xq_norm=xq_norm, xk_norm=xk_norm, w_xo=w_xo, norm_ffn=norm_ffn,
                   w_ff1=w_ff1, w_ff2=w_ff2)
    layers = [_prep_layer({k: v[l] for k, v in stacked.items()}) for l in range(DEPTH)]
    return (_trunk(x_prompt, mem_prompt, layers), _trunk(x_sample, mem_sample, layers))
```

```python
import functools

import jax
import jax.numpy as jnp
import numpy as np
from jax import lax
from jax.experimental import pallas as pl
from jax.experimental.pallas import tpu as pltpu

F32 = jnp.float32
BF16 = jnp.bfloat16

D_MODEL = 1024
DEPTH = 2
N_MEM = 256
ROPE_THETA = 500000.0
NORM_EPS = 1e-6
GLA_HEADS = 4
GLA_DK = 128
GLA_DV = 256
GLA_GATE_RANK = 16
GLA_TAU = 16.0
GLA_CHUNK = 64
MLA_HEADS = 8
MLA_Q_RANK = 256
MLA_KV_RANK = 128
MLA_NOPE = 64
MLA_ROPE = 32
MLA_QK = 96
MLA_V = 128
DIL_GROUPS = ((128, 1), (512, 4), (2048, 16))
DIL_HEADS = 8
DIL_HEAD_DIM = 128
DIL_ROT = 32
DIL_RADIUS = 64
X_HEADS = 4
X_HEAD_DIM = 256
D_FF = 4096

LANES = 128
ROPE_HALF = 16
VMEM_LIMIT_BYTES = 56 * 1024 * 1024

COL_GATE = 9216
PERM_TILE = 1024
HALO = 64
QK_ROWS = 256
COL_AQ, COL_AK, COL_AV, COL_AR = 12288, 12800, 13312, 14336
COL_BQ, COL_BKV, COL_SMALL = 15360, 15616, 15744
IN_COLS_PAD = 16384
SMALL_GF, SMALL_GB = 16, 32
ROPE_LO, ROPE_HI = 0, 64

NT_DIMS = (((1,), (1,)), ((), ()))
LOG2_E = 1.4426950408889634


def _cparams(*sem):
    return pltpu.CompilerParams(dimension_semantics=sem, vmem_limit_bytes=VMEM_LIMIT_BYTES)


def _rms(x, g, n=None):
    ss = jnp.sum(x * x, axis=-1, keepdims=True) * (1.0 / (n or x.shape[-1]))
    return x * lax.rsqrt(ss + NORM_EPS) * g


def _log_sigmoid(z):
    return jnp.minimum(z, 0.0) - jnp.log(1.0 + jnp.exp(-jnp.abs(z)))


def _sigmoid(z):
    return 1.0 / (1.0 + jnp.exp(-z))


def _norm_matmul_kernel(x_ref, g_ref, w_ref, o_ref, h_ref):
    @pl.when(pl.program_id(1) == 0)
    def _():
        h_ref[...] = _rms(x_ref[...].astype(F32), g_ref[...]).astype(BF16)

    o_ref[...] = jnp.dot(h_ref[...], w_ref[...], preferred_element_type=F32).astype(o_ref.dtype)


def _norm_matmul(x, g, w, out_dtype, tm, tn):
    m, k = x.shape
    n = w.shape[1]
    return pl.pallas_call(
        _norm_matmul_kernel,
        grid=(m // tm, n // tn),
        in_specs=[pl.BlockSpec((tm, k), lambda i, j: (i, 0)),
                  pl.BlockSpec((1, k), lambda i, j: (0, 0)),
                  pl.BlockSpec((k, tn), lambda i, j: (0, j))],
        out_specs=pl.BlockSpec((tm, tn), lambda i, j: (i, j)),
        out_shape=jax.ShapeDtypeStruct((m, n), out_dtype),
        scratch_shapes=[pltpu.VMEM((tm, k), BF16)],
        compiler_params=_cparams("parallel", "arbitrary"),
        name="norm_matmul",
    )(x, g, w)


def _dil_tile(comp, group):
    return len(DIL_GROUPS) * group + comp


def _in_proj_kernel(x_ref, g_ref, w_ref, nw_ref, cos_ref, sin_ref, o_ref, h_ref, hf_ref):
    j = pl.program_id(1)
    n_dil_tiles = 3 * len(DIL_GROUPS)
    wide = DIL_HEADS * DIL_HEAD_DIM
    nc = D_MODEL // LANES

    @pl.when(j == 0)
    def _():
        h = _rms(x_ref[...], g_ref[...])
        h_ref[0] = h.astype(BF16)
        for c in range(nc):
            hf_ref[c] = h[:, c * LANES:(c + 1) * LANES]
        for g, (_, dil) in enumerate(DIL_GROUPS):
            if dil == 1:
                continue
            rows = PERM_TILE // dil
            for r in range(dil):
                for c in range(nc):
                    h_ref[g, r * rows:(r + 1) * rows, c * LANES:(c + 1) * LANES] = (
                        hf_ref[c, pl.ds(r, rows, stride=dil), :].astype(BF16))

    is_dil = j < n_dil_tiles
    group = jnp.where(is_dil, j // 3, 0)
    comp = j % 3
    is_qk = is_dil & (comp < 2)

    @pl.when(is_qk)
    def _():
        nw = nw_ref[0]
        scale = jnp.where(comp == 0, DIL_HEAD_DIM ** -0.5, 1.0).astype(F32)
        for c in range(PERM_TILE // QK_ROWS):
            rs = slice(c * QK_ROWS, (c + 1) * QK_ROWS)
            acc = jnp.dot(h_ref[group, rs, :], w_ref[...], preferred_element_type=F32)
            cos = cos_ref[0, rs, :]
            sin = sin_ref[0, rs, :]
            for h in range(wide // LANES):
                hs = slice(h * LANES, (h + 1) * LANES)
                o_ref[rs, hs] = (_rope(_rms(acc[:, hs], nw), cos, sin) * scale).astype(o_ref.dtype)

    @pl.when(jnp.logical_not(is_qk))
    def _():
        o_ref[...] = jnp.dot(h_ref[group], w_ref[...], preferred_element_type=F32).astype(o_ref.dtype)


def _in_proj(x, p, tables, seq):
    m, k = x.shape
    tn = DIL_HEADS * DIL_HEAD_DIM
    ng = len(DIL_GROUPS)
    per_seq = seq // PERM_TILE
    cos, sin = tables
    group_of = lambda j: jnp.where(j < 3 * ng, j // 3, 0)
    table_spec = pl.BlockSpec((1, PERM_TILE, LANES), lambda i, j: (group_of(j), i % per_seq, 0))
    return pl.pallas_call(
        _in_proj_kernel,
        grid=(m // PERM_TILE, IN_COLS_PAD // tn),
        in_specs=[pl.BlockSpec((PERM_TILE, k), lambda i, j: (i, 0)),
                  pl.BlockSpec((1, k), lambda i, j: (0, 0)),
                  pl.BlockSpec((k, tn), lambda i, j: (0, j)),
                  pl.BlockSpec((1, 1, LANES), lambda i, j: (jnp.minimum(j % 3, 1), 0, 0)),
                  table_spec, table_spec],
        out_specs=pl.BlockSpec((PERM_TILE, tn), lambda i, j: (i, j)),
        out_shape=jax.ShapeDtypeStruct((m, IN_COLS_PAD), BF16),
        scratch_shapes=[pltpu.VMEM((ng, PERM_TILE, k), BF16),
                        pltpu.VMEM((k // LANES, PERM_TILE, LANES), F32)],
        compiler_params=_cparams("parallel", "arbitrary"),
        name="in_proj",
    )(x, p["norm_mix"], p["w_in"], p["dil_qk_norm"], cos, sin)


TN_DIMS = (((0,), (0,)), ((), ()))


def _gla_kernel(*refs, tb, reverse, final):
    if final:
        q_ref, k_ref, v_ref, sm_ref, wg_ref, bg_ref, of_ref, r_ref, gn_ref, o_ref, st_ref = refs
    else:
        q_ref, k_ref, v_ref, sm_ref, wg_ref, bg_ref, o_ref, st_ref = refs

    @pl.when(pl.program_id(1) == 0)
    def _():
        st_ref[...] = jnp.zeros_like(st_ref)

    ck = GLA_CHUNK
    la = _log_sigmoid(jnp.dot(sm_ref[...], wg_ref[...], preferred_element_type=F32) + bg_ref[...]) * (1.0 / GLA_TAU)

    r = lax.broadcasted_iota(jnp.int32, (ck, ck), 0)
    c = lax.broadcasted_iota(jnp.int32, (ck, ck), 1)
    tri_mask = (r <= c) if reverse else (r >= c)
    tri = jnp.where(tri_mask, 1.0, 0.0).astype(BF16)
    end = 0 if reverse else ck - 1
    scale = GLA_DK ** -0.5

    n_chunks = tb // ck
    order = range(n_chunks - 1, -1, -1) if reverse else range(n_chunks)
    for ci in order:
        sl = slice(ci * ck, (ci + 1) * ck)
        la_c = la[sl]
        hi = la_c.astype(BF16)
        lo = (la_c - hi.astype(F32)).astype(BF16)
        b = jnp.dot(tri, hi, preferred_element_type=F32) + jnp.dot(tri, lo, preferred_element_type=F32)
        b_end = b[end:end + 1, :]
        kc = k_ref[sl, :].astype(F32)
        qd = (q_ref[sl, :].astype(F32) * scale * jnp.exp(b)).astype(BF16)
        kinv = (kc * jnp.exp(-b)).astype(BF16)
        ktail = (kc * jnp.exp(b_end - b)).astype(BF16)
        dec = jnp.exp(b_end)
        for h in range(GLA_HEADS):
            ks = slice(h * GLA_DK, (h + 1) * GLA_DK)
            vs = slice(h * GLA_DV, (h + 1) * GLA_DV)
            qh = qd[:, ks]
            att = lax.dot_general(qh, kinv[:, ks], NT_DIMS, preferred_element_type=F32)
            att = jnp.where(tri_mask, att, 0.0).astype(BF16)
            vh = v_ref[sl, vs]
            st = st_ref[h]
            o = (jnp.dot(att, vh, preferred_element_type=F32)
                 + lax.dot_general(qh, st.astype(BF16), NT_DIMS, preferred_element_type=F32))
            st_ref[h] = dec[:, ks] * st + lax.dot_general(vh, ktail[:, ks], TN_DIMS, preferred_element_type=F32)
            if final:
                o = _rms(o + of_ref[sl, vs], gn_ref[...])
                rr = r_ref[sl, vs].astype(F32)
                o = o * (rr * _sigmoid(rr))
            o_ref[sl, vs] = o.astype(o_ref.dtype)


def _gla_direction(proj, wg, bg, batch, seq, reverse, o_fwd=None, g_out=None, tb=256):
    nb = seq // tb
    m = batch * seq
    final = o_fwd is not None

    def row(b, i):
        return b * nb + (nb - 1 - i if reverse else i)

    def col(block_cols, off):
        return lambda b, i: (row(b, i), off // block_cols)

    hk = GLA_HEADS * GLA_DK
    hv = GLA_HEADS * GLA_DV
    const = lambda b, i: (0, 0)
    in_specs = [pl.BlockSpec((tb, hk), col(hk, COL_AQ)),
                pl.BlockSpec((tb, hk), col(hk, COL_AK)),
                pl.BlockSpec((tb, hv), col(hv, COL_AV)),
                pl.BlockSpec((tb, LANES), col(LANES, COL_SMALL)),
                pl.BlockSpec((LANES, hk), const),
                pl.BlockSpec((1, hk), const)]
    args = [proj, proj, proj, proj, wg, bg]
    if final:
        in_specs += [pl.BlockSpec((tb, hv), col(hv, 0)),
                     pl.BlockSpec((tb, hv), col(hv, COL_AR)),
                     pl.BlockSpec((1, GLA_DV), const)]
        args += [o_fwd, proj, g_out]
    return pl.pallas_call(
        functools.partial(_gla_kernel, tb=tb, reverse=reverse, final=final),
        grid=(batch, nb),
        in_specs=in_specs,
        out_specs=pl.BlockSpec((tb, hv), col(hv, 0)),
        out_shape=jax.ShapeDtypeStruct((m, hv), BF16 if final else F32),
        scratch_shapes=[pltpu.VMEM((GLA_HEADS, GLA_DV, GLA_DK), F32)],
        compiler_params=_cparams("parallel", "arbitrary"),
        name="gla_bwd" if reverse else "gla_fwd",
    )(*args)


def _rope(y, cos, sin_signed):
    return y * cos + pltpu.roll(y, LANES // 2, 1) * sin_signed


def _mla_prep_kernel(bq_ref, bkv_ref, sm_ref, qln_ref, wuq_ref, kvln_ref, wuk_ref, wuv_ref,
                     qn_ref, kn_ref, cos_ref, sin_ref, q_out, k_out, v_out):
    hq = _rms(bq_ref[...].astype(F32), qln_ref[...]).astype(BF16)
    hkv = _rms(bkv_ref[...].astype(F32), kvln_ref[...]).astype(BF16)
    q = jnp.dot(hq, wuq_ref[...], preferred_element_type=F32)
    kn = jnp.dot(hkv, wuk_ref[...], preferred_element_type=F32)
    v = jnp.dot(hkv, wuv_ref[...], preferred_element_type=F32)
    tm = q.shape[0]
    lane = lax.broadcasted_iota(jnp.int32, (tm, LANES), 1)
    sm = sm_ref[...].astype(F32)
    in_rope = (lane < ROPE_LO + ROPE_HALF) | ((lane >= ROPE_HI) & (lane < ROPE_HI + ROPE_HALF))
    kpe = jnp.where(in_rope, sm, 0.0)
    cos = cos_ref[...]
    sin = sin_ref[...]
    scale = MLA_QK ** -0.5 * LOG2_E
    ones = jnp.ones((tm, LANES), v_out.dtype)
    for h in range(MLA_HEADS):
        hs = slice(h * LANES, (h + 1) * LANES)
        qh = _rope(_rms(q[:, hs], qn_ref[...], MLA_QK), cos, sin)
        q_out[:, hs] = (qh * scale).astype(q_out.dtype)
        kh = _rope(_rms(kn[:, hs] + kpe, kn_ref[...], MLA_QK), cos, sin)
        k_out[:, hs] = kh.astype(k_out.dtype)
        v_out[:, 2 * h * LANES:(2 * h + 1) * LANES] = v[:, hs].astype(v_out.dtype)
        v_out[:, (2 * h + 1) * LANES:(2 * h + 2) * LANES] = ones


def _mla_prep(proj, p, cos, sin, seq, tm=512):
    m = proj.shape[0]
    ns = seq // tm
    const = lambda i: (0, 0)
    wide = MLA_HEADS * LANES
    out = jax.ShapeDtypeStruct((m, wide), BF16)
    return pl.pallas_call(
        _mla_prep_kernel,
        grid=(m // tm,),
        in_specs=[pl.BlockSpec((tm, MLA_Q_RANK), lambda i: (i, COL_BQ // MLA_Q_RANK)),
                  pl.BlockSpec((tm, MLA_KV_RANK), lambda i: (i, COL_BKV // MLA_KV_RANK)),
                  pl.BlockSpec((tm, LANES), lambda i: (i, COL_SMALL // LANES)),
                  pl.BlockSpec((1, MLA_Q_RANK), const),
                  pl.BlockSpec((MLA_Q_RANK, wide), const),
                  pl.BlockSpec((1, MLA_KV_RANK), const),
                  pl.BlockSpec((MLA_KV_RANK, wide), const),
                  pl.BlockSpec((MLA_KV_RANK, wide), const),
                  pl.BlockSpec((1, LANES), const),
                  pl.BlockSpec((1, LANES), const),
                  pl.BlockSpec((tm, LANES), lambda i: (i % ns, 0)),
                  pl.BlockSpec((tm, LANES), lambda i: (i % ns, 0))],
        out_specs=[pl.BlockSpec((tm, wide), lambda i: (i, 0)), pl.BlockSpec((tm, wide), lambda i: (i, 0)),
                   pl.BlockSpec((tm, 2 * wide), lambda i: (i, 0))],
        out_shape=[out, out, jax.ShapeDtypeStruct((m, 2 * wide), BF16)],
        compiler_params=_cparams("parallel"),
        name="mla_prep",
    )(proj, proj, proj, p["mla_q_lat_norm"], p["mla_w_uq"], p["mla_kv_lat_norm"], p["mla_w_uk"],
      p["mla_w_uv"], p["mla_q_norm"], p["mla_k_norm"], cos, sin)


def _mla_attn_kernel(q_ref, k_ref, v_ref, o_ref, *, tk, unroll):
    q = q_ref[...]
    tq = q.shape[0]
    nk = k_ref.shape[0] // tk

    def body(t, carry):
        m, acc = carry
        start = pl.multiple_of(t * tk, tk)
        s = lax.dot_general(q, k_ref[pl.ds(start, tk), :], NT_DIMS, preferred_element_type=F32)
        m_new = jnp.maximum(m, jnp.max(s, axis=-1, keepdims=True))
        p = jnp.exp2(s - m_new).astype(BF16)
        acc = jnp.exp2(m - m_new) * acc + jnp.dot(p, v_ref[pl.ds(start, tk), :], preferred_element_type=F32)
        return m_new, acc

    init = (jnp.full((tq, 1), -jnp.inf, F32), jnp.zeros((tq, 2 * LANES), F32))
    _, acc = lax.fori_loop(0, nk, body, init, unroll=unroll)
    o_ref[...] = (acc[:, :LANES] / acc[:, LANES:]).astype(o_ref.dtype)


def _mla_attn(q, k, v, batch, seq, tk=512, max_unroll=16):
    tq = 512 if seq // tk <= max_unroll // 4 else 256
    nq = seq // tq
    return pl.pallas_call(
        functools.partial(_mla_attn_kernel, tk=tk, unroll=min(max_unroll, seq // tk)),
        grid=(batch, MLA_HEADS, nq),
        in_specs=[pl.BlockSpec((tq, LANES), lambda b, h, i: (b * nq + i, h)),
                  pl.BlockSpec((seq, LANES), lambda b, h, i: (b, h)),
                  pl.BlockSpec((seq, 2 * LANES), lambda b, h, i: (b, h))],
        out_specs=pl.BlockSpec((tq, LANES), lambda b, h, i: (b * nq + i, h)),
        out_shape=jax.ShapeDtypeStruct(q.shape, BF16),
        compiler_params=_cparams("parallel", "parallel", "arbitrary"),
        name="mla_attn",
    )(q, k, v)


DIL_Q_BLOCKS = 2
DIL_K_BLOCKS = DIL_Q_BLOCKS + 2


def _dil_attn_kernel(*refs, sub_len):
    q_ref = refs[0]
    k_refs = refs[1:1 + DIL_K_BLOCKS]
    v_refs = refs[1 + DIL_K_BLOCKS:1 + 2 * DIL_K_BLOCKS]
    o_ref, lse_ref = refs[1 + 2 * DIL_K_BLOCKS:]
    pieces, piece_rows = q_ref.shape[0], q_ref.shape[1]
    tl = DIL_Q_BLOCKS * HALO
    nkeys = DIL_K_BLOCKS * HALO
    row = lax.broadcasted_iota(jnp.int32, (tl, nkeys), 0)
    col = lax.broadcasted_iota(jnp.int32, (tl, nkeys), 1)
    rel = col - row
    pos = pl.program_id(2) * tl - DIL_RADIUS + col
    valid = (rel >= 0) & (rel <= 2 * DIL_RADIUS) & (pos >= 0) & (pos < sub_len)
    lane = lax.broadcasted_iota(jnp.int32, (tl, LANES), 1)
    lse_all = jnp.zeros((tl, LANES), F32)
    for h in range(DIL_HEADS):
        hs = slice(h * LANES, (h + 1) * LANES)
        qq = jnp.concatenate([q_ref[n, :, hs] for n in range(pieces)], axis=0)
        kk = jnp.concatenate([r[:, hs] for r in k_refs], axis=0)
        vv = jnp.concatenate([r[:, hs] for r in v_refs], axis=0)
        s = lax.dot_general(qq, kk, NT_DIMS, preferred_element_type=F32)
        s = jnp.where(valid, s, -jnp.inf)
        m = jnp.max(s, axis=-1, keepdims=True)
        p = jnp.exp(s - m)
        l = jnp.sum(p, axis=-1, keepdims=True)
        o = jnp.dot(p.astype(BF16), vv, preferred_element_type=F32) / l
        for n in range(pieces):
            o_ref[n, :, hs] = o[n * piece_rows:(n + 1) * piece_rows]
        lse_all = jnp.where(lane == h, m + jnp.log(l), lse_all)
    for n in range(pieces):
        lse_ref[n] = lse_all[n * piece_rows:(n + 1) * piece_rows]


def _dil_attn(proj, group, batch, seq):
    _, dil = DIL_GROUPS[group]
    sub_len = seq // dil
    n_blocks = sub_len // HALO
    chunk_blocks = PERM_TILE // dil // HALO
    tile_blocks = PERM_TILE // HALO
    wide = DIL_HEADS * DIL_HEAD_DIM

    def row_block(b, r, n):
        n = jnp.clip(n, 0, n_blocks - 1)
        return (b * (seq // HALO) + (n // chunk_blocks) * tile_blocks + r * chunk_blocks + n % chunk_blocks)

    def halo_spec(comp, offset):
        return pl.BlockSpec((HALO, wide), lambda b, r, i: (row_block(b, r, DIL_Q_BLOCKS * i + offset),
                                                          _dil_tile(comp, group)))

    tl = DIL_Q_BLOCKS * HALO
    chunk = PERM_TILE // dil
    pieces = max(1, tl // chunk)
    piece_rows = tl // pieces
    subs = chunk // piece_rows
    m = batch * seq
    lead = m // (PERM_TILE * pieces)
    per_seq = seq // (PERM_TILE * pieces)

    def query_view(width):
        return (lead, pieces, dil, chunk, width)

    def query_spec(width, col_block):
        return pl.BlockSpec((None, pieces, None, piece_rows, width),
                            lambda b, r, i: (b * per_seq + i // subs, 0, r, i % subs, col_block))

    o, lse = pl.pallas_call(
        functools.partial(_dil_attn_kernel, sub_len=sub_len),
        grid=(batch, dil, n_blocks // DIL_Q_BLOCKS),
        in_specs=([query_spec(wide, _dil_tile(0, group))]
                  + [halo_spec(1, n - 1) for n in range(DIL_K_BLOCKS)]
                  + [halo_spec(2, n - 1) for n in range(DIL_K_BLOCKS)]),
        out_specs=[query_spec(wide, 0), query_spec(LANES, 0)],
        out_shape=[jax.ShapeDtypeStruct(query_view(wide), F32), jax.ShapeDtypeStruct(query_view(LANES), F32)],
        compiler_params=_cparams("parallel", "parallel", "arbitrary"),
        name=f"dil_attn_g{group}",
    )(proj.reshape(query_view(IN_COLS_PAD)), *([proj] * (2 * DIL_K_BLOCKS)))
    return o.reshape(m, wide), lse.reshape(m, LANES)


def _combine_kernel(x_ref, a_ref, b_ref, o0_ref, o1_ref, o2_ref, l0_ref, l1_ref, l2_ref, gate_ref,
                    wa_ref, wb_ref, wc_ref, wo_ref, out_ref, c_ref, on_refs, ln_refs):
    tm = x_ref.shape[0]
    o_tok, l_tok = [], []
    for g, (o_ref, l_ref) in enumerate(((o0_ref, l0_ref), (o1_ref, l1_ref), (o2_ref, l2_ref))):
        dil = DIL_GROUPS[g][1]
        if dil == 1:
            o_tok.append(lambda h, o_ref=o_ref: o_ref[0, 0, :, h * LANES:(h + 1) * LANES])
            l_tok.append(l_ref[0, 0])
            continue
        for r in range(dil):
            rows = pl.ds(r, tm // dil, stride=dil)
            ln_refs[g - 1, rows, :] = l_ref[0, r]
            for h in range(DIL_HEADS):
                on_refs[g - 1, h, rows, :] = o_ref[0, r, :, h * LANES:(h + 1) * LANES]
        o_tok.append(lambda h, g=g: on_refs[g - 1, h])
        l_tok.append(ln_refs[g - 1])
    l0, l1, l2 = l_tok
    mx = jnp.maximum(jnp.maximum(l0, l1), l2)
    e0, e1, e2 = jnp.exp(l0 - mx), jnp.exp(l1 - mx), jnp.exp(l2 - mx)
    inv = 1.0 / (e0 + e1 + e2)
    w0, w1, w2 = e0 * inv, e1 * inv, e2 * inv
    for h in range(DIL_HEADS):
        hs = slice(h * LANES, (h + 1) * LANES)
        c = w0[:, h:h + 1] * o_tok[0](h) + w1[:, h:h + 1] * o_tok[1](h) + w2[:, h:h + 1] * o_tok[2](h)
        c_ref[:, hs] = c.astype(BF16)
    ya = jnp.dot(a_ref[...], wa_ref[...], preferred_element_type=F32)
    yb = jnp.dot(b_ref[...], wb_ref[...], preferred_element_type=F32)
    yc = jnp.dot(c_ref[...], wc_ref[...], preferred_element_type=F32)
    d = D_MODEL
    mix = (_sigmoid(gate_ref[:, 0:d].astype(F32)) * ya
           + _sigmoid(gate_ref[:, d:2 * d].astype(F32)) * yb
           + _sigmoid(gate_ref[:, 2 * d:3 * d].astype(F32)) * yc)
    out_ref[...] = x_ref[...] + jnp.dot(mix.astype(BF16), wo_ref[...], preferred_element_type=F32)


def _combine(x, gla_o, mla_o, dil_o, dil_lse, proj, p, batch, seq, tm=256):
    m = x.shape[0]
    d = D_MODEL
    ns = PERM_TILE // tm
    rowblk = lambda i: (i, 0)
    const = lambda i: (0, 0)
    wspec = pl.BlockSpec((d, d), const)

    def residue_major(arr, group, width):
        dil = DIL_GROUPS[group][1]
        view = arr.reshape(m // PERM_TILE, dil, PERM_TILE // dil, width)
        return view, pl.BlockSpec((1, dil, tm // dil, width), lambda i: (i // ns, 0, i % ns, 0))

    o_views, o_specs = zip(*[residue_major(dil_o[g], g, d) for g in range(len(DIL_GROUPS))])
    l_views, l_specs = zip(*[residue_major(dil_lse[g], g, LANES) for g in range(len(DIL_GROUPS))])
    n_perm = len(DIL_GROUPS) - 1
    return pl.pallas_call(
        _combine_kernel,
        grid=(m // tm,),
        in_specs=[pl.BlockSpec((tm, d), rowblk), pl.BlockSpec((tm, d), rowblk), pl.BlockSpec((tm, d), rowblk),
                  *o_specs, *l_specs,
                  pl.BlockSpec((tm, 3 * d), lambda i: (i, COL_GATE // (3 * d))),
                  wspec, wspec, wspec, wspec],
        out_specs=pl.BlockSpec((tm, d), rowblk),
        out_shape=jax.ShapeDtypeStruct((m, d), F32),
        scratch_shapes=[pltpu.VMEM((tm, d), BF16), pltpu.VMEM((n_perm, DIL_HEADS, tm, LANES), F32),
                        pltpu.VMEM((n_perm, tm, LANES), F32)],
        compiler_params=_cparams("parallel"),
        name="combine",
    )(x, gla_o, mla_o, *o_views, *l_views, proj,
      p["w_branch_a"], p["w_branch_b"], p["w_branch_c"], p["w_out"])


def _xattn_kernel(x_ref, g_ref, wq_ref, qn_ref, kv_ref, kn_ref, wo_ref, out_ref, o_ref):
    x = x_ref[...]
    h = _rms(x, g_ref[...]).astype(BF16)
    q = jnp.dot(h, wq_ref[...], preferred_element_type=F32)
    scale = X_HEAD_DIM ** -0.5
    for hd in range(X_HEADS):
        hs = slice(hd * X_HEAD_DIM, (hd + 1) * X_HEAD_DIM)
        vs = slice(D_MODEL + hd * X_HEAD_DIM, D_MODEL + (hd + 1) * X_HEAD_DIM)
        qh = (_rms(q[:, hs], qn_ref[...]) * scale).astype(BF16)
        kh = _rms(kv_ref[:, hs].astype(F32), kn_ref[...]).astype(BF16)
        s = lax.dot_general(qh, kh, NT_DIMS, preferred_element_type=F32)
        p = jnp.exp(s - jnp.max(s, axis=-1, keepdims=True))
        l = jnp.sum(p, axis=-1, keepdims=True)
        o = jnp.dot(p.astype(BF16), kv_ref[:, vs], preferred_element_type=F32) / l
        o_ref[:, hs] = o.astype(BF16)
    out_ref[...] = x + jnp.dot(o_ref[...], wo_ref[...], preferred_element_type=F32)


def _xattn(x, kv, p, seq, tm=512):
    m = x.shape[0]
    d = D_MODEL
    per_seq = seq // tm
    const = lambda i: (0, 0)
    return pl.pallas_call(
        _xattn_kernel,
        grid=(m // tm,),
        in_specs=[pl.BlockSpec((tm, d), lambda i: (i, 0)),
                  pl.BlockSpec((1, d), const),
                  pl.BlockSpec((d, d), const),
                  pl.BlockSpec((1, X_HEAD_DIM), const),
                  pl.BlockSpec((N_MEM, 2 * d), lambda i: (i // per_seq, 0)),
                  pl.BlockSpec((1, X_HEAD_DIM), const),
                  pl.BlockSpec((d, d), const)],
        out_specs=pl.BlockSpec((tm, d), lambda i: (i, 0)),
        out_shape=jax.ShapeDtypeStruct((m, d), F32),
        scratch_shapes=[pltpu.VMEM((tm, d), BF16)],
        compiler_params=_cparams("parallel"),
        name="xattn",
    )(x, p["norm_xattn"], p["w_xq"], p["xq_norm"], kv, p["xk_norm"], p["w_xo"])


def _ffn_kernel(x_ref, g_ref, w1_ref, w2_ref, out_ref, h_ref, acc_ref):
    j = pl.program_id(1)

    @pl.when(j == 0)
    def _():
        h_ref[...] = _rms(x_ref[...], g_ref[...]).astype(BF16)
        acc_ref[...] = x_ref[...]

    u = jnp.maximum(jnp.dot(h_ref[...], w1_ref[...], preferred_element_type=F32), 0.0)
    acc_ref[...] += jnp.dot((u * u).astype(BF16), w2_ref[...], preferred_element_type=F32)

    @pl.when(j == pl.num_programs(1) - 1)
    def _():
        out_ref[...] = acc_ref[...]


def _ffn(x, p, tm=1024, tf=1024):
    m = x.shape[0]
    d = D_MODEL
    return pl.pallas_call(
        _ffn_kernel,
        grid=(m // tm, D_FF // tf),
        in_specs=[pl.BlockSpec((tm, d), lambda i, j: (i, 0)),
                  pl.BlockSpec((1, d), lambda i, j: (0, 0)),
                  pl.BlockSpec((d, tf), lambda i, j: (0, j)),
                  pl.BlockSpec((tf, d), lambda i, j: (j, 0))],
        out_specs=pl.BlockSpec((tm, d), lambda i, j: (i, 0)),
        out_shape=jax.ShapeDtypeStruct((m, d), F32),
        scratch_shapes=[pltpu.VMEM((tm, d), BF16), pltpu.VMEM((tm, d), F32)],
        compiler_params=_cparams("parallel", "arbitrary"),
        name="ffn",
    )(x, p["norm_ffn"], p["w_ff1"], p["w_ff2"])


def _rope_tables(positions):
    seq = len(positions)
    inv_freq = 1.0 / (ROPE_THETA ** (jnp.arange(0, 2 * ROPE_HALF, 2, dtype=F32) / (2 * ROPE_HALF)))
    ang = jnp.asarray(positions, F32)[:, None] * inv_freq[None, :]
    cos, sin = jnp.cos(ang), jnp.sin(ang)
    gap = ROPE_HI - ROPE_LO - ROPE_HALF
    tail = LANES - ROPE_HI - ROPE_HALF
    fill = lambda n, v: jnp.full((seq, n), v, F32)
    cos_t = jnp.concatenate([fill(ROPE_LO, 1.0), cos, fill(gap, 1.0), cos, fill(tail, 1.0)], axis=1)
    sin_t = jnp.concatenate([fill(ROPE_LO, 0.0), -sin, fill(gap, 0.0), sin, fill(tail, 0.0)], axis=1)
    return cos_t, sin_t


def _lanes_after(*taken):
    used = set(int(i) for t in taken for i in t)
    return [i for i in range(LANES) if i not in used]


_ROPE_LANES = list(range(ROPE_LO, ROPE_LO + ROPE_HALF)) + list(range(ROPE_HI, ROPE_HI + ROPE_HALF))
_MLA_LANE_OF_DIM = np.array(_lanes_after(_ROPE_LANES)[:MLA_NOPE] + _ROPE_LANES, np.int32)
_DIL_LANE_OF_DIM = np.array(_ROPE_LANES + _lanes_after(_ROPE_LANES), np.int32)


def _to_lanes(arr, lane_of_dim):
    dim_of_lane = np.full(LANES, -1)
    dim_of_lane[lane_of_dim] = np.arange(len(lane_of_dim))
    pieces, lane = [], 0
    while lane < LANES:
        run = 1
        while (lane + run < LANES and
               (dim_of_lane[lane + run] == dim_of_lane[lane] + run if dim_of_lane[lane] >= 0
                else dim_of_lane[lane + run] < 0)):
            run += 1
        start = int(dim_of_lane[lane])
        pieces.append(arr[..., start:start + run] if start >= 0
                      else jnp.zeros(arr.shape[:-1] + (run,), arr.dtype))
        lane += run
    return jnp.concatenate(pieces, axis=-1)


def _prep_layer(w):
    d = D_MODEL
    hk = GLA_HEADS * GLA_DK
    hv = GLA_HEADS * GLA_DV
    dil_w = len(DIL_GROUPS) * DIL_HEADS * DIL_HEAD_DIM
    sizes = (hk, hk, hv, hv, GLA_GATE_RANK, GLA_GATE_RANK, MLA_Q_RANK, MLA_KV_RANK, MLA_ROPE,
             dil_w, dil_w, dil_w, 3 * d)
    offs = [0]
    for s in sizes:
        offs.append(offs[-1] + s)
    (a_q, a_k, a_v, a_r, a_gf, a_gb, b_q, b_kv, b_kpe, c_q, c_k, c_v, gates) = [
        w["w_in"][:, offs[i]:offs[i + 1]] for i in range(len(sizes))]
    z = lambda n: jnp.zeros((d, n), F32)
    small_lane_of_dim = np.concatenate([_MLA_LANE_OF_DIM[MLA_NOPE:], SMALL_GF + np.arange(GLA_GATE_RANK),
                                        SMALL_GB + np.arange(GLA_GATE_RANK)])
    small = _to_lanes(jnp.concatenate([b_kpe, a_gf, a_gb], axis=1), small_lane_of_dim)

    def dil_heads(cols):
        per_head = cols.reshape(d, len(DIL_GROUPS) * DIL_HEADS, DIL_HEAD_DIM)
        return _to_lanes(per_head, _DIL_LANE_OF_DIM).reshape(d, dil_w)

    gw = DIL_HEADS * DIL_HEAD_DIM
    dil_tiles = [t[:, g * gw:(g + 1) * gw] for g in range(len(DIL_GROUPS))
                 for t in (dil_heads(c_q), dil_heads(c_k), c_v)]
    w_in = jnp.concatenate(dil_tiles + [gates, a_q, a_k, a_v, a_r, b_q, b_kv, small,
                                        z(IN_COLS_PAD - COL_SMALL - LANES)], axis=1).astype(BF16)

    def gate_w(wg, lane0):
        rows = lambda n: jnp.zeros((n, hk), F32)
        return jnp.concatenate([rows(lane0), wg, rows(LANES - lane0 - GLA_GATE_RANK)], axis=0).astype(BF16)

    wgf = gate_w(w["gla_w_gate_f"], SMALL_GF)
    wgb = gate_w(w["gla_w_gate_b"], SMALL_GB)

    w_uq = _to_lanes(w["mla_w_uq"].reshape(MLA_Q_RANK, MLA_HEADS, MLA_QK), _MLA_LANE_OF_DIM)
    w_uq = w_uq.reshape(MLA_Q_RANK, MLA_HEADS * LANES)
    w_ukv = w["mla_w_ukv"].reshape(MLA_KV_RANK, MLA_HEADS, MLA_NOPE + MLA_V)
    w_uk = _to_lanes(w_ukv[:, :, :MLA_NOPE], _MLA_LANE_OF_DIM[:MLA_NOPE])
    w_uk = w_uk.reshape(MLA_KV_RANK, MLA_HEADS * LANES)
    w_uv = w_ukv[:, :, MLA_NOPE:].reshape(MLA_KV_RANK, MLA_HEADS * MLA_V)
    row = lambda v: v.reshape(1, -1).astype(F32)
    return {
        "norm_mix": row(w["norm_mix"]), "w_in": w_in,
        "gla_wgf": wgf, "gla_bgf": row(w["gla_b_gate_f"]),
        "gla_wgb": wgb, "gla_bgb": row(w["gla_b_gate_b"]),
        "gla_out_norm": row(w["gla_out_norm"]), "w_branch_a": w["w_branch_a"].astype(BF16),
        "mla_q_lat_norm": row(w["mla_q_lat_norm"]), "mla_w_uq": w_uq.astype(BF16),
        "mla_kv_lat_norm": row(w["mla_kv_lat_norm"]), "mla_w_uk": w_uk.astype(BF16),
        "mla_w_uv": w_uv.astype(BF16),
        "mla_q_norm": _to_lanes(w["mla_q_norm"], _MLA_LANE_OF_DIM).reshape(1, LANES),
        "mla_k_norm": _to_lanes(w["mla_k_norm"], _MLA_LANE_OF_DIM).reshape(1, LANES),
        "w_branch_b": w["w_branch_b"].astype(BF16),
        "dil_qk_norm": _to_lanes(jnp.stack([w["dil_q_norm"], w["dil_k_norm"]]),
                                 _DIL_LANE_OF_DIM).reshape(2, 1, LANES),
        "w_branch_c": w["w_branch_c"].astype(BF16), "w_out": w["w_out"].astype(BF16),
        "norm_xattn": row(w["norm_xattn"]), "norm_mem": row(w["norm_mem"]),
        "w_xq": w["w_xq"].astype(BF16), "w_xkv": w["w_xkv"].astype(BF16),
        "xq_norm": row(w["xq_norm"]), "xk_norm": row(w["xk_norm"]), "w_xo": w["w_xo"].astype(BF16),
        "norm_ffn": row(w["norm_ffn"]), "w_ff1": w["w_ff1"].astype(BF16), "w_ff2": w["w_ff2"].astype(BF16),
    }


def _layer(x, mem, p, tables, batch, seq):
    (cos_m, sin_m), dil_tables = tables
    proj = _in_proj(x, p, dil_tables, seq)
    o_fwd = _gla_direction(proj, p["gla_wgf"], p["gla_bgf"], batch, seq, False)
    gla_o = _gla_direction(proj, p["gla_wgb"], p["gla_bgb"], batch, seq, True,
                           o_fwd=o_fwd, g_out=p["gla_out_norm"])
    q_m, k_m, v_m = _mla_prep(proj, p, cos_m, sin_m, seq)
    mla_o = _mla_attn(q_m, k_m, v_m, batch, seq)
    dil = [_dil_attn(proj, g, batch, seq) for g in range(len(DIL_GROUPS))]
    x = _combine(x, gla_o, mla_o, [o for o, _ in dil], [l for _, l in dil], proj, p, batch, seq)
    kv = _norm_matmul(mem, p["norm_mem"], p["w_xkv"], BF16, tm=N_MEM, tn=1024)
    x = _xattn(x, kv, p, seq)
    return _ffn(x, p)


def _trunk(x, mem, layers):
    batch, seq, d = x.shape
    natural = np.arange(seq)
    row_pos = [natural.reshape(seq // PERM_TILE, PERM_TILE // dil, dil).transpose(0, 2, 1).reshape(seq)
               for _, dil in DIL_GROUPS]
    dil_cos, dil_sin = zip(*[_rope_tables(pos) for pos in row_pos])
    tables = (_rope_tables(natural), (jnp.stack(dil_cos), jnp.stack(dil_sin)))
    xf = x.reshape(batch * seq, d)
    memf = mem.reshape(batch * mem.shape[1], d)
    for p in layers:
        xf = _layer(xf, memf, p, tables, batch, seq)
    return xf.reshape(batch, seq, d)


def kernel(x_prompt, x_sample, mem_prompt, mem_sample, norm_mix, w_in, gla_w_gate_f, gla_b_gate_f, gla_w_gate_b, gla_b_gate_b, gla_out_norm, w_branch_a, mla_q_lat_norm, mla_w_uq, mla_kv_lat_norm, mla_w_ukv, mla_q_norm, mla_k_norm, w_branch_b, dil_q_norm, dil_k_norm, w_branch_c, w_out, norm_xattn, norm_mem, w_xq, w_xkv, xq_norm, xk_norm, w_xo, norm_ffn, w_ff1, w_ff2):
    stacked = dict(norm_mix=norm_mix, w_in=w_in, gla_w_gate_f=gla_w_gate_f, gla_b_gate_f=gla_b_gate_f,
                   gla_w_gate_b=gla_w_gate_b, gla_b_gate_b=gla_b_gate_b, gla_out_norm=gla_out_norm,
                   w_branch_a=w_branch_a, mla_q_lat_norm=mla_q_lat_norm, mla_w_uq=mla_w_uq,
                   mla_kv_lat_norm=mla_kv_lat_norm, mla_w_ukv=mla_w_ukv, mla_q_norm=mla_q_norm,
                   mla_k_norm=mla_k_norm, w_branch_b=w_branch_b, dil_q_norm=dil_q_norm, dil_k_norm=dil_k_norm,
                   w_branch_c=w_branch_c, w_out=w_out, norm_xattn=norm_xattn, norm_mem=norm_mem, w_xq=w_xq,
                   w_xkv=w_xkv, xq_norm=xq_norm, xk_norm=xk_norm, w_xo=w_xo, norm_ffn=norm_ffn,
                   w_ff1=w_ff1, w_ff2=w_ff2)
    layers = [_prep_layer({k: v[l] for k, v in stacked.items()}) for l in range(DEPTH)]
    return (_trunk(x_prompt, mem_prompt, layers), _trunk(x_sample, mem_sample, layers))
```

```python
import functools

import jax
import jax.numpy as jnp
import numpy as np
from jax import lax
from jax.experimental import pallas as pl
from jax.experimental.pallas import tpu as pltpu

F32 = jnp.float32
BF16 = jnp.bfloat16

D_MODEL = 1024
DEPTH = 2
N_MEM = 256
ROPE_THETA = 500000.0
NORM_EPS = 1e-6
GLA_HEADS = 4
GLA_DK = 128
GLA_DV = 256
GLA_GATE_RANK = 16
GLA_TAU = 16.0
GLA_CHUNK = 64
MLA_HEADS = 8
MLA_Q_RANK = 256
MLA_KV_RANK = 128
MLA_NOPE = 64
MLA_ROPE = 32
MLA_QK = 96
MLA_V = 128
DIL_GROUPS = ((128, 1), (512, 4), (2048, 16))
DIL_HEADS = 8
DIL_HEAD_DIM = 128
DIL_ROT = 32
DIL_RADIUS = 64
X_HEADS = 4
X_HEAD_DIM = 256
D_FF = 4096

LANES = 128
ROPE_HALF = 16
VMEM_LIMIT_BYTES = 56 * 1024 * 1024

COL_GATE = 9216
PERM_TILE = 1024
HALO = 64
QK_ROWS = 256
COL_AQ, COL_AK, COL_AV, COL_AR = 12288, 12800, 13312, 14336
COL_BQ, COL_BKV, COL_SMALL = 15360, 15616, 15744
IN_COLS_PAD = 16384
SMALL_GF, SMALL_GB = 16, 32
ROPE_LO, ROPE_HI = 0, 64

NT_DIMS = (((1,), (1,)), ((), ()))
LOG2_E = 1.4426950408889634


def _cparams(*sem):
    return pltpu.CompilerParams(dimension_semantics=sem, vmem_limit_bytes=VMEM_LIMIT_BYTES)


def _rms(x, g, n=None):
    ss = jnp.sum(x * x, axis=-1, keepdims=True) * (1.0 / (n or x.shape[-1]))
    return x * lax.rsqrt(ss + NORM_EPS) * g


def _log_sigmoid(z):
    return jnp.minimum(z, 0.0) - jnp.log(1.0 + jnp.exp(-jnp.abs(z)))


def _sigmoid(z):
    return 1.0 / (1.0 + jnp.exp(-z))


def _norm_matmul_kernel(x_ref, g_ref, w_ref, o_ref, h_ref):
    @pl.when(pl.program_id(1) == 0)
    def _():
        h_ref[...] = _rms(x_ref[...].astype(F32), g_ref[...]).astype(BF16)

    o_ref[...] = jnp.dot(h_ref[...], w_ref[...], preferred_element_type=F32).astype(o_ref.dtype)


def _norm_matmul(x, g, w, out_dtype, tm, tn):
    m, k = x.shape
    n = w.shape[1]
    return pl.pallas_call(
        _norm_matmul_kernel,
        grid=(m // tm, n // tn),
        in_specs=[pl.BlockSpec((tm, k), lambda i, j: (i, 0)),
                  pl.BlockSpec((1, k), lambda i, j: (0, 0)),
                  pl.BlockSpec((k, tn), lambda i, j: (0, j))],
        out_specs=pl.BlockSpec((tm, tn), lambda i, j: (i, j)),
        out_shape=jax.ShapeDtypeStruct((m, n), out_dtype),
        scratch_shapes=[pltpu.VMEM((tm, k), BF16)],
        compiler_params=_cparams("parallel", "arbitrary"),
        name="norm_matmul",
    )(x, g, w)


def _dil_tile(comp, group):
    return len(DIL_GROUPS) * group + comp


def _in_proj_kernel(x_ref, g_ref, w_ref, nw_ref, cos_ref, sin_ref, o_ref, h_ref, hf_ref):
    j = pl.program_id(1)
    n_dil_tiles = 3 * len(DIL_GROUPS)
    wide = DIL_HEADS * DIL_HEAD_DIM
    nc = D_MODEL // LANES

    @pl.when(j == 0)
    def _():
        h = _rms(x_ref[...], g_ref[...])
        h_ref[0] = h.astype(BF16)
        for c in range(nc):
            hf_ref[c] = h[:, c * LANES:(c + 1) * LANES]
        for g, (_, dil) in enumerate(DIL_GROUPS):
            if dil == 1:
                continue
            rows = PERM_TILE // dil
            for r in range(dil):
                for c in range(nc):
                    h_ref[g, r * rows:(r + 1) * rows, c * LANES:(c + 1) * LANES] = (
                        hf_ref[c, pl.ds(r, rows, stride=dil), :].astype(BF16))

    is_dil = j < n_dil_tiles
    group = jnp.where(is_dil, j // 3, 0)
    comp = j % 3
    is_qk = is_dil & (comp < 2)

    @pl.when(is_qk)
    def _():
        nw = nw_ref[0]
        scale = jnp.where(comp == 0, DIL_HEAD_DIM ** -0.5, 1.0).astype(F32)
        for c in range(PERM_TILE // QK_ROWS):
            rs = slice(c * QK_ROWS, (c + 1) * QK_ROWS)
            acc = jnp.dot(h_ref[group, rs, :], w_ref[...], preferred_element_type=F32)
            cos = cos_ref[0, rs, :]
            sin = sin_ref[0, rs, :]
            for h in range(wide // LANES):
                hs = slice(h * LANES, (h + 1) * LANES)
                o_ref[rs, hs] = (_rope(_rms(acc[:, hs], nw), cos, sin) * scale).astype(o_ref.dtype)

    @pl.when(jnp.logical_not(is_qk))
    def _():
        o_ref[...] = jnp.dot(h_ref[group], w_ref[...], preferred_element_type=F32).astype(o_ref.dtype)


def _in_proj(x, p, tables, seq):
    m, k = x.shape
    tn = DIL_HEADS * DIL_HEAD_DIM
    ng = len(DIL_GROUPS)
    per_seq = seq // PERM_TILE
    cos, sin = tables
    group_of = lambda j: jnp.where(j < 3 * ng, j // 3, 0)
    table_spec = pl.BlockSpec((1, PERM_TILE, LANES), lambda i, j: (group_of(j), i % per_seq, 0))
    return pl.pallas_call(
        _in_proj_kernel,
        grid=(m // PERM_TILE, IN_COLS_PAD // tn),
        in_specs=[pl.BlockSpec((PERM_TILE, k), lambda i, j: (i, 0)),
                  pl.BlockSpec((1, k), lambda i, j: (0, 0)),
                  pl.BlockSpec((k, tn), lambda i, j: (0, j)),
                  pl.BlockSpec((1, 1, LANES), lambda i, j: (jnp.minimum(j % 3, 1), 0, 0)),
                  table_spec, table_spec],
        out_specs=pl.BlockSpec((PERM_TILE, tn), lambda i, j: (i, j)),
        out_shape=jax.ShapeDtypeStruct((m, IN_COLS_PAD), BF16),
        scratch_shapes=[pltpu.VMEM((ng, PERM_TILE, k), BF16),
                        pltpu.VMEM((k // LANES, PERM_TILE, LANES), F32)],
        compiler_params=_cparams("parallel", "arbitrary"),
        name="in_proj",
    )(x, p["norm_mix"], p["w_in"], p["dil_qk_norm"], cos, sin)


TN_DIMS = (((0,), (0,)), ((), ()))


def _gla_kernel(*refs, tb, reverse, final):
    if final:
        q_ref, k_ref, v_ref, sm_ref, wg_ref, bg_ref, of_ref, r_ref, gn_ref, o_ref, st_ref = refs
    else:
        q_ref, k_ref, v_ref, sm_ref, wg_ref, bg_ref, o_ref, st_ref = refs

    @pl.when(pl.program_id(1) == 0)
    def _():
        st_ref[...] = jnp.zeros_like(st_ref)

    ck = GLA_CHUNK
    la = _log_sigmoid(jnp.dot(sm_ref[...], wg_ref[...], preferred_element_type=F32) + bg_ref[...]) * (1.0 / GLA_TAU)

    r = lax.broadcasted_iota(jnp.int32, (ck, ck), 0)
    c = lax.broadcasted_iota(jnp.int32, (ck, ck), 1)
    tri_mask = (r <= c) if reverse else (r >= c)
    tri = jnp.where(tri_mask, 1.0, 0.0).astype(BF16)
    end = 0 if reverse else ck - 1
    scale = GLA_DK ** -0.5

    n_chunks = tb // ck
    order = range(n_chunks - 1, -1, -1) if reverse else range(n_chunks)
    for ci in order:
        sl = slice(ci * ck, (ci + 1) * ck)
        la_c = la[sl]
        hi = la_c.astype(BF16)
        lo = (la_c - hi.astype(F32)).astype(BF16)
        b = jnp.dot(tri, hi, preferred_element_type=F32) + jnp.dot(tri, lo, preferred_element_type=F32)
        b_end = b[end:end + 1, :]
        kc = k_ref[sl, :].astype(F32)
        qd = (q_ref[sl, :].astype(F32) * scale * jnp.exp(b)).astype(BF16)
        kinv = (kc * jnp.exp(-b)).astype(BF16)
        ktail = (kc * jnp.exp(b_end - b)).astype(BF16)
        dec = jnp.exp(b_end)
        for h in range(GLA_HEADS):
            ks = slice(h * GLA_DK, (h + 1) * GLA_DK)
            vs = slice(h * GLA_DV, (h + 1) * GLA_DV)
            qh = qd[:, ks]
            att = lax.dot_general(qh, kinv[:, ks], NT_DIMS, preferred_element_type=F32)
            att = jnp.where(tri_mask, att, 0.0).astype(BF16)
            vh = v_ref[sl, vs]
            st = st_ref[h]
            o = (jnp.dot(att, vh, preferred_element_type=F32)
                 + lax.dot_general(qh, st.astype(BF16), NT_DIMS, preferred_element_type=F32))
            st_ref[h] = dec[:, ks] * st + lax.dot_general(vh, ktail[:, ks], TN_DIMS, preferred_element_type=F32)
            if final:
                o = _rms(o + of_ref[sl, vs], gn_ref[...])
                rr = r_ref[sl, vs].astype(F32)
                o = o * (rr * _sigmoid(rr))
            o_ref[sl, vs] = o.astype(o_ref.dtype)


def _gla_direction(proj, wg, bg, batch, seq, reverse, o_fwd=None, g_out=None, tb=256):
    nb = seq // tb
    m = batch * seq
    final = o_fwd is not None

    def row(b, i):
        return b * nb + (nb - 1 - i if reverse else i)

    def col(block_cols, off):
        return lambda b, i: (row(b, i), off // block_cols)

    hk = GLA_HEADS * GLA_DK
    hv = GLA_HEADS * GLA_DV
    const = lambda b, i: (0, 0)
    in_specs = [pl.BlockSpec((tb, hk), col(hk, COL_AQ)),
                pl.BlockSpec((tb, hk), col(hk, COL_AK)),
                pl.BlockSpec((tb, hv), col(hv, COL_AV)),
                pl.BlockSpec((tb, LANES), col(LANES, COL_SMALL)),
                pl.BlockSpec((LANES, hk), const),
                pl.BlockSpec((1, hk), const)]
    args = [proj, proj, proj, proj, wg, bg]
    if final:
        in_specs += [pl.BlockSpec((tb, hv), col(hv, 0)),
                     pl.BlockSpec((tb, hv), col(hv, COL_AR)),
                     pl.BlockSpec((1, GLA_DV), const)]
        args += [o_fwd, proj, g_out]
    return pl.pallas_call(
        functools.partial(_gla_kernel, tb=tb, reverse=reverse, final=final),
        grid=(batch, nb),
        in_specs=in_specs,
        out_specs=pl.BlockSpec((tb, hv), col(hv, 0)),
        out_shape=jax.ShapeDtypeStruct((m, hv), BF16 if final else F32),
        scratch_shapes=[pltpu.VMEM((GLA_HEADS, GLA_DV, GLA_DK), F32)],
        compiler_params=_cparams("parallel", "arbitrary"),
        name="gla_bwd" if reverse else "gla_fwd",
    )(*args)


def _rope(y, cos, sin_signed):
    return y * cos + pltpu.roll(y, LANES // 2, 1) * sin_signed


def _mla_prep_kernel(bq_ref, bkv_ref, sm_ref, qln_ref, wuq_ref, kvln_ref, wuk_ref, wuv_ref,
                     qn_ref, kn_ref, cos_ref, sin_ref, q_out, k_out, v_out):
    hq = _rms(bq_ref[...].astype(F32), qln_ref[...]).astype(BF16)
    hkv = _rms(bkv_ref[...].astype(F32), kvln_ref[...]).astype(BF16)
    q = jnp.dot(hq, wuq_ref[...], preferred_element_type=F32)
    kn = jnp.dot(hkv, wuk_ref[...], preferred_element_type=F32)
    v = jnp.dot(hkv, wuv_ref[...], preferred_element_type=F32)
    tm = q.shape[0]
    lane = lax.broadcasted_iota(jnp.int32, (tm, LANES), 1)
    sm = sm_ref[...].astype(F32)
    in_rope = (lane < ROPE_LO + ROPE_HALF) | ((lane >= ROPE_HI) & (lane < ROPE_HI + ROPE_HALF))
    kpe = jnp.where(in_rope, sm, 0.0)
    cos = cos_ref[...]
    sin = sin_ref[...]
    scale = MLA_QK ** -0.5 * LOG2_E
    ones = jnp.ones((tm, LANES), v_out.dtype)
    for h in range(MLA_HEADS):
        hs = slice(h * LANES, (h + 1) * LANES)
        qh = _rope(_rms(q[:, hs], qn_ref[...], MLA_QK), cos, sin)
        q_out[:, hs] = (qh * scale).astype(q_out.dtype)
        kh = _rope(_rms(kn[:, hs] + kpe, kn_ref[...], MLA_QK), cos, sin)
        k_out[:, hs] = kh.astype(k_out.dtype)
        v_out[:, 2 * h * LANES:(2 * h + 1) * LANES] = v[:, hs].astype(v_out.dtype)
        v_out[:, (2 * h + 1) * LANES:(2 * h + 2) * LANES] = ones


def _mla_prep(proj, p, cos, sin, seq, tm=512):
    m = proj.shape[0]
    ns = seq // tm
    const = lambda i: (0, 0)
    wide = MLA_HEADS * LANES
    out = jax.ShapeDtypeStruct((m, wide), BF16)
    return pl.pallas_call(
        _mla_prep_kernel,
        grid=(m // tm,),
        in_specs=[pl.BlockSpec((tm, MLA_Q_RANK), lambda i: (i, COL_BQ // MLA_Q_RANK)),
                  pl.BlockSpec((tm, MLA_KV_RANK), lambda i: (i, COL_BKV // MLA_KV_RANK)),
                  pl.BlockSpec((tm, LANES), lambda i: (i, COL_SMALL // LANES)),
                  pl.BlockSpec((1, MLA_Q_RANK), const),
                  pl.BlockSpec((MLA_Q_RANK, wide), const),
                  pl.BlockSpec((1, MLA_KV_RANK), const),
                  pl.BlockSpec((MLA_KV_RANK, wide), const),
                  pl.BlockSpec((MLA_KV_RANK, wide), const),
                  pl.BlockSpec((1, LANES), const),
                  pl.BlockSpec((1, LANES), const),
                  pl.BlockSpec((tm, LANES), lambda i: (i % ns, 0)),
                  pl.BlockSpec((tm, LANES), lambda i: (i % ns, 0))],
        out_specs=[pl.BlockSpec((tm, wide), lambda i: (i, 0)), pl.BlockSpec((tm, wide), lambda i: (i, 0)),
                   pl.BlockSpec((tm, 2 * wide), lambda i: (i, 0))],
        out_shape=[out, out, jax.ShapeDtypeStruct((m, 2 * wide), BF16)],
        compiler_params=_cparams("parallel"),
        name="mla_prep",
    )(proj, proj, proj, p["mla_q_lat_norm"], p["mla_w_uq"], p["mla_kv_lat_norm"], p["mla_w_uk"],
      p["mla_w_uv"], p["mla_q_norm"], p["mla_k_norm"], cos, sin)


def _mla_attn_kernel(q_ref, k_ref, v_ref, o_ref, *, tk, unroll):
    q = q_ref[...]
    tq = q.shape[0]
    nk = k_ref.shape[0] // tk

    def body(t, carry):
        m, acc = carry
        start = pl.multiple_of(t * tk, tk)
        s = lax.dot_general(q, k_ref[pl.ds(start, tk), :], NT_DIMS, preferred_element_type=F32)
        m_new = jnp.maximum(m, jnp.max(s, axis=-1, keepdims=True))
        p = jnp.exp2(s - m_new).astype(BF16)
        acc = jnp.exp2(m - m_new) * acc + jnp.dot(p, v_ref[pl.ds(start, tk), :], preferred_element_type=F32)
        return m_new, acc

    init = (jnp.full((tq, 1), -jnp.inf, F32), jnp.zeros((tq, 2 * LANES), F32))
    _, acc = lax.fori_loop(0, nk, body, init, unroll=unroll)
    o_ref[...] = (acc[:, :LANES] / acc[:, LANES:]).astype(o_ref.dtype)


def _mla_attn(q, k, v, batch, seq, tk=512, max_unroll=32):
    tq = 512 if seq // tk <= max_unroll // 4 else 256
    nq = seq // tq
    return pl.pallas_call(
        functools.partial(_mla_attn_kernel, tk=tk, unroll=min(max_unroll, seq // tk)),
        grid=(batch, MLA_HEADS, nq),
        in_specs=[pl.BlockSpec((tq, LANES), lambda b, h, i: (b * nq + i, h)),
                  pl.BlockSpec((seq, LANES), lambda b, h, i: (b, h)),
                  pl.BlockSpec((seq, 2 * LANES), lambda b, h, i: (b, h))],
        out_specs=pl.BlockSpec((tq, LANES), lambda b, h, i: (b * nq + i, h)),
        out_shape=jax.ShapeDtypeStruct(q.shape, BF16),
        compiler_params=_cparams("parallel", "parallel", "arbitrary"),
        name="mla_attn",
    )(q, k, v)


DIL_SUB_BLOCKS = 2
DIL_MAX_Q_BLOCKS = 4


def _dil_attn_kernel(*refs, sub_len, q_blocks):
    k_blocks = q_blocks + 2
    q_ref = refs[0]
    k_refs = refs[1:1 + k_blocks]
    v_refs = refs[1 + k_blocks:1 + 2 * k_blocks]
    o_ref, lse_ref = refs[1 + 2 * k_blocks:]
    piece_rows = q_ref.shape[1]
    tl = DIL_SUB_BLOCKS * HALO
    nkeys = (DIL_SUB_BLOCKS + 2) * HALO

    def row_runs(start, n):
        runs = []
        while n > 0:
            off = start % piece_rows
            take = min(n, piece_rows - off)
            runs.append((start // piece_rows, slice(off, off + take)))
            start, n = start + take, n - take
        return runs

    row = lax.broadcasted_iota(jnp.int32, (tl, nkeys), 0)
    col = lax.broadcasted_iota(jnp.int32, (tl, nkeys), 1)
    rel = col - row
    in_band = (rel >= 0) & (rel <= 2 * DIL_RADIUS)
    lane = lax.broadcasted_iota(jnp.int32, (tl, LANES), 1)
    for sub in range(q_blocks // DIL_SUB_BLOCKS):
        q_runs = row_runs(sub * tl, tl)
        pos = (pl.program_id(2) * q_blocks + sub * DIL_SUB_BLOCKS - 1) * HALO + col
        valid = in_band & (pos >= 0) & (pos < sub_len)
        kb = range(sub * DIL_SUB_BLOCKS, sub * DIL_SUB_BLOCKS + DIL_SUB_BLOCKS + 2)
        lse_all = jnp.zeros((tl, LANES), F32)
        for h in range(DIL_HEADS):
            hs = slice(h * LANES, (h + 1) * LANES)
            qq = jnp.concatenate([q_ref[p, rs, hs] for p, rs in q_runs], axis=0)
            kk = jnp.concatenate([k_refs[n][:, hs] for n in kb], axis=0)
            vv = jnp.concatenate([v_refs[n][:, hs] for n in kb], axis=0)
            s = lax.dot_general(qq, kk, NT_DIMS, preferred_element_type=F32)
            s = jnp.where(valid, s, -jnp.inf)
            m = jnp.max(s, axis=-1, keepdims=True)
            p = jnp.exp(s - m)
            l = jnp.sum(p, axis=-1, keepdims=True)
            o = jnp.dot(p.astype(BF16), vv, preferred_element_type=F32) / l
            done = 0
            for pc, rs in q_runs:
                o_ref[pc, rs, hs] = o[done:done + rs.stop - rs.start]
                done += rs.stop - rs.start
            lse_all = jnp.where(lane == h, m + jnp.log(l), lse_all)
        done = 0
        for pc, rs in q_runs:
            lse_ref[pc, rs, :] = lse_all[done:done + rs.stop - rs.start]
            done += rs.stop - rs.start


def _dil_attn(proj, group, batch, seq):
    _, dil = DIL_GROUPS[group]
    sub_len = seq // dil
    n_blocks = sub_len // HALO
    chunk_blocks = PERM_TILE // dil // HALO
    tile_blocks = PERM_TILE // HALO
    wide = DIL_HEADS * DIL_HEAD_DIM

    def row_block(b, r, n):
        n = jnp.clip(n, 0, n_blocks - 1)
        return (b * (seq // HALO) + (n // chunk_blocks) * tile_blocks + r * chunk_blocks + n % chunk_blocks)

    q_blocks = min(DIL_MAX_Q_BLOCKS, n_blocks)
    k_blocks = q_blocks + 2

    def halo_spec(comp, offset):
        return pl.BlockSpec((HALO, wide), lambda b, r, i: (row_block(b, r, q_blocks * i + offset),
                                                          _dil_tile(comp, group)))

    tl = q_blocks * HALO
    chunk = PERM_TILE // dil
    pieces = max(1, tl // chunk)
    piece_rows = tl // pieces
    subs = chunk // piece_rows
    m = batch * seq
    lead = m // (PERM_TILE * pieces)
    per_seq = seq // (PERM_TILE * pieces)

    def query_view(width):
        return (lead, pieces, dil, chunk, width)

    def query_spec(width, col_block):
        return pl.BlockSpec((None, pieces, None, piece_rows, width),
                            lambda b, r, i: (b * per_seq + i // subs, 0, r, i % subs, col_block))

    o, lse = pl.pallas_call(
        functools.partial(_dil_attn_kernel, sub_len=sub_len, q_blocks=q_blocks),
        grid=(batch, dil, n_blocks // q_blocks),
        in_specs=([query_spec(wide, _dil_tile(0, group))]
                  + [halo_spec(1, n - 1) for n in range(k_blocks)]
                  + [halo_spec(2, n - 1) for n in range(k_blocks)]),
        out_specs=[query_spec(wide, 0), query_spec(LANES, 0)],
        out_shape=[jax.ShapeDtypeStruct(query_view(wide), F32), jax.ShapeDtypeStruct(query_view(LANES), F32)],
        compiler_params=_cparams("parallel", "parallel", "arbitrary"),
        name=f"dil_attn_g{group}",
    )(proj.reshape(query_view(IN_COLS_PAD)), *([proj] * (2 * k_blocks)))
    return o.reshape(m, wide), lse.reshape(m, LANES)


def _combine_kernel(x_ref, a_ref, b_ref, o0_ref, o1_ref, o2_ref, l0_ref, l1_ref, l2_ref, gate_ref,
                    wa_ref, wb_ref, wc_ref, wo_ref, out_ref, c_ref, on_refs, ln_refs):
    tm = x_ref.shape[0]
    o_tok, l_tok = [], []
    for g, (o_ref, l_ref) in enumerate(((o0_ref, l0_ref), (o1_ref, l1_ref), (o2_ref, l2_ref))):
        dil = DIL_GROUPS[g][1]
        if dil == 1:
            o_tok.append(lambda h, o_ref=o_ref: o_ref[0, 0, :, h * LANES:(h + 1) * LANES])
            l_tok.append(l_ref[0, 0])
            continue
        for r in range(dil):
            rows = pl.ds(r, tm // dil, stride=dil)
            ln_refs[g - 1, rows, :] = l_ref[0, r]
            for h in range(DIL_HEADS):
                on_refs[g - 1, h, rows, :] = o_ref[0, r, :, h * LANES:(h + 1) * LANES]
        o_tok.append(lambda h, g=g: on_refs[g - 1, h])
        l_tok.append(ln_refs[g - 1])
    l0, l1, l2 = l_tok
    mx = jnp.maximum(jnp.maximum(l0, l1), l2)
    e0, e1, e2 = jnp.exp(l0 - mx), jnp.exp(l1 - mx), jnp.exp(l2 - mx)
    inv = 1.0 / (e0 + e1 + e2)
    w0, w1, w2 = e0 * inv, e1 * inv, e2 * inv
    for h in range(DIL_HEADS):
        hs = slice(h * LANES, (h + 1) * LANES)
        c = w0[:, h:h + 1] * o_tok[0](h) + w1[:, h:h + 1] * o_tok[1](h) + w2[:, h:h + 1] * o_tok[2](h)
        c_ref[:, hs] = c.astype(BF16)
    ya = jnp.dot(a_ref[...], wa_ref[...], preferred_element_type=F32)
    yb = jnp.dot(b_ref[...], wb_ref[...], preferred_element_type=F32)
    yc = jnp.dot(c_ref[...], wc_ref[...], preferred_element_type=F32)
    d = D_MODEL
    mix = (_sigmoid(gate_ref[:, 0:d].astype(F32)) * ya
           + _sigmoid(gate_ref[:, d:2 * d].astype(F32)) * yb
           + _sigmoid(gate_ref[:, 2 * d:3 * d].astype(F32)) * yc)
    out_ref[...] = x_ref[...] + jnp.dot(mix.astype(BF16), wo_ref[...], preferred_element_type=F32)


def _combine(x, gla_o, mla_o, dil_o, dil_lse, proj, p, batch, seq, tm=256):
    m = x.shape[0]
    d = D_MODEL
    ns = PERM_TILE // tm
    rowblk = lambda i: (i, 0)
    const = lambda i: (0, 0)
    wspec = pl.BlockSpec((d, d), const)

    def residue_major(arr, group, width):
        dil = DIL_GROUPS[group][1]
        view = arr.reshape(m // PERM_TILE, dil, PERM_TILE // dil, width)
        return view, pl.BlockSpec((1, dil, tm // dil, width), lambda i: (i // ns, 0, i % ns, 0))

    o_views, o_specs = zip(*[residue_major(dil_o[g], g, d) for g in range(len(DIL_GROUPS))])
    l_views, l_specs = zip(*[residue_major(dil_lse[g], g, LANES) for g in range(len(DIL_GROUPS))])
    n_perm = len(DIL_GROUPS) - 1
    return pl.pallas_call(
        _combine_kernel,
        grid=(m // tm,),
        in_specs=[pl.BlockSpec((tm, d), rowblk), pl.BlockSpec((tm, d), rowblk), pl.BlockSpec((tm, d), rowblk),
                  *o_specs, *l_specs,
                  pl.BlockSpec((tm, 3 * d), lambda i: (i, COL_GATE // (3 * d))),
                  wspec, wspec, wspec, wspec],
        out_specs=pl.BlockSpec((tm, d), rowblk),
        out_shape=jax.ShapeDtypeStruct((m, d), F32),
        scratch_shapes=[pltpu.VMEM((tm, d), BF16), pltpu.VMEM((n_perm, DIL_HEADS, tm, LANES), F32),
                        pltpu.VMEM((n_perm, tm, LANES), F32)],
        compiler_params=_cparams("parallel"),
        name="combine",
    )(x, gla_o, mla_o, *o_views, *l_views, proj,
      p["w_branch_a"], p["w_branch_b"], p["w_branch_c"], p["w_out"])


def _xattn_kernel(x_ref, g_ref, wq_ref, qn_ref, kv_ref, kn_ref, wo_ref, out_ref, o_ref):
    x = x_ref[...]
    h = _rms(x, g_ref[...]).astype(BF16)
    q = jnp.dot(h, wq_ref[...], preferred_element_type=F32)
    scale = X_HEAD_DIM ** -0.5
    for hd in range(X_HEADS):
        hs = slice(hd * X_HEAD_DIM, (hd + 1) * X_HEAD_DIM)
        vs = slice(D_MODEL + hd * X_HEAD_DIM, D_MODEL + (hd + 1) * X_HEAD_DIM)
        qh = (_rms(q[:, hs], qn_ref[...]) * scale).astype(BF16)
        kh = _rms(kv_ref[:, hs].astype(F32), kn_ref[...]).astype(BF16)
        s = lax.dot_general(qh, kh, NT_DIMS, preferred_element_type=F32)
        p = jnp.exp(s - jnp.max(s, axis=-1, keepdims=True))
        l = jnp.sum(p, axis=-1, keepdims=True)
        o = jnp.dot(p.astype(BF16), kv_ref[:, vs], preferred_element_type=F32) / l
        o_ref[:, hs] = o.astype(BF16)
    out_ref[...] = x + jnp.dot(o_ref[...], wo_ref[...], preferred_element_type=F32)


def _xattn(x, kv, p, seq, tm=512):
    m = x.shape[0]
    d = D_MODEL
    per_seq = seq // tm
    const = lambda i: (0, 0)
    return pl.pallas_call(
        _xattn_kernel,
        grid=(m // tm,),
        in_specs=[pl.BlockSpec((tm, d), lambda i: (i, 0)),
                  pl.BlockSpec((1, d), const),
                  pl.BlockSpec((d, d), const),
                  pl.BlockSpec((1, X_HEAD_DIM), const),
                  pl.BlockSpec((N_MEM, 2 * d), lambda i: (i // per_seq, 0)),
                  pl.BlockSpec((1, X_HEAD_DIM), const),
                  pl.BlockSpec((d, d), const)],
        out_specs=pl.BlockSpec((tm, d), lambda i: (i, 0)),
        out_shape=jax.ShapeDtypeStruct((m, d), F32),
        scratch_shapes=[pltpu.VMEM((tm, d), BF16)],
        compiler_params=_cparams("parallel"),
        name="xattn",
    )(x, p["norm_xattn"], p["w_xq"], p["xq_norm"], kv, p["xk_norm"], p["w_xo"])


def _ffn_kernel(x_ref, g_ref, w1_ref, w2_ref, out_ref, h_ref, acc_ref):
    j = pl.program_id(1)

    @pl.when(j == 0)
    def _():
        h_ref[...] = _rms(x_ref[...], g_ref[...]).astype(BF16)
        acc_ref[...] = x_ref[...]

    u = jnp.maximum(jnp.dot(h_ref[...], w1_ref[...], preferred_element_type=F32), 0.0)
    acc_ref[...] += jnp.dot((u * u).astype(BF16), w2_ref[...], preferred_element_type=F32)

    @pl.when(j == pl.num_programs(1) - 1)
    def _():
        out_ref[...] = acc_ref[...]


def _ffn(x, p, tm=1024, tf=1024):
    m = x.shape[0]
    d = D_MODEL
    return pl.pallas_call(
        _ffn_kernel,
        grid=(m // tm, D_FF // tf),
        in_specs=[pl.BlockSpec((tm, d), lambda i, j: (i, 0)),
                  pl.BlockSpec((1, d), lambda i, j: (0, 0)),
                  pl.BlockSpec((d, tf), lambda i, j: (0, j)),
                  pl.BlockSpec((tf, d), lambda i, j: (j, 0))],
        out_specs=pl.BlockSpec((tm, d), lambda i, j: (i, 0)),
        out_shape=jax.ShapeDtypeStruct((m, d), F32),
        scratch_shapes=[pltpu.VMEM((tm, d), BF16), pltpu.VMEM((tm, d), F32)],
        compiler_params=_cparams("parallel", "arbitrary"),
        name="ffn",
    )(x, p["norm_ffn"], p["w_ff1"], p["w_ff2"])


def _rope_tables(positions):
    seq = len(positions)
    inv_freq = 1.0 / (ROPE_THETA ** (jnp.arange(0, 2 * ROPE_HALF, 2, dtype=F32) / (2 * ROPE_HALF)))
    ang = jnp.asarray(positions, F32)[:, None] * inv_freq[None, :]
    cos, sin = jnp.cos(ang), jnp.sin(ang)
    gap = ROPE_HI - ROPE_LO - ROPE_HALF
    tail = LANES - ROPE_HI - ROPE_HALF
    fill = lambda n, v: jnp.full((seq, n), v, F32)
    cos_t = jnp.concatenate([fill(ROPE_LO, 1.0), cos, fill(gap, 1.0), cos, fill(tail, 1.0)], axis=1)
    sin_t = jnp.concatenate([fill(ROPE_LO, 0.0), -sin, fill(gap, 0.0), sin, fill(tail, 0.0)], axis=1)
    return cos_t, sin_t


def _lanes_after(*taken):
    used = set(int(i) for t in taken for i in t)
    return [i for i in range(LANES) if i not in used]


_ROPE_LANES = list(range(ROPE_LO, ROPE_LO + ROPE_HALF)) + list(range(ROPE_HI, ROPE_HI + ROPE_HALF))
_MLA_LANE_OF_DIM = np.array(_lanes_after(_ROPE_LANES)[:MLA_NOPE] + _ROPE_LANES, np.int32)
_DIL_LANE_OF_DIM = np.array(_ROPE_LANES + _lanes_after(_ROPE_LANES), np.int32)


def _to_lanes(arr, lane_of_dim):
    dim_of_lane = np.full(LANES, -1)
    dim_of_lane[lane_of_dim] = np.arange(len(lane_of_dim))
    pieces, lane = [], 0
    while lane < LANES:
        run = 1
        while (lane + run < LANES and
               (dim_of_lane[lane + run] == dim_of_lane[lane] + run if dim_of_lane[lane] >= 0
                else dim_of_lane[lane + run] < 0)):
            run += 1
        start = int(dim_of_lane[lane])
        pieces.append(arr[..., start:start + run] if start >= 0
                      else jnp.zeros(arr.shape[:-1] + (run,), arr.dtype))
        lane += run
    return jnp.concatenate(pieces, axis=-1)


def _prep_layer(w):
    d = D_MODEL
    hk = GLA_HEADS * GLA_DK
    hv = GLA_HEADS * GLA_DV
    dil_w = len(DIL_GROUPS) * DIL_HEADS * DIL_HEAD_DIM
    sizes = (hk, hk, hv, hv, GLA_GATE_RANK, GLA_GATE_RANK, MLA_Q_RANK, MLA_KV_RANK, MLA_ROPE,
             dil_w, dil_w, dil_w, 3 * d)
    offs = [0]
    for s in sizes:
        offs.append(offs[-1] + s)
    (a_q, a_k, a_v, a_r, a_gf, a_gb, b_q, b_kv, b_kpe, c_q, c_k, c_v, gates) = [
        w["w_in"][:, offs[i]:offs[i + 1]] for i in range(len(sizes))]
    z = lambda n: jnp.zeros((d, n), F32)
    small_lane_of_dim = np.concatenate([_MLA_LANE_OF_DIM[MLA_NOPE:], SMALL_GF + np.arange(GLA_GATE_RANK),
                                        SMALL_GB + np.arange(GLA_GATE_RANK)])
    small = _to_lanes(jnp.concatenate([b_kpe, a_gf, a_gb], axis=1), small_lane_of_dim)

    def dil_heads(cols):
        per_head = cols.reshape(d, len(DIL_GROUPS) * DIL_HEADS, DIL_HEAD_DIM)
        return _to_lanes(per_head, _DIL_LANE_OF_DIM).reshape(d, dil_w)

    gw = DIL_HEADS * DIL_HEAD_DIM
    dil_tiles = [t[:, g * gw:(g + 1) * gw] for g in range(len(DIL_GROUPS))
                 for t in (dil_heads(c_q), dil_heads(c_k), c_v)]
    w_in = jnp.concatenate(dil_tiles + [gates, a_q, a_k, a_v, a_r, b_q, b_kv, small,
                                        z(IN_COLS_PAD - COL_SMALL - LANES)], axis=1).astype(BF16)

    def gate_w(wg, lane0):
        rows = lambda n: jnp.zeros((n, hk), F32)
        return jnp.concatenate([rows(lane0), wg, rows(LANES - lane0 - GLA_GATE_RANK)], axis=0).astype(BF16)

    wgf = gate_w(w["gla_w_gate_f"], SMALL_GF)
    wgb = gate_w(w["gla_w_gate_b"], SMALL_GB)

    w_uq = _to_lanes(w["mla_w_uq"].reshape(MLA_Q_RANK, MLA_HEADS, MLA_QK), _MLA_LANE_OF_DIM)
    w_uq = w_uq.reshape(MLA_Q_RANK, MLA_HEADS * LANES)
    w_ukv = w["mla_w_ukv"].reshape(MLA_KV_RANK, MLA_HEADS, MLA_NOPE + MLA_V)
    w_uk = _to_lanes(w_ukv[:, :, :MLA_NOPE], _MLA_LANE_OF_DIM[:MLA_NOPE])
    w_uk = w_uk.reshape(MLA_KV_RANK, MLA_HEADS * LANES)
    w_uv = w_ukv[:, :, MLA_NOPE:].reshape(MLA_KV_RANK, MLA_HEADS * MLA_V)
    row = lambda v: v.reshape(1, -1).astype(F32)
    return {
        "norm_mix": row(w["norm_mix"]), "w_in": w_in,
        "gla_wgf": wgf, "gla_bgf": row(w["gla_b_gate_f"]),
        "gla_wgb": wgb, "gla_bgb": row(w["gla_b_gate_b"]),
        "gla_out_norm": row(w["gla_out_norm"]), "w_branch_a": w["w_branch_a"].astype(BF16),
        "mla_q_lat_norm": row(w["mla_q_lat_norm"]), "mla_w_uq": w_uq.astype(BF16),
        "mla_kv_lat_norm": row(w["mla_kv_lat_norm"]), "mla_w_uk": w_uk.astype(BF16),
        "mla_w_uv": w_uv.astype(BF16),
        "mla_q_norm": _to_lanes(w["mla_q_norm"], _MLA_LANE_OF_DIM).reshape(1, LANES),
        "mla_k_norm": _to_lanes(w["mla_k_norm"], _MLA_LANE_OF_DIM).reshape(1, LANES),
        "w_branch_b": w["w_branch_b"].astype(BF16),
        "dil_qk_norm": _to_lanes(jnp.stack([w["dil_q_norm"], w["dil_k_norm"]]),
                                 _DIL_LANE_OF_DIM).reshape(2, 1, LANES),
        "w_branch_c": w["w_branch_c"].astype(BF16), "w_out": w["w_out"].astype(BF16),
        "norm_xattn": row(w["norm_xattn"]), "norm_mem": row(w["norm_mem"]),
        "w_xq": w["w_xq"].astype(BF16), "w_xkv": w["w_xkv"].astype(BF16),
        "xq_norm": row(w["xq_norm"]), "xk_norm": row(w["xk_norm"]), "w_xo": w["w_xo"].astype(BF16),
        "norm_ffn": row(w["norm_ffn"]), "w_ff1": w["w_ff1"].astype(BF16), "w_ff2": w["w_ff2"].astype(BF16),
    }


def _layer(x, mem, p, tables, batch, seq):
    (cos_m, sin_m), dil_tables = tables
    proj = _in_proj(x, p, dil_tables, seq)
    o_fwd = _gla_direction(proj, p["gla_wgf"], p["gla_bgf"], batch, seq, False)
    gla_o = _gla_direction(proj, p["gla_wgb"], p["gla_bgb"], batch, seq, True,
                           o_fwd=o_fwd, g_out=p["gla_out_norm"])
    q_m, k_m, v_m = _mla_prep(proj, p, cos_m, sin_m, seq)
    mla_o = _mla_attn(q_m, k_m, v_m, batch, seq)
    dil = [_dil_attn(proj, g, batch, seq) for g in range(len(DIL_GROUPS))]
    x = _combine(x, gla_o, mla_o, [o for o, _ in dil], [l for _, l in dil], proj, p, batch, seq)
    kv = _norm_matmul(mem, p["norm_mem"], p["w_xkv"], BF16, tm=N_MEM, tn=1024)
    x = _xattn(x, kv, p, seq)
    return _ffn(x, p)


def _trunk(x, mem, layers):
    batch, seq, d = x.shape
    natural = np.arange(seq)
    row_pos = [natural.reshape(seq // PERM_TILE, PERM_TILE // dil, dil).transpose(0, 2, 1).reshape(seq)
               for _, dil in DIL_GROUPS]
    dil_cos, dil_sin = zip(*[_rope_tables(pos) for pos in row_pos])
    tables = (_rope_tables(natural), (jnp.stack(dil_cos), jnp.stack(dil_sin)))
    xf = x.reshape(batch * seq, d)
    memf = mem.reshape(batch * mem.shape[1], d)
    for p in layers:
        xf = _layer(xf, memf, p, tables, batch, seq)
    return xf.reshape(batch, seq, d)


def kernel(x_prompt, x_sample, mem_prompt, mem_sample, norm_mix, w_in, gla_w_gate_f, gla_b_gate_f, gla_w_gate_b, gla_b_gate_b, gla_out_norm, w_branch_a, mla_q_lat_norm, mla_w_uq, mla_kv_lat_norm, mla_w_ukv, mla_q_norm, mla_k_norm, w_branch_b, dil_q_norm, dil_k_norm, w_branch_c, w_out, norm_xattn, norm_mem, w_xq, w_xkv, xq_norm, xk_norm, w_xo, norm_ffn, w_ff1, w_ff2):
    stacked = dict(norm_mix=norm_mix, w_in=w_in, gla_w_gate_f=gla_w_gate_f, gla_b_gate_f=gla_b_gate_f,
                   gla_w_gate_b=gla_w_gate_b, gla_b_gate_b=gla_b_gate_b, gla_out_norm=gla_out_norm,
                   w_branch_a=w_branch_a, mla_q_lat_norm=mla_q_lat_norm, mla_w_uq=mla_w_uq,
                   mla_kv_lat_norm=mla_kv_lat_norm, mla_w_ukv=mla_w_ukv, mla_q_norm=mla_q_norm,
                   mla_k_norm=mla_k_norm, w_branch_b=w_branch_b, dil_q_norm=dil_q_norm, dil_k_norm=dil_k_norm,
                   w_branch_c=w_branch_c, w_out=w_out, norm_xattn=norm_xattn, norm_mem=norm_mem, w_xq=w_xq,
                   w_xkv=w_xkv, xq_norm=xq_norm, xk_norm=xk_norm, w_xo=w_xo, norm_ffn=norm_ffn,
                   w_ff1=w_ff1, w_ff2=w_ff2)
    layers = [_prep_layer({k: v[l] for k, v in stacked.items()}) for l in range(DEPTH)]
    return (_trunk(x_prompt, mem_prompt, layers), _trunk(x_sample, mem_sample, layers))
```

```python
import functools

import jax
import jax.numpy as jnp
import numpy as np
from jax import lax
from jax.experimental import pallas as pl
from jax.experimental.pallas import tpu as pltpu

F32 = jnp.float32
BF16 = jnp.bfloat16

D_MODEL = 1024
DEPTH = 2
N_MEM = 256
ROPE_THETA = 500000.0
NORM_EPS = 1e-6
GLA_HEADS = 4
GLA_DK = 128
GLA_DV = 256
GLA_GATE_RANK = 16
GLA_TAU = 16.0
GLA_CHUNK = 64
MLA_HEADS = 8
MLA_Q_RANK = 256
MLA_KV_RANK = 128
MLA_NOPE = 64
MLA_ROPE = 32
MLA_QK = 96
MLA_V = 128
DIL_GROUPS = ((128, 1), (512, 4), (2048, 16))
DIL_HEADS = 8
DIL_HEAD_DIM = 128
DIL_ROT = 32
DIL_RADIUS = 64
X_HEADS = 4
X_HEAD_DIM = 256
D_FF = 4096

LANES = 128
ROPE_HALF = 16
VMEM_LIMIT_BYTES = 56 * 1024 * 1024

COL_GATE = 9216
PERM_TILE = 1024
HALO = 64
QK_ROWS = 256
COL_AQ, COL_AK, COL_AV, COL_AR = 12288, 12800, 13312, 14336
COL_BQ, COL_BKV, COL_SMALL = 15360, 15616, 15744
IN_COLS_PAD = 16384
SMALL_GF, SMALL_GB = 16, 32
ROPE_LO, ROPE_HI = 0, 64

NT_DIMS = (((1,), (1,)), ((), ()))
LOG2_E = 1.4426950408889634


def _cparams(*sem):
    return pltpu.CompilerParams(dimension_semantics=sem, vmem_limit_bytes=VMEM_LIMIT_BYTES)


def _rms(x, g, n=None):
    ss = jnp.sum(x * x, axis=-1, keepdims=True) * (1.0 / (n or x.shape[-1]))
    return x * lax.rsqrt(ss + NORM_EPS) * g


def _log_sigmoid(z):
    return jnp.minimum(z, 0.0) - jnp.log(1.0 + jnp.exp(-jnp.abs(z)))


def _sigmoid(z):
    return 1.0 / (1.0 + jnp.exp(-z))


def _norm_matmul_kernel(x_ref, g_ref, w_ref, o_ref, h_ref):
    @pl.when(pl.program_id(1) == 0)
    def _():
        h_ref[...] = _rms(x_ref[...].astype(F32), g_ref[...]).astype(BF16)

    o_ref[...] = jnp.dot(h_ref[...], w_ref[...], preferred_element_type=F32).astype(o_ref.dtype)


def _norm_matmul(x, g, w, out_dtype, tm, tn):
    m, k = x.shape
    n = w.shape[1]
    return pl.pallas_call(
        _norm_matmul_kernel,
        grid=(m // tm, n // tn),
        in_specs=[pl.BlockSpec((tm, k), lambda i, j: (i, 0)),
                  pl.BlockSpec((1, k), lambda i, j: (0, 0)),
                  pl.BlockSpec((k, tn), lambda i, j: (0, j))],
        out_specs=pl.BlockSpec((tm, tn), lambda i, j: (i, j)),
        out_shape=jax.ShapeDtypeStruct((m, n), out_dtype),
        scratch_shapes=[pltpu.VMEM((tm, k), BF16)],
        compiler_params=_cparams("parallel", "arbitrary"),
        name="norm_matmul",
    )(x, g, w)


def _dil_tile(comp, group):
    return len(DIL_GROUPS) * group + comp


def _in_proj_kernel(x_ref, g_ref, w_ref, nw_ref, cos_ref, sin_ref, o_ref, h_ref, hf_ref):
    j = pl.program_id(1)
    n_dil_tiles = 3 * len(DIL_GROUPS)
    wide = DIL_HEADS * DIL_HEAD_DIM
    nc = D_MODEL // LANES

    @pl.when(j == 0)
    def _():
        h = _rms(x_ref[...], g_ref[...])
        h_ref[0] = h.astype(BF16)
        for c in range(nc):
            hf_ref[c] = h[:, c * LANES:(c + 1) * LANES]
        for g, (_, dil) in enumerate(DIL_GROUPS):
            if dil == 1:
                continue
            rows = PERM_TILE // dil
            for r in range(dil):
                for c in range(nc):
                    h_ref[g, r * rows:(r + 1) * rows, c * LANES:(c + 1) * LANES] = (
                        hf_ref[c, pl.ds(r, rows, stride=dil), :].astype(BF16))

    is_dil = j < n_dil_tiles
    group = jnp.where(is_dil, j // 3, 0)
    comp = j % 3
    is_qk = is_dil & (comp < 2)

    @pl.when(is_qk)
    def _():
        nw = nw_ref[0]
        scale = jnp.where(comp == 0, DIL_HEAD_DIM ** -0.5, 1.0).astype(F32)
        for c in range(PERM_TILE // QK_ROWS):
            rs = slice(c * QK_ROWS, (c + 1) * QK_ROWS)
            acc = jnp.dot(h_ref[group, rs, :], w_ref[...], preferred_element_type=F32)
            cos = cos_ref[0, rs, :]
            sin = sin_ref[0, rs, :]
            for h in range(wide // LANES):
                hs = slice(h * LANES, (h + 1) * LANES)
                o_ref[rs, hs] = (_rope(_rms(acc[:, hs], nw), cos, sin) * scale).astype(o_ref.dtype)

    @pl.when(jnp.logical_not(is_qk))
    def _():
        o_ref[...] = jnp.dot(h_ref[group], w_ref[...], preferred_element_type=F32).astype(o_ref.dtype)


def _in_proj(x, p, tables, seq):
    m, k = x.shape
    tn = DIL_HEADS * DIL_HEAD_DIM
    ng = len(DIL_GROUPS)
    per_seq = seq // PERM_TILE
    cos, sin = tables
    group_of = lambda j: jnp.where(j < 3 * ng, j // 3, 0)
    table_spec = pl.BlockSpec((1, PERM_TILE, LANES), lambda i, j: (group_of(j), i % per_seq, 0))
    return pl.pallas_call(
        _in_proj_kernel,
        grid=(m // PERM_TILE, IN_COLS_PAD // tn),
        in_specs=[pl.BlockSpec((PERM_TILE, k), lambda i, j: (i, 0)),
                  pl.BlockSpec((1, k), lambda i, j: (0, 0)),
                  pl.BlockSpec((k, tn), lambda i, j: (0, j)),
                  pl.BlockSpec((1, 1, LANES), lambda i, j: (jnp.minimum(j % 3, 1), 0, 0)),
                  table_spec, table_spec],
        out_specs=pl.BlockSpec((PERM_TILE, tn), lambda i, j: (i, j)),
        out_shape=jax.ShapeDtypeStruct((m, IN_COLS_PAD), BF16),
        scratch_shapes=[pltpu.VMEM((ng, PERM_TILE, k), BF16),
                        pltpu.VMEM((k // LANES, PERM_TILE, LANES), F32)],
        compiler_params=_cparams("parallel", "arbitrary"),
        name="in_proj",
    )(x, p["norm_mix"], p["w_in"], p["dil_qk_norm"], cos, sin)


TN_DIMS = (((0,), (0,)), ((), ()))


def _gla_kernel(*refs, tb, reverse, final):
    if final:
        q_ref, k_ref, v_ref, sm_ref, wg_ref, bg_ref, of_ref, r_ref, gn_ref, o_ref, st_ref = refs
    else:
        q_ref, k_ref, v_ref, sm_ref, wg_ref, bg_ref, o_ref, st_ref = refs

    @pl.when(pl.program_id(1) == 0)
    def _():
        st_ref[...] = jnp.zeros_like(st_ref)

    ck = GLA_CHUNK
    la = _log_sigmoid(jnp.dot(sm_ref[...], wg_ref[...], preferred_element_type=F32) + bg_ref[...]) * (1.0 / GLA_TAU)

    r = lax.broadcasted_iota(jnp.int32, (ck, ck), 0)
    c = lax.broadcasted_iota(jnp.int32, (ck, ck), 1)
    tri_mask = (r <= c) if reverse else (r >= c)
    tri = jnp.where(tri_mask, 1.0, 0.0).astype(BF16)
    end = 0 if reverse else ck - 1
    scale = GLA_DK ** -0.5

    n_chunks = tb // ck
    order = range(n_chunks - 1, -1, -1) if reverse else range(n_chunks)
    for ci in order:
        sl = slice(ci * ck, (ci + 1) * ck)
        la_c = la[sl]
        hi = la_c.astype(BF16)
        lo = (la_c - hi.astype(F32)).astype(BF16)
        b = jnp.dot(tri, hi, preferred_element_type=F32) + jnp.dot(tri, lo, preferred_element_type=F32)
        b_end = b[end:end + 1, :]
        kc = k_ref[sl, :].astype(F32)
        qd = (q_ref[sl, :].astype(F32) * scale * jnp.exp(b)).astype(BF16)
        kinv = (kc * jnp.exp(-b)).astype(BF16)
        ktail = (kc * jnp.exp(b_end - b)).astype(BF16)
        dec = jnp.exp(b_end)
        for h in range(GLA_HEADS):
            ks = slice(h * GLA_DK, (h + 1) * GLA_DK)
            vs = slice(h * GLA_DV, (h + 1) * GLA_DV)
            qh = qd[:, ks]
            att = lax.dot_general(qh, kinv[:, ks], NT_DIMS, preferred_element_type=F32)
            att = jnp.where(tri_mask, att, 0.0).astype(BF16)
            vh = v_ref[sl, vs]
            st = st_ref[h]
            o = (jnp.dot(att, vh, preferred_element_type=F32)
                 + lax.dot_general(qh, st.astype(BF16), NT_DIMS, preferred_element_type=F32))
            st_ref[h] = dec[:, ks] * st + lax.dot_general(vh, ktail[:, ks], TN_DIMS, preferred_element_type=F32)
            if final:
                o = _rms(o + of_ref[sl, vs], gn_ref[...])
                rr = r_ref[sl, vs].astype(F32)
                o = o * (rr * _sigmoid(rr))
            o_ref[sl, vs] = o.astype(o_ref.dtype)


def _gla_direction(proj, wg, bg, batch, seq, reverse, o_fwd=None, g_out=None, tb=256):
    nb = seq // tb
    m = batch * seq
    final = o_fwd is not None

    def row(b, i):
        return b * nb + (nb - 1 - i if reverse else i)

    def col(block_cols, off):
        return lambda b, i: (row(b, i), off // block_cols)

    hk = GLA_HEADS * GLA_DK
    hv = GLA_HEADS * GLA_DV
    const = lambda b, i: (0, 0)
    in_specs = [pl.BlockSpec((tb, hk), col(hk, COL_AQ)),
                pl.BlockSpec((tb, hk), col(hk, COL_AK)),
                pl.BlockSpec((tb, hv), col(hv, COL_AV)),
                pl.BlockSpec((tb, LANES), col(LANES, COL_SMALL)),
                pl.BlockSpec((LANES, hk), const),
                pl.BlockSpec((1, hk), const)]
    args = [proj, proj, proj, proj, wg, bg]
    if final:
        in_specs += [pl.BlockSpec((tb, hv), col(hv, 0)),
                     pl.BlockSpec((tb, hv), col(hv, COL_AR)),
                     pl.BlockSpec((1, GLA_DV), const)]
        args += [o_fwd, proj, g_out]
    return pl.pallas_call(
        functools.partial(_gla_kernel, tb=tb, reverse=reverse, final=final),
        grid=(batch, nb),
        in_specs=in_specs,
        out_specs=pl.BlockSpec((tb, hv), col(hv, 0)),
        out_shape=jax.ShapeDtypeStruct((m, hv), BF16 if final else F32),
        scratch_shapes=[pltpu.VMEM((GLA_HEADS, GLA_DV, GLA_DK), F32)],
        compiler_params=_cparams("parallel", "arbitrary"),
        name="gla_bwd" if reverse else "gla_fwd",
    )(*args)


def _rope(y, cos, sin_signed):
    return y * cos + pltpu.roll(y, LANES // 2, 1) * sin_signed


def _mla_prep_kernel(bq_ref, bkv_ref, sm_ref, qln_ref, wuq_ref, kvln_ref, wuk_ref, wuv_ref,
                     qn_ref, kn_ref, cos_ref, sin_ref, q_out, k_out, v_out):
    hq = _rms(bq_ref[...].astype(F32), qln_ref[...]).astype(BF16)
    hkv = _rms(bkv_ref[...].astype(F32), kvln_ref[...]).astype(BF16)
    q = jnp.dot(hq, wuq_ref[...], preferred_element_type=F32)
    kn = jnp.dot(hkv, wuk_ref[...], preferred_element_type=F32)
    v = jnp.dot(hkv, wuv_ref[...], preferred_element_type=F32)
    tm = q.shape[0]
    lane = lax.broadcasted_iota(jnp.int32, (tm, LANES), 1)
    sm = sm_ref[...].astype(F32)
    in_rope = (lane < ROPE_LO + ROPE_HALF) | ((lane >= ROPE_HI) & (lane < ROPE_HI + ROPE_HALF))
    kpe = jnp.where(in_rope, sm, 0.0)
    cos = cos_ref[...]
    sin = sin_ref[...]
    scale = MLA_QK ** -0.5 * LOG2_E
    ones = jnp.ones((tm, LANES), v_out.dtype)
    for h in range(MLA_HEADS):
        hs = slice(h * LANES, (h + 1) * LANES)
        qh = _rope(_rms(q[:, hs], qn_ref[...], MLA_QK), cos, sin)
        q_out[:, hs] = (qh * scale).astype(q_out.dtype)
        kh = _rope(_rms(kn[:, hs] + kpe, kn_ref[...], MLA_QK), cos, sin)
        k_out[:, hs] = kh.astype(k_out.dtype)
        v_out[:, 2 * h * LANES:(2 * h + 1) * LANES] = v[:, hs].astype(v_out.dtype)
        v_out[:, (2 * h + 1) * LANES:(2 * h + 2) * LANES] = ones


def _mla_prep(proj, p, cos, sin, seq, tm=512):
    m = proj.shape[0]
    ns = seq // tm
    const = lambda i: (0, 0)
    wide = MLA_HEADS * LANES
    out = jax.ShapeDtypeStruct((m, wide), BF16)
    return pl.pallas_call(
        _mla_prep_kernel,
        grid=(m // tm,),
        in_specs=[pl.BlockSpec((tm, MLA_Q_RANK), lambda i: (i, COL_BQ // MLA_Q_RANK)),
                  pl.BlockSpec((tm, MLA_KV_RANK), lambda i: (i, COL_BKV // MLA_KV_RANK)),
                  pl.BlockSpec((tm, LANES), lambda i: (i, COL_SMALL // LANES)),
                  pl.BlockSpec((1, MLA_Q_RANK), const),
                  pl.BlockSpec((MLA_Q_RANK, wide), const),
                  pl.BlockSpec((1, MLA_KV_RANK), const),
                  pl.BlockSpec((MLA_KV_RANK, wide), const),
                  pl.BlockSpec((MLA_KV_RANK, wide), const),
                  pl.BlockSpec((1, LANES), const),
                  pl.BlockSpec((1, LANES), const),
                  pl.BlockSpec((tm, LANES), lambda i: (i % ns, 0)),
                  pl.BlockSpec((tm, LANES), lambda i: (i % ns, 0))],
        out_specs=[pl.BlockSpec((tm, wide), lambda i: (i, 0)), pl.BlockSpec((tm, wide), lambda i: (i, 0)),
                   pl.BlockSpec((tm, 2 * wide), lambda i: (i, 0))],
        out_shape=[out, out, jax.ShapeDtypeStruct((m, 2 * wide), BF16)],
        compiler_params=_cparams("parallel"),
        name="mla_prep",
    )(proj, proj, proj, p["mla_q_lat_norm"], p["mla_w_uq"], p["mla_kv_lat_norm"], p["mla_w_uk"],
      p["mla_w_uv"], p["mla_q_norm"], p["mla_k_norm"], cos, sin)


def _mla_attn_kernel(q_ref, k_ref, v_ref, o_ref, *, tk, unroll, sub):
    nsub = q_ref.shape[0] // sub
    nk = k_ref.shape[0] // tk
    qs = [q_ref[i * sub:(i + 1) * sub, :] for i in range(nsub)]

    def body(t, carry):
        start = pl.multiple_of(t * tk, tk)
        out = []
        for q, (m, acc) in zip(qs, carry):
            s = lax.dot_general(q, k_ref[pl.ds(start, tk), :], NT_DIMS, preferred_element_type=F32)
            m_new = jnp.maximum(m, jnp.max(s, axis=-1, keepdims=True))
            p = jnp.exp2(s - m_new).astype(BF16)
            acc = jnp.exp2(m - m_new) * acc + jnp.dot(p, v_ref[pl.ds(start, tk), :], preferred_element_type=F32)
            out.append((m_new, acc))
        return tuple(out)

    init = tuple((jnp.full((sub, 1), -jnp.inf, F32), jnp.zeros((sub, 2 * LANES), F32)) for _ in range(nsub))
    final = lax.fori_loop(0, nk, body, init, unroll=unroll)
    for i, (_, acc) in enumerate(final):
        o_ref[i * sub:(i + 1) * sub, :] = (acc[:, :LANES] / acc[:, LANES:]).astype(o_ref.dtype)


MLA_TILES_IN_FLIGHT = 32
MLA_SUB_ROWS = (256, 512)


def _mla_attn(q, k, v, batch, seq, tk=512):
    unroll = min(MLA_TILES_IN_FLIGHT, seq // tk)
    tq = min(seq, MLA_SUB_ROWS[0] * (MLA_TILES_IN_FLIGHT // unroll))
    sub = min(tq, MLA_SUB_ROWS[unroll < MLA_TILES_IN_FLIGHT])
    nq = seq // tq
    return pl.pallas_call(
        functools.partial(_mla_attn_kernel, tk=tk, unroll=unroll, sub=sub),
        grid=(batch, MLA_HEADS, nq),
        in_specs=[pl.BlockSpec((tq, LANES), lambda b, h, i: (b * nq + i, h)),
                  pl.BlockSpec((seq, LANES), lambda b, h, i: (b, h)),
                  pl.BlockSpec((seq, 2 * LANES), lambda b, h, i: (b, h))],
        out_specs=pl.BlockSpec((tq, LANES), lambda b, h, i: (b * nq + i, h)),
        out_shape=jax.ShapeDtypeStruct(q.shape, BF16),
        compiler_params=_cparams("parallel", "parallel", "arbitrary"),
        name="mla_attn",
    )(q, k, v)


DIL_SUB_BLOCKS = 2
DIL_MAX_Q_BLOCKS = 4


def _dil_attn_kernel(*refs, sub_len, q_blocks):
    k_blocks = q_blocks + 2
    q_ref = refs[0]
    k_refs = refs[1:1 + k_blocks]
    v_refs = refs[1 + k_blocks:1 + 2 * k_blocks]
    o_ref, lse_ref = refs[1 + 2 * k_blocks:]
    piece_rows = q_ref.shape[1]
    tl = DIL_SUB_BLOCKS * HALO
    nkeys = (DIL_SUB_BLOCKS + 2) * HALO

    def row_runs(start, n):
        runs = []
        while n > 0:
            off = start % piece_rows
            take = min(n, piece_rows - off)
            runs.append((start // piece_rows, slice(off, off + take)))
            start, n = start + take, n - take
        return runs

    row = lax.broadcasted_iota(jnp.int32, (tl, nkeys), 0)
    col = lax.broadcasted_iota(jnp.int32, (tl, nkeys), 1)
    rel = col - row
    in_band = (rel >= 0) & (rel <= 2 * DIL_RADIUS)
    lane = lax.broadcasted_iota(jnp.int32, (tl, LANES), 1)
    for sub in range(q_blocks // DIL_SUB_BLOCKS):
        q_runs = row_runs(sub * tl, tl)
        pos = (pl.program_id(2) * q_blocks + sub * DIL_SUB_BLOCKS - 1) * HALO + col
        valid = in_band & (pos >= 0) & (pos < sub_len)
        kb = range(sub * DIL_SUB_BLOCKS, sub * DIL_SUB_BLOCKS + DIL_SUB_BLOCKS + 2)
        lse_all = jnp.zeros((tl, LANES), F32)
        for h in range(DIL_HEADS):
            hs = slice(h * LANES, (h + 1) * LANES)
            qq = jnp.concatenate([q_ref[p, rs, hs] for p, rs in q_runs], axis=0)
            kk = jnp.concatenate([k_refs[n][:, hs] for n in kb], axis=0)
            vv = jnp.concatenate([v_refs[n][:, hs] for n in kb], axis=0)
            s = lax.dot_general(qq, kk, NT_DIMS, preferred_element_type=F32)
            s = jnp.where(valid, s, -jnp.inf)
            m = jnp.max(s, axis=-1, keepdims=True)
            p = jnp.exp(s - m)
            l = jnp.sum(p, axis=-1, keepdims=True)
            o = jnp.dot(p.astype(BF16), vv, preferred_element_type=F32) / l
            done = 0
            for pc, rs in q_runs:
                o_ref[pc, rs, hs] = o[done:done + rs.stop - rs.start]
                done += rs.stop - rs.start
            lse_all = jnp.where(lane == h, m + jnp.log(l), lse_all)
        done = 0
        for pc, rs in q_runs:
            lse_ref[pc, rs, :] = lse_all[done:done + rs.stop - rs.start]
            done += rs.stop - rs.start


def _dil_attn(proj, group, batch, seq):
    _, dil = DIL_GROUPS[group]
    sub_len = seq // dil
    n_blocks = sub_len // HALO
    chunk_blocks = PERM_TILE // dil // HALO
    tile_blocks = PERM_TILE // HALO
    wide = DIL_HEADS * DIL_HEAD_DIM

    def row_block(b, r, n):
        n = jnp.clip(n, 0, n_blocks - 1)
        return (b * (seq // HALO) + (n // chunk_blocks) * tile_blocks + r * chunk_blocks + n % chunk_blocks)

    q_blocks = min(DIL_MAX_Q_BLOCKS, n_blocks)
    k_blocks = q_blocks + 2

    def halo_spec(comp, offset):
        return pl.BlockSpec((HALO, wide), lambda b, r, i: (row_block(b, r, q_blocks * i + offset),
                                                          _dil_tile(comp, group)))

    tl = q_blocks * HALO
    chunk = PERM_TILE // dil
    pieces = max(1, tl // chunk)
    piece_rows = tl // pieces
    subs = chunk // piece_rows
    m = batch * seq
    lead = m // (PERM_TILE * pieces)
    per_seq = seq // (PERM_TILE * pieces)

    def query_view(width):
        return (lead, pieces, dil, chunk, width)

    def query_spec(width, col_block):
        return pl.BlockSpec((None, pieces, None, piece_rows, width),
                            lambda b, r, i: (b * per_seq + i // subs, 0, r, i % subs, col_block))

    o, lse = pl.pallas_call(
        functools.partial(_dil_attn_kernel, sub_len=sub_len, q_blocks=q_blocks),
        grid=(batch, dil, n_blocks // q_blocks),
        in_specs=([query_spec(wide, _dil_tile(0, group))]
                  + [halo_spec(1, n - 1) for n in range(k_blocks)]
                  + [halo_spec(2, n - 1) for n in range(k_blocks)]),
        out_specs=[query_spec(wide, 0), query_spec(LANES, 0)],
        out_shape=[jax.ShapeDtypeStruct(query_view(wide), F32), jax.ShapeDtypeStruct(query_view(LANES), F32)],
        compiler_params=_cparams("parallel", "parallel", "arbitrary"),
        name=f"dil_attn_g{group}",
    )(proj.reshape(query_view(IN_COLS_PAD)), *([proj] * (2 * k_blocks)))
    return o.reshape(m, wide), lse.reshape(m, LANES)


def _combine_kernel(x_ref, a_ref, b_ref, o0_ref, o1_ref, o2_ref, l0_ref, l1_ref, l2_ref, gate_ref,
                    wa_ref, wb_ref, wc_ref, wo_ref, out_ref, c_ref, on_refs, ln_refs):
    tm = x_ref.shape[0]
    o_tok, l_tok = [], []
    for g, (o_ref, l_ref) in enumerate(((o0_ref, l0_ref), (o1_ref, l1_ref), (o2_ref, l2_ref))):
        dil = DIL_GROUPS[g][1]
        if dil == 1:
            o_tok.append(lambda h, o_ref=o_ref: o_ref[0, 0, :, h * LANES:(h + 1) * LANES])
            l_tok.append(l_ref[0, 0])
            continue
        for r in range(dil):
            rows = pl.ds(r, tm // dil, stride=dil)
            ln_refs[g - 1, rows, :] = l_ref[0, r]
            for h in range(DIL_HEADS):
                on_refs[g - 1, h, rows, :] = o_ref[0, r, :, h * LANES:(h + 1) * LANES]
        o_tok.append(lambda h, g=g: on_refs[g - 1, h])
        l_tok.append(ln_refs[g - 1])
    l0, l1, l2 = l_tok
    mx = jnp.maximum(jnp.maximum(l0, l1), l2)
    e0, e1, e2 = jnp.exp(l0 - mx), jnp.exp(l1 - mx), jnp.exp(l2 - mx)
    inv = 1.0 / (e0 + e1 + e2)
    w0, w1, w2 = e0 * inv, e1 * inv, e2 * inv
    for h in range(DIL_HEADS):
        hs = slice(h * LANES, (h + 1) * LANES)
        c = w0[:, h:h + 1] * o_tok[0](h) + w1[:, h:h + 1] * o_tok[1](h) + w2[:, h:h + 1] * o_tok[2](h)
        c_ref[:, hs] = c.astype(BF16)
    ya = jnp.dot(a_ref[...], wa_ref[...], preferred_element_type=F32)
    yb = jnp.dot(b_ref[...], wb_ref[...], preferred_element_type=F32)
    yc = jnp.dot(c_ref[...], wc_ref[...], preferred_element_type=F32)
    d = D_MODEL
    mix = (_sigmoid(gate_ref[:, 0:d].astype(F32)) * ya
           + _sigmoid(gate_ref[:, d:2 * d].astype(F32)) * yb
           + _sigmoid(gate_ref[:, 2 * d:3 * d].astype(F32)) * yc)
    out_ref[...] = x_ref[...] + jnp.dot(mix.astype(BF16), wo_ref[...], preferred_element_type=F32)


def _combine(x, gla_o, mla_o, dil_o, dil_lse, proj, p, batch, seq, tm=256):
    m = x.shape[0]
    d = D_MODEL
    ns = PERM_TILE // tm
    rowblk = lambda i: (i, 0)
    const = lambda i: (0, 0)
    wspec = pl.BlockSpec((d, d), const)

    def residue_major(arr, group, width):
        dil = DIL_GROUPS[group][1]
        view = arr.reshape(m // PERM_TILE, dil, PERM_TILE // dil, width)
        return view, pl.BlockSpec((1, dil, tm // dil, width), lambda i: (i // ns, 0, i % ns, 0))

    o_views, o_specs = zip(*[residue_major(dil_o[g], g, d) for g in range(len(DIL_GROUPS))])
    l_views, l_specs = zip(*[residue_major(dil_lse[g], g, LANES) for g in range(len(DIL_GROUPS))])
    n_perm = len(DIL_GROUPS) - 1
    return pl.pallas_call(
        _combine_kernel,
        grid=(m // tm,),
        in_specs=[pl.BlockSpec((tm, d), rowblk), pl.BlockSpec((tm, d), rowblk), pl.BlockSpec((tm, d), rowblk),
                  *o_specs, *l_specs,
                  pl.BlockSpec((tm, 3 * d), lambda i: (i, COL_GATE // (3 * d))),
                  wspec, wspec, wspec, wspec],
        out_specs=pl.BlockSpec((tm, d), rowblk),
        out_shape=jax.ShapeDtypeStruct((m, d), F32),
        scratch_shapes=[pltpu.VMEM((tm, d), BF16), pltpu.VMEM((n_perm, DIL_HEADS, tm, LANES), F32),
                        pltpu.VMEM((n_perm, tm, LANES), F32)],
        compiler_params=_cparams("parallel"),
        name="combine",
    )(x, gla_o, mla_o, *o_views, *l_views, proj,
      p["w_branch_a"], p["w_branch_b"], p["w_branch_c"], p["w_out"])


def _xattn_kernel(x_ref, g_ref, wq_ref, qn_ref, kv_ref, kn_ref, wo_ref, out_ref, o_ref):
    x = x_ref[...]
    h = _rms(x, g_ref[...]).astype(BF16)
    q = jnp.dot(h, wq_ref[...], preferred_element_type=F32)
    scale = X_HEAD_DIM ** -0.5
    for hd in range(X_HEADS):
        hs = slice(hd * X_HEAD_DIM, (hd + 1) * X_HEAD_DIM)
        vs = slice(D_MODEL + hd * X_HEAD_DIM, D_MODEL + (hd + 1) * X_HEAD_DIM)
        qh = (_rms(q[:, hs], qn_ref[...]) * scale).astype(BF16)
        kh = _rms(kv_ref[:, hs].astype(F32), kn_ref[...]).astype(BF16)
        s = lax.dot_general(qh, kh, NT_DIMS, preferred_element_type=F32)
        p = jnp.exp(s - jnp.max(s, axis=-1, keepdims=True))
        l = jnp.sum(p, axis=-1, keepdims=True)
        o = jnp.dot(p.astype(BF16), kv_ref[:, vs], preferred_element_type=F32) / l
        o_ref[:, hs] = o.astype(BF16)
    out_ref[...] = x + jnp.dot(o_ref[...], wo_ref[...], preferred_element_type=F32)


def _xattn(x, kv, p, seq, tm=512):
    m = x.shape[0]
    d = D_MODEL
    per_seq = seq // tm
    const = lambda i: (0, 0)
    return pl.pallas_call(
        _xattn_kernel,
        grid=(m // tm,),
        in_specs=[pl.BlockSpec((tm, d), lambda i: (i, 0)),
                  pl.BlockSpec((1, d), const),
                  pl.BlockSpec((d, d), const),
                  pl.BlockSpec((1, X_HEAD_DIM), const),
                  pl.BlockSpec((N_MEM, 2 * d), lambda i: (i // per_seq, 0)),
                  pl.BlockSpec((1, X_HEAD_DIM), const),
                  pl.BlockSpec((d, d), const)],
        out_specs=pl.BlockSpec((tm, d), lambda i: (i, 0)),
        out_shape=jax.ShapeDtypeStruct((m, d), F32),
        scratch_shapes=[pltpu.VMEM((tm, d), BF16)],
        compiler_params=_cparams("parallel"),
        name="xattn",
    )(x, p["norm_xattn"], p["w_xq"], p["xq_norm"], kv, p["xk_norm"], p["w_xo"])


def _ffn_kernel(x_ref, g_ref, w1_ref, w2_ref, out_ref, h_ref, acc_ref):
    j = pl.program_id(1)

    @pl.when(j == 0)
    def _():
        h_ref[...] = _rms(x_ref[...], g_ref[...]).astype(BF16)
        acc_ref[...] = x_ref[...]

    u = jnp.maximum(jnp.dot(h_ref[...], w1_ref[...], preferred_element_type=F32), 0.0)
    acc_ref[...] += jnp.dot((u * u).astype(BF16), w2_ref[...], preferred_element_type=F32)

    @pl.when(j == pl.num_programs(1) - 1)
    def _():
        out_ref[...] = acc_ref[...]


def _ffn(x, p, tm=1024, tf=1024):
    m = x.shape[0]
    d = D_MODEL
    return pl.pallas_call(
        _ffn_kernel,
        grid=(m // tm, D_FF // tf),
        in_specs=[pl.BlockSpec((tm, d), lambda i, j: (i, 0)),
                  pl.BlockSpec((1, d), lambda i, j: (0, 0)),
                  pl.BlockSpec((d, tf), lambda i, j: (0, j)),
                  pl.BlockSpec((tf, d), lambda i, j: (j, 0))],
        out_specs=pl.BlockSpec((tm, d), lambda i, j: (i, 0)),
        out_shape=jax.ShapeDtypeStruct((m, d), F32),
        scratch_shapes=[pltpu.VMEM((tm, d), BF16), pltpu.VMEM((tm, d), F32)],
        compiler_params=_cparams("parallel", "arbitrary"),
        name="ffn",
    )(x, p["norm_ffn"], p["w_ff1"], p["w_ff2"])


def _rope_tables(positions):
    seq = len(positions)
    inv_freq = 1.0 / (ROPE_THETA ** (np.arange(0, 2 * ROPE_HALF, 2, dtype=np.float64) / (2 * ROPE_HALF)))
    ang = np.asarray(positions, np.float64)[:, None] * inv_freq[None, :]
    cos, sin = jnp.asarray(np.cos(ang), F32), jnp.asarray(np.sin(ang), F32)
    gap = ROPE_HI - ROPE_LO - ROPE_HALF
    tail = LANES - ROPE_HI - ROPE_HALF
    fill = lambda n, v: jnp.full((seq, n), v, F32)
    cos_t = jnp.concatenate([fill(ROPE_LO, 1.0), cos, fill(gap, 1.0), cos, fill(tail, 1.0)], axis=1)
    sin_t = jnp.concatenate([fill(ROPE_LO, 0.0), -sin, fill(gap, 0.0), sin, fill(tail, 0.0)], axis=1)
    return cos_t, sin_t


def _lanes_after(*taken):
    used = set(int(i) for t in taken for i in t)
    return [i for i in range(LANES) if i not in used]


_ROPE_LANES = list(range(ROPE_LO, ROPE_LO + ROPE_HALF)) + list(range(ROPE_HI, ROPE_HI + ROPE_HALF))
_MLA_LANE_OF_DIM = np.array(_lanes_after(_ROPE_LANES)[:MLA_NOPE] + _ROPE_LANES, np.int32)
_DIL_LANE_OF_DIM = np.array(_ROPE_LANES + _lanes_after(_ROPE_LANES), np.int32)


def _to_lanes(arr, lane_of_dim):
    dim_of_lane = np.full(LANES, -1)
    dim_of_lane[lane_of_dim] = np.arange(len(lane_of_dim))
    pieces, lane = [], 0
    while lane < LANES:
        run = 1
        while (lane + run < LANES and
               (dim_of_lane[lane + run] == dim_of_lane[lane] + run if dim_of_lane[lane] >= 0
                else dim_of_lane[lane + run] < 0)):
            run += 1
        start = int(dim_of_lane[lane])
        pieces.append(arr[..., start:start + run] if start >= 0
                      else jnp.zeros(arr.shape[:-1] + (run,), arr.dtype))
        lane += run
    return jnp.concatenate(pieces, axis=-1)


def _prep_layer(w):
    d = D_MODEL
    hk = GLA_HEADS * GLA_DK
    hv = GLA_HEADS * GLA_DV
    dil_w = len(DIL_GROUPS) * DIL_HEADS * DIL_HEAD_DIM
    sizes = (hk, hk, hv, hv, GLA_GATE_RANK, GLA_GATE_RANK, MLA_Q_RANK, MLA_KV_RANK, MLA_ROPE,
             dil_w, dil_w, dil_w, 3 * d)
    offs = [0]
    for s in sizes:
        offs.append(offs[-1] + s)
    (a_q, a_k, a_v, a_r, a_gf, a_gb, b_q, b_kv, b_kpe, c_q, c_k, c_v, gates) = [
        w["w_in"][:, offs[i]:offs[i + 1]] for i in range(len(sizes))]
    z = lambda n: jnp.zeros((d, n), F32)
    small_lane_of_dim = np.concatenate([_MLA_LANE_OF_DIM[MLA_NOPE:], SMALL_GF + np.arange(GLA_GATE_RANK),
                                        SMALL_GB + np.arange(GLA_GATE_RANK)])
    small = _to_lanes(jnp.concatenate([b_kpe, a_gf, a_gb], axis=1), small_lane_of_dim)

    def dil_heads(cols):
        per_head = cols.reshape(d, len(DIL_GROUPS) * DIL_HEADS, DIL_HEAD_DIM)
        return _to_lanes(per_head, _DIL_LANE_OF_DIM).reshape(d, dil_w)

    gw = DIL_HEADS * DIL_HEAD_DIM
    dil_tiles = [t[:, g * gw:(g + 1) * gw] for g in range(len(DIL_GROUPS))
                 for t in (dil_heads(c_q), dil_heads(c_k), c_v)]
    w_in = jnp.concatenate(dil_tiles + [gates, a_q, a_k, a_v, a_r, b_q, b_kv, small,
                                        z(IN_COLS_PAD - COL_SMALL - LANES)], axis=1).astype(BF16)

    def gate_w(wg, lane0):
        rows = lambda n: jnp.zeros((n, hk), F32)
        return jnp.concatenate([rows(lane0), wg, rows(LANES - lane0 - GLA_GATE_RANK)], axis=0).astype(BF16)

    wgf = gate_w(w["gla_w_gate_f"], SMALL_GF)
    wgb = gate_w(w["gla_w_gate_b"], SMALL_GB)

    w_uq = _to_lanes(w["mla_w_uq"].reshape(MLA_Q_RANK, MLA_HEADS, MLA_QK), _MLA_LANE_OF_DIM)
    w_uq = w_uq.reshape(MLA_Q_RANK, MLA_HEADS * LANES)
    w_ukv = w["mla_w_ukv"].reshape(MLA_KV_RANK, MLA_HEADS, MLA_NOPE + MLA_V)
    w_uk = _to_lanes(w_ukv[:, :, :MLA_NOPE], _MLA_LANE_OF_DIM[:MLA_NOPE])
    w_uk = w_uk.reshape(MLA_KV_RANK, MLA_HEADS * LANES)
    w_uv = w_ukv[:, :, MLA_NOPE:].reshape(MLA_KV_RANK, MLA_HEADS * MLA_V)
    row = lambda v: v.reshape(1, -1).astype(F32)
    return {
        "norm_mix": row(w["norm_mix"]), "w_in": w_in,
        "gla_wgf": wgf, "gla_bgf": row(w["gla_b_gate_f"]),
        "gla_wgb": wgb, "gla_bgb": row(w["gla_b_gate_b"]),
        "gla_out_norm": row(w["gla_out_norm"]), "w_branch_a": w["w_branch_a"].astype(BF16),
        "mla_q_lat_norm": row(w["mla_q_lat_norm"]), "mla_w_uq": w_uq.astype(BF16),
        "mla_kv_lat_norm": row(w["mla_kv_lat_norm"]), "mla_w_uk": w_uk.astype(BF16),
        "mla_w_uv": w_uv.astype(BF16),
        "mla_q_norm": _to_lanes(w["mla_q_norm"], _MLA_LANE_OF_DIM).reshape(1, LANES),
        "mla_k_norm": _to_lanes(w["mla_k_norm"], _MLA_LANE_OF_DIM).reshape(1, LANES),
        "w_branch_b": w["w_branch_b"].astype(BF16),
        "dil_qk_norm": _to_lanes(jnp.stack([w["dil_q_norm"], w["dil_k_norm"]]),
                                 _DIL_LANE_OF_DIM).reshape(2, 1, LANES),
        "w_branch_c": w["w_branch_c"].astype(BF16), "w_out": w["w_out"].astype(BF16),
        "norm_xattn": row(w["norm_xattn"]), "norm_mem": row(w["norm_mem"]),
        "w_xq": w["w_xq"].astype(BF16), "w_xkv": w["w_xkv"].astype(BF16),
        "xq_norm": row(w["xq_norm"]), "xk_norm": row(w["xk_norm"]), "w_xo": w["w_xo"].astype(BF16),
        "norm_ffn": row(w["norm_ffn"]), "w_ff1": w["w_ff1"].astype(BF16), "w_ff2": w["w_ff2"].astype(BF16),
    }


def _layer(x, mem, p, tables, batch, seq):
    (cos_m, sin_m), dil_tables = tables
    proj = _in_proj(x, p, dil_tables, seq)
    o_fwd = _gla_direction(proj, p["gla_wgf"], p["gla_bgf"], batch, seq, False)
    gla_o = _gla_direction(proj, p["gla_wgb"], p["gla_bgb"], batch, seq, True,
                           o_fwd=o_fwd, g_out=p["gla_out_norm"])
    q_m, k_m, v_m = _mla_prep(proj, p, cos_m, sin_m, seq)
    mla_o = _mla_attn(q_m, k_m, v_m, batch, seq)
    dil = [_dil_attn(proj, g, batch, seq) for g in range(len(DIL_GROUPS))]
    x = _combine(x, gla_o, mla_o, [o for o, _ in dil], [l for _, l in dil], proj, p, batch, seq)
    kv = _norm_matmul(mem, p["norm_mem"], p["w_xkv"], BF16, tm=N_MEM, tn=1024)
    x = _xattn(x, kv, p, seq)
    return _ffn(x, p)


def _trunk(x, mem, layers):
    batch, seq, d = x.shape
    natural = np.arange(seq)
    row_pos = [natural.reshape(seq // PERM_TILE, PERM_TILE // dil, dil).transpose(0, 2, 1).reshape(seq)
               for _, dil in DIL_GROUPS]
    dil_cos, dil_sin = zip(*[_rope_tables(pos) for pos in row_pos])
    tables = (_rope_tables(natural), (jnp.stack(dil_cos), jnp.stack(dil_sin)))
    xf = x.reshape(batch * seq, d)
    memf = mem.reshape(batch * mem.shape[1], d)
    for p in layers:
        xf = _layer(xf, memf, p, tables, batch, seq)
    return xf.reshape(batch, seq, d)


def kernel(x_prompt, x_sample, mem_prompt, mem_sample, norm_mix, w_in, gla_w_gate_f, gla_b_gate_f, gla_w_gate_b, gla_b_gate_b, gla_out_norm, w_branch_a, mla_q_lat_norm, mla_w_uq, mla_kv_lat_norm, mla_w_ukv, mla_q_norm, mla_k_norm, w_branch_b, dil_q_norm, dil_k_norm, w_branch_c, w_out, norm_xattn, norm_mem, w_xq, w_xkv, xq_norm, xk_norm, w_xo, norm_ffn, w_ff1, w_ff2):
    stacked = dict(norm_mix=norm_mix, w_in=w_in, gla_w_gate_f=gla_w_gate_f, gla_b_gate_f=gla_b_gate_f,
                   gla_w_gate_b=gla_w_gate_b, gla_b_gate_b=gla_b_gate_b, gla_out_norm=gla_out_norm,
                   w_branch_a=w_branch_a, mla_q_lat_norm=mla_q_lat_norm, mla_w_uq=mla_w_uq,
                   mla_kv_lat_norm=mla_kv_lat_norm, mla_w_ukv=mla_w_ukv, mla_q_norm=mla_q_norm,
                   mla_k_norm=mla_k_norm, w_branch_b=w_branch_b, dil_q_norm=dil_q_norm, dil_k_norm=dil_k_norm,
                   w_branch_c=w_branch_c, w_out=w_out, norm_xattn=norm_xattn, norm_mem=norm_mem, w_xq=w_xq,
                   w_xkv=w_xkv, xq_norm=xq_norm, xk_norm=xk_norm, w_xo=w_xo, norm_ffn=norm_ffn,
                   w_ff1=w_ff1, w_ff2=w_ff2)
    layers = [_prep_layer({k: v[l] for k, v in stacked.items()}) for l in range(DEPTH)]
    return (_trunk(x_prompt, mem_prompt, layers), _trunk(x_sample, mem_sample, layers))
```

```python
import functools

import jax
import jax.numpy as jnp
import numpy as np
from jax import lax
from jax.experimental import pallas as pl
from jax.experimental.pallas import tpu as pltpu

F32 = jnp.float32
BF16 = jnp.bfloat16

D_MODEL = 1024
DEPTH = 2
N_MEM = 256
ROPE_THETA = 500000.0
NORM_EPS = 1e-6
GLA_HEADS = 4
GLA_DK = 128
GLA_DV = 256
GLA_GATE_RANK = 16
GLA_TAU = 16.0
GLA_CHUNK = 64
MLA_HEADS = 8
MLA_Q_RANK = 256
MLA_KV_RANK = 128
MLA_NOPE = 64
MLA_ROPE = 32
MLA_QK = 96
MLA_V = 128
DIL_GROUPS = ((128, 1), (512, 4), (2048, 16))
DIL_HEADS = 8
DIL_HEAD_DIM = 128
DIL_ROT = 32
DIL_RADIUS = 64
X_HEADS = 4
X_HEAD_DIM = 256
D_FF = 4096

LANES = 128
ROPE_HALF = 16
VMEM_LIMIT_BYTES = 56 * 1024 * 1024

COL_GATE = 9216
PERM_TILE = 1024
HALO = 64
QK_ROWS = 256
COL_AQ, COL_AK, COL_AV, COL_AR = 12288, 12800, 13312, 14336
COL_BQ, COL_BKV, COL_SMALL = 15360, 15616, 15744
IN_COLS_PAD = 16384
SMALL_GF, SMALL_GB = 16, 32
ROPE_LO, ROPE_HI = 0, 64

NT_DIMS = (((1,), (1,)), ((), ()))
LOG2_E = 1.4426950408889634


def _cparams(*sem):
    return pltpu.CompilerParams(dimension_semantics=sem, vmem_limit_bytes=VMEM_LIMIT_BYTES)


def _rms(x, g, n=None):
    ss = jnp.sum(x * x, axis=-1, keepdims=True) * (1.0 / (n or x.shape[-1]))
    return x * lax.rsqrt(ss + NORM_EPS) * g


def _log_sigmoid(z):
    return jnp.minimum(z, 0.0) - jnp.log(1.0 + jnp.exp(-jnp.abs(z)))


def _sigmoid(z):
    return 1.0 / (1.0 + jnp.exp(-z))


def _norm_matmul_kernel(x_ref, g_ref, w_ref, o_ref, h_ref):
    @pl.when(pl.program_id(1) == 0)
    def _():
        h_ref[...] = _rms(x_ref[...].astype(F32), g_ref[...]).astype(BF16)

    o_ref[...] = jnp.dot(h_ref[...], w_ref[...], preferred_element_type=F32).astype(o_ref.dtype)


def _norm_matmul(x, g, w, out_dtype, tm, tn):
    m, k = x.shape
    n = w.shape[1]
    return pl.pallas_call(
        _norm_matmul_kernel,
        grid=(m // tm, n // tn),
        in_specs=[pl.BlockSpec((tm, k), lambda i, j: (i, 0)),
                  pl.BlockSpec((1, k), lambda i, j: (0, 0)),
                  pl.BlockSpec((k, tn), lambda i, j: (0, j))],
        out_specs=pl.BlockSpec((tm, tn), lambda i, j: (i, j)),
        out_shape=jax.ShapeDtypeStruct((m, n), out_dtype),
        scratch_shapes=[pltpu.VMEM((tm, k), BF16)],
        compiler_params=_cparams("parallel", "arbitrary"),
        name="norm_matmul",
    )(x, g, w)


def _dil_tile(comp, group):
    return len(DIL_GROUPS) * group + comp


def _in_proj_kernel(x_ref, g_ref, w_ref, nw_ref, cos_ref, sin_ref, o_ref, h_ref, hf_ref):
    j = pl.program_id(1)
    n_dil_tiles = 3 * len(DIL_GROUPS)
    wide = DIL_HEADS * DIL_HEAD_DIM
    nc = D_MODEL // LANES

    @pl.when(j == 0)
    def _():
        h = _rms(x_ref[...], g_ref[...])
        h_ref[0] = h.astype(BF16)
        for c in range(nc):
            hf_ref[c] = h[:, c * LANES:(c + 1) * LANES]
        for g, (_, dil) in enumerate(DIL_GROUPS):
            if dil == 1:
                continue
            rows = PERM_TILE // dil
            for r in range(dil):
                for c in range(nc):
                    h_ref[g, r * rows:(r + 1) * rows, c * LANES:(c + 1) * LANES] = (
                        hf_ref[c, pl.ds(r, rows, stride=dil), :].astype(BF16))

    is_dil = j < n_dil_tiles
    group = jnp.where(is_dil, j // 3, 0)
    comp = j % 3
    is_qk = is_dil & (comp < 2)

    @pl.when(is_qk)
    def _():
        nw = nw_ref[0]
        scale = jnp.where(comp == 0, DIL_HEAD_DIM ** -0.5, 1.0).astype(F32)
        for c in range(PERM_TILE // QK_ROWS):
            rs = slice(c * QK_ROWS, (c + 1) * QK_ROWS)
            acc = jnp.dot(h_ref[group, rs, :], w_ref[...], preferred_element_type=F32)
            cos = cos_ref[0, rs, :]
            sin = sin_ref[0, rs, :]
            for h in range(wide // LANES):
                hs = slice(h * LANES, (h + 1) * LANES)
                o_ref[rs, hs] = (_rope(_rms(acc[:, hs], nw), cos, sin) * scale).astype(o_ref.dtype)

    @pl.when(jnp.logical_not(is_qk))
    def _():
        o_ref[...] = jnp.dot(h_ref[group], w_ref[...], preferred_element_type=F32).astype(o_ref.dtype)


def _in_proj(x, p, tables, seq):
    m, k = x.shape
    tn = DIL_HEADS * DIL_HEAD_DIM
    ng = len(DIL_GROUPS)
    per_seq = seq // PERM_TILE
    cos, sin = tables
    group_of = lambda j: jnp.where(j < 3 * ng, j // 3, 0)
    table_spec = pl.BlockSpec((1, PERM_TILE, LANES), lambda i, j: (group_of(j), i % per_seq, 0))
    return pl.pallas_call(
        _in_proj_kernel,
        grid=(m // PERM_TILE, IN_COLS_PAD // tn),
        in_specs=[pl.BlockSpec((PERM_TILE, k), lambda i, j: (i, 0)),
                  pl.BlockSpec((1, k), lambda i, j: (0, 0)),
                  pl.BlockSpec((k, tn), lambda i, j: (0, j)),
                  pl.BlockSpec((1, 1, LANES), lambda i, j: (jnp.minimum(j % 3, 1), 0, 0)),
                  table_spec, table_spec],
        out_specs=pl.BlockSpec((PERM_TILE, tn), lambda i, j: (i, j)),
        out_shape=jax.ShapeDtypeStruct((m, IN_COLS_PAD), BF16),
        scratch_shapes=[pltpu.VMEM((ng, PERM_TILE, k), BF16),
                        pltpu.VMEM((k // LANES, PERM_TILE, LANES), F32)],
        compiler_params=_cparams("parallel", "arbitrary"),
        name="in_proj",
    )(x, p["norm_mix"], p["w_in"], p["dil_qk_norm"], cos, sin)


TN_DIMS = (((0,), (0,)), ((), ()))


def _gla_kernel(*refs, tb, reverse, final):
    if final:
        q_ref, k_ref, v_ref, sm_ref, wg_ref, bg_ref, of_ref, r_ref, gn_ref, o_ref, st_ref = refs
    else:
        q_ref, k_ref, v_ref, sm_ref, wg_ref, bg_ref, o_ref, st_ref = refs

    @pl.when(pl.program_id(1) == 0)
    def _():
        st_ref[...] = jnp.zeros_like(st_ref)

    ck = GLA_CHUNK
    la = _log_sigmoid(jnp.dot(sm_ref[...], wg_ref[...], preferred_element_type=F32) + bg_ref[...]) * (1.0 / GLA_TAU)

    r = lax.broadcasted_iota(jnp.int32, (ck, ck), 0)
    c = lax.broadcasted_iota(jnp.int32, (ck, ck), 1)
    tri_mask = (r <= c) if reverse else (r >= c)
    tri = jnp.where(tri_mask, 1.0, 0.0).astype(BF16)
    end = 0 if reverse else ck - 1
    scale = GLA_DK ** -0.5

    n_chunks = tb // ck
    order = range(n_chunks - 1, -1, -1) if reverse else range(n_chunks)
    for ci in order:
        sl = slice(ci * ck, (ci + 1) * ck)
        la_c = la[sl]
        hi = la_c.astype(BF16)
        lo = (la_c - hi.astype(F32)).astype(BF16)
        b = jnp.dot(tri, hi, preferred_element_type=F32) + jnp.dot(tri, lo, preferred_element_type=F32)
        b_end = b[end:end + 1, :]
        kc = k_ref[sl, :].astype(F32)
        qd = (q_ref[sl, :].astype(F32) * scale * jnp.exp(b)).astype(BF16)
        kinv = (kc * jnp.exp(-b)).astype(BF16)
        ktail = (kc * jnp.exp(b_end - b)).astype(BF16)
        dec = jnp.exp(b_end)
        for h in range(GLA_HEADS):
            ks = slice(h * GLA_DK, (h + 1) * GLA_DK)
            vs = slice(h * GLA_DV, (h + 1) * GLA_DV)
            qh = qd[:, ks]
            att = lax.dot_general(qh, kinv[:, ks], NT_DIMS, preferred_element_type=F32)
            att = jnp.where(tri_mask, att, 0.0).astype(BF16)
            vh = v_ref[sl, vs]
            st = st_ref[h]
            o = (jnp.dot(att, vh, preferred_element_type=F32)
                 + lax.dot_general(qh, st.astype(BF16), NT_DIMS, preferred_element_type=F32))
            st_ref[h] = dec[:, ks] * st + lax.dot_general(vh, ktail[:, ks], TN_DIMS, preferred_element_type=F32)
            if final:
                o = _rms(o + of_ref[sl, vs], gn_ref[...])
                rr = r_ref[sl, vs].astype(F32)
                o = o * (rr * _sigmoid(rr))
            o_ref[sl, vs] = o.astype(o_ref.dtype)


def _gla_direction(proj, wg, bg, batch, seq, reverse, o_fwd=None, g_out=None, tb=256):
    nb = seq // tb
    m = batch * seq
    final = o_fwd is not None

    def row(b, i):
        return b * nb + (nb - 1 - i if reverse else i)

    def col(block_cols, off):
        return lambda b, i: (row(b, i), off // block_cols)

    hk = GLA_HEADS * GLA_DK
    hv = GLA_HEADS * GLA_DV
    const = lambda b, i: (0, 0)
    in_specs = [pl.BlockSpec((tb, hk), col(hk, COL_AQ)),
                pl.BlockSpec((tb, hk), col(hk, COL_AK)),
                pl.BlockSpec((tb, hv), col(hv, COL_AV)),
                pl.BlockSpec((tb, LANES), col(LANES, COL_SMALL)),
                pl.BlockSpec((LANES, hk), const),
                pl.BlockSpec((1, hk), const)]
    args = [proj, proj, proj, proj, wg, bg]
    if final:
        in_specs += [pl.BlockSpec((tb, hv), col(hv, 0)),
                     pl.BlockSpec((tb, hv), col(hv, COL_AR)),
                     pl.BlockSpec((1, GLA_DV), const)]
        args += [o_fwd, proj, g_out]
    return pl.pallas_call(
        functools.partial(_gla_kernel, tb=tb, reverse=reverse, final=final),
        grid=(batch, nb),
        in_specs=in_specs,
        out_specs=pl.BlockSpec((tb, hv), col(hv, 0)),
        out_shape=jax.ShapeDtypeStruct((m, hv), BF16 if final else F32),
        scratch_shapes=[pltpu.VMEM((GLA_HEADS, GLA_DV, GLA_DK), F32)],
        compiler_params=_cparams("parallel", "arbitrary"),
        name="gla_bwd" if reverse else "gla_fwd",
    )(*args)


def _rope(y, cos, sin_signed):
    return y * cos + pltpu.roll(y, LANES // 2, 1) * sin_signed


def _mla_prep_kernel(bq_ref, bkv_ref, sm_ref, qln_ref, wuq_ref, kvln_ref, wuk_ref, wuv_ref,
                     qn_ref, kn_ref, cos_ref, sin_ref, q_out, k_out, v_out):
    hq = _rms(bq_ref[...].astype(F32), qln_ref[...]).astype(BF16)
    hkv = _rms(bkv_ref[...].astype(F32), kvln_ref[...]).astype(BF16)
    q = jnp.dot(hq, wuq_ref[...], preferred_element_type=F32)
    kn = jnp.dot(hkv, wuk_ref[...], preferred_element_type=F32)
    v = jnp.dot(hkv, wuv_ref[...], preferred_element_type=F32)
    tm = q.shape[0]
    lane = lax.broadcasted_iota(jnp.int32, (tm, LANES), 1)
    sm = sm_ref[...].astype(F32)
    in_rope = (lane < ROPE_LO + ROPE_HALF) | ((lane >= ROPE_HI) & (lane < ROPE_HI + ROPE_HALF))
    kpe = jnp.where(in_rope, sm, 0.0)
    cos = cos_ref[...]
    sin = sin_ref[...]
    scale = MLA_QK ** -0.5 * LOG2_E
    ones = jnp.ones((tm, LANES), v_out.dtype)
    for h in range(MLA_HEADS):
        hs = slice(h * LANES, (h + 1) * LANES)
        qh = _rope(_rms(q[:, hs], qn_ref[...], MLA_QK), cos, sin)
        q_out[:, hs] = (qh * scale).astype(q_out.dtype)
        kh = _rope(_rms(kn[:, hs] + kpe, kn_ref[...], MLA_QK), cos, sin)
        k_out[:, hs] = kh.astype(k_out.dtype)
        v_out[:, 2 * h * LANES:(2 * h + 1) * LANES] = v[:, hs].astype(v_out.dtype)
        v_out[:, (2 * h + 1) * LANES:(2 * h + 2) * LANES] = ones


def _mla_prep(proj, p, cos, sin, seq, tm=512):
    m = proj.shape[0]
    ns = seq // tm
    const = lambda i: (0, 0)
    wide = MLA_HEADS * LANES
    out = jax.ShapeDtypeStruct((m, wide), BF16)
    return pl.pallas_call(
        _mla_prep_kernel,
        grid=(m // tm,),
        in_specs=[pl.BlockSpec((tm, MLA_Q_RANK), lambda i: (i, COL_BQ // MLA_Q_RANK)),
                  pl.BlockSpec((tm, MLA_KV_RANK), lambda i: (i, COL_BKV // MLA_KV_RANK)),
                  pl.BlockSpec((tm, LANES), lambda i: (i, COL_SMALL // LANES)),
                  pl.BlockSpec((1, MLA_Q_RANK), const),
                  pl.BlockSpec((MLA_Q_RANK, wide), const),
                  pl.BlockSpec((1, MLA_KV_RANK), const),
                  pl.BlockSpec((MLA_KV_RANK, wide), const),
                  pl.BlockSpec((MLA_KV_RANK, wide), const),
                  pl.BlockSpec((1, LANES), const),
                  pl.BlockSpec((1, LANES), const),
                  pl.BlockSpec((tm, LANES), lambda i: (i % ns, 0)),
                  pl.BlockSpec((tm, LANES), lambda i: (i % ns, 0))],
        out_specs=[pl.BlockSpec((tm, wide), lambda i: (i, 0)), pl.BlockSpec((tm, wide), lambda i: (i, 0)),
                   pl.BlockSpec((tm, 2 * wide), lambda i: (i, 0))],
        out_shape=[out, out, jax.ShapeDtypeStruct((m, 2 * wide), BF16)],
        compiler_params=_cparams("parallel"),
        name="mla_prep",
    )(proj, proj, proj, p["mla_q_lat_norm"], p["mla_w_uq"], p["mla_kv_lat_norm"], p["mla_w_uk"],
      p["mla_w_uv"], p["mla_q_norm"], p["mla_k_norm"], cos, sin)


def _mla_attn_kernel(q_ref, k_ref, v_ref, o_ref, *, tk, unroll, sub):
    nsub = q_ref.shape[0] // sub
    nk = k_ref.shape[0] // tk
    qs = [q_ref[i * sub:(i + 1) * sub, :] for i in range(nsub)]

    def body(t, carry):
        start = pl.multiple_of(t * tk, tk)
        out = []
        for q, (m, acc) in zip(qs, carry):
            s = lax.dot_general(q, k_ref[pl.ds(start, tk), :], NT_DIMS, preferred_element_type=F32)
            m_new = jnp.maximum(m, jnp.max(s, axis=-1, keepdims=True))
            p = jnp.exp2(s - m_new).astype(BF16)
            acc = jnp.exp2(m - m_new) * acc + jnp.dot(p, v_ref[pl.ds(start, tk), :], preferred_element_type=F32)
            out.append((m_new, acc))
        return tuple(out)

    init = tuple((jnp.full((sub, 1), -jnp.inf, F32), jnp.zeros((sub, 2 * LANES), F32)) for _ in range(nsub))
    final = lax.fori_loop(0, nk, body, init, unroll=unroll)
    for i, (_, acc) in enumerate(final):
        o_ref[i * sub:(i + 1) * sub, :] = (acc[:, :LANES] / acc[:, LANES:]).astype(o_ref.dtype)


MLA_KV_TILE = 256
MLA_TILES_IN_FLIGHT = 64
MLA_SUB_ROWS = (256, 512)


def _mla_attn(q, k, v, batch, seq, tk=MLA_KV_TILE):
    unroll = min(MLA_TILES_IN_FLIGHT, seq // tk)
    tq = min(seq, MLA_SUB_ROWS[0] * (MLA_TILES_IN_FLIGHT // unroll))
    sub = min(tq, MLA_SUB_ROWS[unroll < MLA_TILES_IN_FLIGHT])
    nq = seq // tq
    return pl.pallas_call(
        functools.partial(_mla_attn_kernel, tk=tk, unroll=unroll, sub=sub),
        grid=(batch, MLA_HEADS, nq),
        in_specs=[pl.BlockSpec((tq, LANES), lambda b, h, i: (b * nq + i, h)),
                  pl.BlockSpec((seq, LANES), lambda b, h, i: (b, h)),
                  pl.BlockSpec((seq, 2 * LANES), lambda b, h, i: (b, h))],
        out_specs=pl.BlockSpec((tq, LANES), lambda b, h, i: (b * nq + i, h)),
        out_shape=jax.ShapeDtypeStruct(q.shape, BF16),
        compiler_params=_cparams("parallel", "parallel", "arbitrary"),
        name="mla_attn",
    )(q, k, v)


DIL_SUB_BLOCKS = 2
DIL_MAX_Q_BLOCKS = 4


def _dil_attn_kernel(*refs, sub_len, q_blocks):
    k_blocks = q_blocks + 2
    q_ref = refs[0]
    k_refs = refs[1:1 + k_blocks]
    v_refs = refs[1 + k_blocks:1 + 2 * k_blocks]
    o_ref, lse_ref = refs[1 + 2 * k_blocks:]
    piece_rows = q_ref.shape[1]
    tl = DIL_SUB_BLOCKS * HALO
    nkeys = (DIL_SUB_BLOCKS + 2) * HALO

    def row_runs(start, n):
        runs = []
        while n > 0:
            off = start % piece_rows
            take = min(n, piece_rows - off)
            runs.append((start // piece_rows, slice(off, off + take)))
            start, n = start + take, n - take
        return runs

    row = lax.broadcasted_iota(jnp.int32, (tl, nkeys), 0)
    col = lax.broadcasted_iota(jnp.int32, (tl, nkeys), 1)
    rel = col - row
    in_band = (rel >= 0) & (rel <= 2 * DIL_RADIUS)
    lane = lax.broadcasted_iota(jnp.int32, (tl, LANES), 1)
    for sub in range(q_blocks // DIL_SUB_BLOCKS):
        q_runs = row_runs(sub * tl, tl)
        pos = (pl.program_id(2) * q_blocks + sub * DIL_SUB_BLOCKS - 1) * HALO + col
        valid = in_band & (pos >= 0) & (pos < sub_len)
        kb = range(sub * DIL_SUB_BLOCKS, sub * DIL_SUB_BLOCKS + DIL_SUB_BLOCKS + 2)
        lse_all = jnp.zeros((tl, LANES), F32)
        for h in range(DIL_HEADS):
            hs = slice(h * LANES, (h + 1) * LANES)
            qq = jnp.concatenate([q_ref[p, rs, hs] for p, rs in q_runs], axis=0)
            kk = jnp.concatenate([k_refs[n][:, hs] for n in kb], axis=0)
            vv = jnp.concatenate([v_refs[n][:, hs] for n in kb], axis=0)
            s = lax.dot_general(qq, kk, NT_DIMS, preferred_element_type=F32)
            s = jnp.where(valid, s, -jnp.inf)
            m = jnp.max(s, axis=-1, keepdims=True)
            p = jnp.exp(s - m)
            l = jnp.sum(p, axis=-1, keepdims=True)
            o = jnp.dot(p.astype(BF16), vv, preferred_element_type=F32) / l
            done = 0
            for pc, rs in q_runs:
                o_ref[pc, rs, hs] = o[done:done + rs.stop - rs.start].astype(o_ref.dtype)
                done += rs.stop - rs.start
            lse_all = jnp.where(lane == h, m + jnp.log(l), lse_all)
        done = 0
        for pc, rs in q_runs:
            lse_ref[pc, rs, :] = lse_all[done:done + rs.stop - rs.start]
            done += rs.stop - rs.start


def _dil_attn(proj, group, batch, seq):
    _, dil = DIL_GROUPS[group]
    sub_len = seq // dil
    n_blocks = sub_len // HALO
    chunk_blocks = PERM_TILE // dil // HALO
    tile_blocks = PERM_TILE // HALO
    wide = DIL_HEADS * DIL_HEAD_DIM

    def row_block(b, r, n):
        n = jnp.clip(n, 0, n_blocks - 1)
        return (b * (seq // HALO) + (n // chunk_blocks) * tile_blocks + r * chunk_blocks + n % chunk_blocks)

    q_blocks = min(DIL_MAX_Q_BLOCKS, n_blocks)
    k_blocks = q_blocks + 2

    def halo_spec(comp, offset):
        return pl.BlockSpec((HALO, wide), lambda b, r, i: (row_block(b, r, q_blocks * i + offset),
                                                          _dil_tile(comp, group)))

    tl = q_blocks * HALO
    chunk = PERM_TILE // dil
    pieces = max(1, tl // chunk)
    piece_rows = tl // pieces
    subs = chunk // piece_rows
    m = batch * seq
    lead = m // (PERM_TILE * pieces)
    per_seq = seq // (PERM_TILE * pieces)

    def query_view(width):
        return (lead, pieces, dil, chunk, width)

    def query_spec(width, col_block):
        return pl.BlockSpec((None, pieces, None, piece_rows, width),
                            lambda b, r, i: (b * per_seq + i // subs, 0, r, i % subs, col_block))

    o, lse = pl.pallas_call(
        functools.partial(_dil_attn_kernel, sub_len=sub_len, q_blocks=q_blocks),
        grid=(batch, dil, n_blocks // q_blocks),
        in_specs=([query_spec(wide, _dil_tile(0, group))]
                  + [halo_spec(1, n - 1) for n in range(k_blocks)]
                  + [halo_spec(2, n - 1) for n in range(k_blocks)]),
        out_specs=[query_spec(wide, 0), query_spec(LANES, 0)],
        out_shape=[jax.ShapeDtypeStruct(query_view(wide), BF16), jax.ShapeDtypeStruct(query_view(LANES), F32)],
        compiler_params=_cparams("parallel", "parallel", "arbitrary"),
        name=f"dil_attn_g{group}",
    )(proj.reshape(query_view(IN_COLS_PAD)), *([proj] * (2 * k_blocks)))
    return o.reshape(m, wide), lse.reshape(m, LANES)


def _combine_kernel(x_ref, a_ref, b_ref, o0_ref, o1_ref, o2_ref, l0_ref, l1_ref, l2_ref, gate_ref,
                    wa_ref, wb_ref, wc_ref, wo_ref, out_ref, c_ref, on_refs, ln_refs):
    tm = x_ref.shape[0]
    o_tok, l_tok = [], []
    for g, (o_ref, l_ref) in enumerate(((o0_ref, l0_ref), (o1_ref, l1_ref), (o2_ref, l2_ref))):
        dil = DIL_GROUPS[g][1]
        if dil == 1:
            o_tok.append(lambda h, o_ref=o_ref: o_ref[0, 0, :, h * LANES:(h + 1) * LANES].astype(F32))
            l_tok.append(l_ref[0, 0])
            continue
        for r in range(dil):
            rows = pl.ds(r, tm // dil, stride=dil)
            ln_refs[g - 1, rows, :] = l_ref[0, r]
            for h in range(DIL_HEADS):
                on_refs[g - 1, h, rows, :] = o_ref[0, r, :, h * LANES:(h + 1) * LANES].astype(F32)
        o_tok.append(lambda h, g=g: on_refs[g - 1, h])
        l_tok.append(ln_refs[g - 1])
    l0, l1, l2 = l_tok
    mx = jnp.maximum(jnp.maximum(l0, l1), l2)
    e0, e1, e2 = jnp.exp(l0 - mx), jnp.exp(l1 - mx), jnp.exp(l2 - mx)
    inv = 1.0 / (e0 + e1 + e2)
    w0, w1, w2 = e0 * inv, e1 * inv, e2 * inv
    for h in range(DIL_HEADS):
        hs = slice(h * LANES, (h + 1) * LANES)
        c = w0[:, h:h + 1] * o_tok[0](h) + w1[:, h:h + 1] * o_tok[1](h) + w2[:, h:h + 1] * o_tok[2](h)
        c_ref[:, hs] = c.astype(BF16)
    ya = jnp.dot(a_ref[...], wa_ref[...], preferred_element_type=F32)
    yb = jnp.dot(b_ref[...], wb_ref[...], preferred_element_type=F32)
    yc = jnp.dot(c_ref[...], wc_ref[...], preferred_element_type=F32)
    d = D_MODEL
    mix = (_sigmoid(gate_ref[:, 0:d].astype(F32)) * ya
           + _sigmoid(gate_ref[:, d:2 * d].astype(F32)) * yb
           + _sigmoid(gate_ref[:, 2 * d:3 * d].astype(F32)) * yc)
    out_ref[...] = x_ref[...] + jnp.dot(mix.astype(BF16), wo_ref[...], preferred_element_type=F32)


def _combine(x, gla_o, mla_o, dil_o, dil_lse, proj, p, batch, seq, tm=256):
    m = x.shape[0]
    d = D_MODEL
    ns = PERM_TILE // tm
    rowblk = lambda i: (i, 0)
    const = lambda i: (0, 0)
    wspec = pl.BlockSpec((d, d), const)

    def residue_major(arr, group, width):
        dil = DIL_GROUPS[group][1]
        view = arr.reshape(m // PERM_TILE, dil, PERM_TILE // dil, width)
        return view, pl.BlockSpec((1, dil, tm // dil, width), lambda i: (i // ns, 0, i % ns, 0))

    o_views, o_specs = zip(*[residue_major(dil_o[g], g, d) for g in range(len(DIL_GROUPS))])
    l_views, l_specs = zip(*[residue_major(dil_lse[g], g, LANES) for g in range(len(DIL_GROUPS))])
    n_perm = len(DIL_GROUPS) - 1
    return pl.pallas_call(
        _combine_kernel,
        grid=(m // tm,),
        in_specs=[pl.BlockSpec((tm, d), rowblk), pl.BlockSpec((tm, d), rowblk), pl.BlockSpec((tm, d), rowblk),
                  *o_specs, *l_specs,
                  pl.BlockSpec((tm, 3 * d), lambda i: (i, COL_GATE // (3 * d))),
                  wspec, wspec, wspec, wspec],
        out_specs=pl.BlockSpec((tm, d), rowblk),
        out_shape=jax.ShapeDtypeStruct((m, d), F32),
        scratch_shapes=[pltpu.VMEM((tm, d), BF16), pltpu.VMEM((n_perm, DIL_HEADS, tm, LANES), F32),
                        pltpu.VMEM((n_perm, tm, LANES), F32)],
        compiler_params=_cparams("parallel"),
        name="combine",
    )(x, gla_o, mla_o, *o_views, *l_views, proj,
      p["w_branch_a"], p["w_branch_b"], p["w_branch_c"], p["w_out"])


def _xattn_kernel(x_ref, g_ref, wq_ref, qn_ref, kv_ref, kn_ref, wo_ref, out_ref, o_ref):
    x = x_ref[...]
    h = _rms(x, g_ref[...]).astype(BF16)
    q = jnp.dot(h, wq_ref[...], preferred_element_type=F32)
    scale = X_HEAD_DIM ** -0.5
    for hd in range(X_HEADS):
        hs = slice(hd * X_HEAD_DIM, (hd + 1) * X_HEAD_DIM)
        vs = slice(D_MODEL + hd * X_HEAD_DIM, D_MODEL + (hd + 1) * X_HEAD_DIM)
        qh = (_rms(q[:, hs], qn_ref[...]) * scale).astype(BF16)
        kh = _rms(kv_ref[:, hs].astype(F32), kn_ref[...]).astype(BF16)
        s = lax.dot_general(qh, kh, NT_DIMS, preferred_element_type=F32)
        p = jnp.exp(s - jnp.max(s, axis=-1, keepdims=True))
        l = jnp.sum(p, axis=-1, keepdims=True)
        o = jnp.dot(p.astype(BF16), kv_ref[:, vs], preferred_element_type=F32) / l
        o_ref[:, hs] = o.astype(BF16)
    out_ref[...] = x + jnp.dot(o_ref[...], wo_ref[...], preferred_element_type=F32)


def _xattn(x, kv, p, seq, tm=512):
    m = x.shape[0]
    d = D_MODEL
    per_seq = seq // tm
    const = lambda i: (0, 0)
    return pl.pallas_call(
        _xattn_kernel,
        grid=(m // tm,),
        in_specs=[pl.BlockSpec((tm, d), lambda i: (i, 0)),
                  pl.BlockSpec((1, d), const),
                  pl.BlockSpec((d, d), const),
                  pl.BlockSpec((1, X_HEAD_DIM), const),
                  pl.BlockSpec((N_MEM, 2 * d), lambda i: (i // per_seq, 0)),
                  pl.BlockSpec((1, X_HEAD_DIM), const),
                  pl.BlockSpec((d, d), const)],
        out_specs=pl.BlockSpec((tm, d), lambda i: (i, 0)),
        out_shape=jax.ShapeDtypeStruct((m, d), F32),
        scratch_shapes=[pltpu.VMEM((tm, d), BF16)],
        compiler_params=_cparams("parallel"),
        name="xattn",
    )(x, p["norm_xattn"], p["w_xq"], p["xq_norm"], kv, p["xk_norm"], p["w_xo"])


def _ffn_kernel(x_ref, g_ref, w1_ref, w2_ref, out_ref, h_ref, acc_ref):
    j = pl.program_id(1)

    @pl.when(j == 0)
    def _():
        h_ref[...] = _rms(x_ref[...], g_ref[...]).astype(BF16)
        acc_ref[...] = x_ref[...]

    u = jnp.maximum(jnp.dot(h_ref[...], w1_ref[...], preferred_element_type=F32), 0.0)
    acc_ref[...] += jnp.dot((u * u).astype(BF16), w2_ref[...], preferred_element_type=F32)

    @pl.when(j == pl.num_programs(1) - 1)
    def _():
        out_ref[...] = acc_ref[...]


def _ffn(x, p, tm=1024, tf=1024):
    m = x.shape[0]
    d = D_MODEL
    return pl.pallas_call(
        _ffn_kernel,
        grid=(m // tm, D_FF // tf),
        in_specs=[pl.BlockSpec((tm, d), lambda i, j: (i, 0)),
                  pl.BlockSpec((1, d), lambda i, j: (0, 0)),
                  pl.BlockSpec((d, tf), lambda i, j: (0, j)),
                  pl.BlockSpec((tf, d), lambda i, j: (j, 0))],
        out_specs=pl.BlockSpec((tm, d), lambda i, j: (i, 0)),
        out_shape=jax.ShapeDtypeStruct((m, d), F32),
        scratch_shapes=[pltpu.VMEM((tm, d), BF16), pltpu.VMEM((tm, d), F32)],
        compiler_params=_cparams("parallel", "arbitrary"),
        name="ffn",
    )(x, p["norm_ffn"], p["w_ff1"], p["w_ff2"])


def _rope_tables(positions):
    seq = len(positions)
    inv_freq = 1.0 / (ROPE_THETA ** (np.arange(0, 2 * ROPE_HALF, 2, dtype=np.float64) / (2 * ROPE_HALF)))
    ang = np.asarray(positions, np.float64)[:, None] * inv_freq[None, :]
    cos, sin = jnp.asarray(np.cos(ang), F32), jnp.asarray(np.sin(ang), F32)
    gap = ROPE_HI - ROPE_LO - ROPE_HALF
    tail = LANES - ROPE_HI - ROPE_HALF
    fill = lambda n, v: jnp.full((seq, n), v, F32)
    cos_t = jnp.concatenate([fill(ROPE_LO, 1.0), cos, fill(gap, 1.0), cos, fill(tail, 1.0)], axis=1)
    sin_t = jnp.concatenate([fill(ROPE_LO, 0.0), -sin, fill(gap, 0.0), sin, fill(tail, 0.0)], axis=1)
    return cos_t, sin_t


def _lanes_after(*taken):
    used = set(int(i) for t in taken for i in t)
    return [i for i in range(LANES) if i not in used]


_ROPE_LANES = list(range(ROPE_LO, ROPE_LO + ROPE_HALF)) + list(range(ROPE_HI, ROPE_HI + ROPE_HALF))
_MLA_LANE_OF_DIM = np.array(_lanes_after(_ROPE_LANES)[:MLA_NOPE] + _ROPE_LANES, np.int32)
_DIL_LANE_OF_DIM = np.array(_ROPE_LANES + _lanes_after(_ROPE_LANES), np.int32)


def _to_lanes(arr, lane_of_dim):
    dim_of_lane = np.full(LANES, -1)
    dim_of_lane[lane_of_dim] = np.arange(len(lane_of_dim))
    pieces, lane = [], 0
    while lane < LANES:
        run = 1
        while (lane + run < LANES and
               (dim_of_lane[lane + run] == dim_of_lane[lane] + run if dim_of_lane[lane] >= 0
                else dim_of_lane[lane + run] < 0)):
            run += 1
        start = int(dim_of_lane[lane])
        pieces.append(arr[..., start:start + run] if start >= 0
                      else jnp.zeros(arr.shape[:-1] + (run,), arr.dtype))
        lane += run
    return jnp.concatenate(pieces, axis=-1)


def _prep_layer(w):
    d = D_MODEL
    hk = GLA_HEADS * GLA_DK
    hv = GLA_HEADS * GLA_DV
    dil_w = len(DIL_GROUPS) * DIL_HEADS * DIL_HEAD_DIM
    sizes = (hk, hk, hv, hv, GLA_GATE_RANK, GLA_GATE_RANK, MLA_Q_RANK, MLA_KV_RANK, MLA_ROPE,
             dil_w, dil_w, dil_w, 3 * d)
    offs = [0]
    for s in sizes:
        offs.append(offs[-1] + s)
    (a_q, a_k, a_v, a_r, a_gf, a_gb, b_q, b_kv, b_kpe, c_q, c_k, c_v, gates) = [
        w["w_in"][:, offs[i]:offs[i + 1]] for i in range(len(sizes))]
    z = lambda n: jnp.zeros((d, n), F32)
    small_lane_of_dim = np.concatenate([_MLA_LANE_OF_DIM[MLA_NOPE:], SMALL_GF + np.arange(GLA_GATE_RANK),
                                        SMALL_GB + np.arange(GLA_GATE_RANK)])
    small = _to_lanes(jnp.concatenate([b_kpe, a_gf, a_gb], axis=1), small_lane_of_dim)

    def dil_heads(cols):
        per_head = cols.reshape(d, len(DIL_GROUPS) * DIL_HEADS, DIL_HEAD_DIM)
        return _to_lanes(per_head, _DIL_LANE_OF_DIM).reshape(d, dil_w)

    gw = DIL_HEADS * DIL_HEAD_DIM
    dil_tiles = [t[:, g * gw:(g + 1) * gw] for g in range(len(DIL_GROUPS))
                 for t in (dil_heads(c_q), dil_heads(c_k), c_v)]
    w_in = jnp.concatenate(dil_tiles + [gates, a_q, a_k, a_v, a_r, b_q, b_kv, small,
                                        z(IN_COLS_PAD - COL_SMALL - LANES)], axis=1).astype(BF16)

    def gate_w(wg, lane0):
        rows = lambda n: jnp.zeros((n, hk), F32)
        return jnp.concatenate([rows(lane0), wg, rows(LANES - lane0 - GLA_GATE_RANK)], axis=0).astype(BF16)

    wgf = gate_w(w["gla_w_gate_f"], SMALL_GF)
    wgb = gate_w(w["gla_w_gate_b"], SMALL_GB)

    w_uq = _to_lanes(w["mla_w_uq"].reshape(MLA_Q_RANK, MLA_HEADS, MLA_QK), _MLA_LANE_OF_DIM)
    w_uq = w_uq.reshape(MLA_Q_RANK, MLA_HEADS * LANES)
    w_ukv = w["mla_w_ukv"].reshape(MLA_KV_RANK, MLA_HEADS, MLA_NOPE + MLA_V)
    w_uk = _to_lanes(w_ukv[:, :, :MLA_NOPE], _MLA_LANE_OF_DIM[:MLA_NOPE])
    w_uk = w_uk.reshape(MLA_KV_RANK, MLA_HEADS * LANES)
    w_uv = w_ukv[:, :, MLA_NOPE:].reshape(MLA_KV_RANK, MLA_HEADS * MLA_V)
    row = lambda v: v.reshape(1, -1).astype(F32)
    return {
        "norm_mix": row(w["norm_mix"]), "w_in": w_in,
        "gla_wgf": wgf, "gla_bgf": row(w["gla_b_gate_f"]),
        "gla_wgb": wgb, "gla_bgb": row(w["gla_b_gate_b"]),
        "gla_out_norm": row(w["gla_out_norm"]), "w_branch_a": w["w_branch_a"].astype(BF16),
        "mla_q_lat_norm": row(w["mla_q_lat_norm"]), "mla_w_uq": w_uq.astype(BF16),
        "mla_kv_lat_norm": row(w["mla_kv_lat_norm"]), "mla_w_uk": w_uk.astype(BF16),
        "mla_w_uv": w_uv.astype(BF16),
        "mla_q_norm": _to_lanes(w["mla_q_norm"], _MLA_LANE_OF_DIM).reshape(1, LANES),
        "mla_k_norm": _to_lanes(w["mla_k_norm"], _MLA_LANE_OF_DIM).reshape(1, LANES),
        "w_branch_b": w["w_branch_b"].astype(BF16),
        "dil_qk_norm": _to_lanes(jnp.stack([w["dil_q_norm"], w["dil_k_norm"]]),
                                 _DIL_LANE_OF_DIM).reshape(2, 1, LANES),
        "w_branch_c": w["w_branch_c"].astype(BF16), "w_out": w["w_out"].astype(BF16),
        "norm_xattn": row(w["norm_xattn"]), "norm_mem": row(w["norm_mem"]),
        "w_xq": w["w_xq"].astype(BF16), "w_xkv": w["w_xkv"].astype(BF16),
        "xq_norm": row(w["xq_norm"]), "xk_norm": row(w["xk_norm"]), "w_xo": w["w_xo"].astype(BF16),
        "norm_ffn": row(w["norm_ffn"]), "w_ff1": w["w_ff1"].astype(BF16), "w_ff2": w["w_ff2"].astype(BF16),
    }


def _layer(x, mem, p, tables, batch, seq):
    (cos_m, sin_m), dil_tables = tables
    proj = _in_proj(x, p, dil_tables, seq)
    o_fwd = _gla_direction(proj, p["gla_wgf"], p["gla_bgf"], batch, seq, False)
    gla_o = _gla_direction(proj, p["gla_wgb"], p["gla_bgb"], batch, seq, True,
                           o_fwd=o_fwd, g_out=p["gla_out_norm"])
    q_m, k_m, v_m = _mla_prep(proj, p, cos_m, sin_m, seq)
    mla_o = _mla_attn(q_m, k_m, v_m, batch, seq)
    dil = [_dil_attn(proj, g, batch, seq) for g in range(len(DIL_GROUPS))]
    x = _combine(x, gla_o, mla_o, [o for o, _ in dil], [l for _, l in dil], proj, p, batch, seq)
    kv = _norm_matmul(mem, p["norm_mem"], p["w_xkv"], BF16, tm=N_MEM, tn=1024)
    x = _xattn(x, kv, p, seq)
    return _ffn(x, p)


def _trunk(x, mem, layers):
    batch, seq, d = x.shape
    natural = np.arange(seq)
    row_pos = [natural.reshape(seq // PERM_TILE, PERM_TILE // dil, dil).transpose(0, 2, 1).reshape(seq)
               for _, dil in DIL_GROUPS]
    dil_cos, dil_sin = zip(*[_rope_tables(pos) for pos in row_pos])
    tables = (_rope_tables(natural), (jnp.stack(dil_cos), jnp.stack(dil_sin)))
    xf = x.reshape(batch * seq, d)
    memf = mem.reshape(batch * mem.shape[1], d)
    for p in layers:
        xf = _layer(xf, memf, p, tables, batch, seq)
    return xf.reshape(batch, seq, d)


def kernel(x_prompt, x_sample, mem_prompt, mem_sample, norm_mix, w_in, gla_w_gate_f, gla_b_gate_f, gla_w_gate_b, gla_b_gate_b, gla_out_norm, w_branch_a, mla_q_lat_norm, mla_w_uq, mla_kv_lat_norm, mla_w_ukv, mla_q_norm, mla_k_norm, w_branch_b, dil_q_norm, dil_k_norm, w_branch_c, w_out, norm_xattn, norm_mem, w_xq, w_xkv, xq_norm, xk_norm, w_xo, norm_ffn, w_ff1, w_ff2):
    stacked = dict(norm_mix=norm_mix, w_in=w_in, gla_w_gate_f=gla_w_gate_f, gla_b_gate_f=gla_b_gate_f,
                   gla_w_gate_b=gla_w_gate_b, gla_b_gate_b=gla_b_gate_b, gla_out_norm=gla_out_norm,
                   w_branch_a=w_branch_a, mla_q_lat_norm=mla_q_lat_norm, mla_w_uq=mla_w_uq,
                   mla_kv_lat_norm=mla_kv_lat_norm, mla_w_ukv=mla_w_ukv, mla_q_norm=mla_q_norm,
                   mla_k_norm=mla_k_norm, w_branch_b=w_branch_b, dil_q_norm=dil_q_norm, dil_k_norm=dil_k_norm,
                   w_branch_c=w_branch_c, w_out=w_out, norm_xattn=norm_xattn, norm_mem=norm_mem, w_xq=w_xq,
                   w_xkv=w_xkv, xq_norm=xq_norm, xk_norm=xk_norm, w_xo=w_xo, norm_ffn=norm_ffn,
                   w_ff1=w_ff1, w_ff2=w_ff2)
    layers = [_prep_layer({k: v[l] for k, v in stacked.items()}) for l in range(DEPTH)]
    return (_trunk(x_prompt, mem_prompt, layers), _trunk(x_sample, mem_sample, layers))
```

```python
import functools

import jax
import jax.numpy as jnp
import numpy as np
from jax import lax
from jax.experimental import pallas as pl
from jax.experimental.pallas import tpu as pltpu

F32 = jnp.float32
BF16 = jnp.bfloat16

D_MODEL = 1024
DEPTH = 2
N_MEM = 256
ROPE_THETA = 500000.0
NORM_EPS = 1e-6
GLA_HEADS = 4
GLA_DK = 128
GLA_DV = 256
GLA_GATE_RANK = 16
GLA_TAU = 16.0
GLA_CHUNK = 64
MLA_HEADS = 8
MLA_Q_RANK = 256
MLA_KV_RANK = 128
MLA_NOPE = 64
MLA_ROPE = 32
MLA_QK = 96
MLA_V = 128
DIL_GROUPS = ((128, 1), (512, 4), (2048, 16))
DIL_HEADS = 8
DIL_HEAD_DIM = 128
DIL_ROT = 32
DIL_RADIUS = 64
X_HEADS = 4
X_HEAD_DIM = 256
D_FF = 4096

LANES = 128
ROPE_HALF = 16
VMEM_LIMIT_BYTES = 56 * 1024 * 1024

COL_GATE = 9216
PERM_TILE = 1024
HALO = 64
QK_ROWS = 256
COL_AQ, COL_AK, COL_AV, COL_AR = 12288, 12800, 13312, 14336
COL_BQ, COL_BKV, COL_SMALL = 15360, 15616, 15744
IN_COLS_PAD = 16384
SMALL_GF, SMALL_GB = 16, 32
ROPE_LO, ROPE_HI = 0, 64

NT_DIMS = (((1,), (1,)), ((), ()))
LOG2_E = 1.4426950408889634


def _cparams(*sem):
    return pltpu.CompilerParams(dimension_semantics=sem, vmem_limit_bytes=VMEM_LIMIT_BYTES)


def _rms(x, g, n=None):
    ss = jnp.sum(x * x, axis=-1, keepdims=True) * (1.0 / (n or x.shape[-1]))
    return x * lax.rsqrt(ss + NORM_EPS) * g


def _rms_head_mxu(x, g, n):
    ones = jnp.ones((LANES, LANES), BF16)
    ss = jnp.dot((x * x).astype(BF16), ones, preferred_element_type=F32)
    return x * lax.rsqrt(ss * (1.0 / n) + NORM_EPS) * g


def _log_sigmoid(z):
    return jnp.minimum(z, 0.0) - jnp.log(1.0 + jnp.exp(-jnp.abs(z)))


def _sigmoid(z):
    return 1.0 / (1.0 + jnp.exp(-z))


def _norm_matmul_kernel(x_ref, g_ref, w_ref, o_ref, h_ref):
    @pl.when(pl.program_id(1) == 0)
    def _():
        h_ref[...] = _rms(x_ref[...].astype(F32), g_ref[...]).astype(BF16)

    o_ref[...] = jnp.dot(h_ref[...], w_ref[...], preferred_element_type=F32).astype(o_ref.dtype)


def _norm_matmul(x, g, w, out_dtype, tm, tn):
    m, k = x.shape
    n = w.shape[1]
    return pl.pallas_call(
        _norm_matmul_kernel,
        grid=(m // tm, n // tn),
        in_specs=[pl.BlockSpec((tm, k), lambda i, j: (i, 0)),
                  pl.BlockSpec((1, k), lambda i, j: (0, 0)),
                  pl.BlockSpec((k, tn), lambda i, j: (0, j))],
        out_specs=pl.BlockSpec((tm, tn), lambda i, j: (i, j)),
        out_shape=jax.ShapeDtypeStruct((m, n), out_dtype),
        scratch_shapes=[pltpu.VMEM((tm, k), BF16)],
        compiler_params=_cparams("parallel", "arbitrary"),
        name="norm_matmul",
    )(x, g, w)


def _dil_tile(comp, group):
    return len(DIL_GROUPS) * group + comp


def _in_proj_kernel(x_ref, g_ref, w_ref, nw_ref, cos_ref, sin_ref, o_ref, h_ref, hf_ref):
    j = pl.program_id(1)
    n_dil_tiles = 3 * len(DIL_GROUPS)
    wide = DIL_HEADS * DIL_HEAD_DIM
    nc = D_MODEL // LANES

    @pl.when(j == 0)
    def _():
        h = _rms(x_ref[...], g_ref[...])
        h_ref[0] = h.astype(BF16)
        for c in range(nc):
            hf_ref[c] = h[:, c * LANES:(c + 1) * LANES]
        for g, (_, dil) in enumerate(DIL_GROUPS):
            if dil == 1:
                continue
            rows = PERM_TILE // dil
            for r in range(dil):
                for c in range(nc):
                    h_ref[g, r * rows:(r + 1) * rows, c * LANES:(c + 1) * LANES] = (
                        hf_ref[c, pl.ds(r, rows, stride=dil), :].astype(BF16))

    is_dil = j < n_dil_tiles
    group = jnp.where(is_dil, j // 3, 0)
    comp = j % 3
    is_qk = is_dil & (comp < 2)

    @pl.when(is_qk)
    def _():
        nw = nw_ref[0]
        scale = jnp.where(comp == 0, DIL_HEAD_DIM ** -0.5 * LOG2_E, 1.0).astype(F32)
        for c in range(PERM_TILE // QK_ROWS):
            rs = slice(c * QK_ROWS, (c + 1) * QK_ROWS)
            acc = jnp.dot(h_ref[group, rs, :], w_ref[...], preferred_element_type=F32)
            cos = cos_ref[0, rs, :]
            sin = sin_ref[0, rs, :]
            for h in range(wide // LANES):
                hs = slice(h * LANES, (h + 1) * LANES)
                o_ref[rs, hs] = (_rope(_rms(acc[:, hs], nw), cos, sin) * scale).astype(o_ref.dtype)

    @pl.when(jnp.logical_not(is_qk))
    def _():
        o_ref[...] = jnp.dot(h_ref[group], w_ref[...], preferred_element_type=F32).astype(o_ref.dtype)


def _in_proj(x, p, tables, seq):
    m, k = x.shape
    tn = DIL_HEADS * DIL_HEAD_DIM
    ng = len(DIL_GROUPS)
    per_seq = seq // PERM_TILE
    cos, sin = tables
    group_of = lambda j: jnp.where(j < 3 * ng, j // 3, 0)
    table_spec = pl.BlockSpec((1, PERM_TILE, LANES), lambda i, j: (group_of(j), i % per_seq, 0))
    return pl.pallas_call(
        _in_proj_kernel,
        grid=(m // PERM_TILE, IN_COLS_PAD // tn),
        in_specs=[pl.BlockSpec((PERM_TILE, k), lambda i, j: (i, 0)),
                  pl.BlockSpec((1, k), lambda i, j: (0, 0)),
                  pl.BlockSpec((k, tn), lambda i, j: (0, j)),
                  pl.BlockSpec((1, 1, LANES), lambda i, j: (jnp.minimum(j % 3, 1), 0, 0)),
                  table_spec, table_spec],
        out_specs=pl.BlockSpec((PERM_TILE, tn), lambda i, j: (i, j)),
        out_shape=jax.ShapeDtypeStruct((m, IN_COLS_PAD), BF16),
        scratch_shapes=[pltpu.VMEM((ng, PERM_TILE, k), BF16),
                        pltpu.VMEM((k // LANES, PERM_TILE, LANES), F32)],
        compiler_params=_cparams("parallel", "arbitrary"),
        name="in_proj",
    )(x, p["norm_mix"], p["w_in"], p["dil_qk_norm"], cos, sin)


TN_DIMS = (((0,), (0,)), ((), ()))


def _gla_kernel(*refs, tb, reverse, final):
    if final:
        q_ref, k_ref, v_ref, sm_ref, wg_ref, bg_ref, of_ref, r_ref, gn_ref, o_ref, st_ref = refs
    else:
        q_ref, k_ref, v_ref, sm_ref, wg_ref, bg_ref, o_ref, st_ref = refs

    @pl.when(pl.program_id(1) == 0)
    def _():
        st_ref[...] = jnp.zeros_like(st_ref)

    ck = GLA_CHUNK
    la = _log_sigmoid(jnp.dot(sm_ref[...], wg_ref[...], preferred_element_type=F32) + bg_ref[...]) * (1.0 / GLA_TAU)

    r = lax.broadcasted_iota(jnp.int32, (ck, ck), 0)
    c = lax.broadcasted_iota(jnp.int32, (ck, ck), 1)
    tri_mask = (r <= c) if reverse else (r >= c)
    tri = jnp.where(tri_mask, 1.0, 0.0).astype(BF16)
    end = 0 if reverse else ck - 1
    scale = GLA_DK ** -0.5

    n_chunks = tb // ck
    order = range(n_chunks - 1, -1, -1) if reverse else range(n_chunks)
    for ci in order:
        sl = slice(ci * ck, (ci + 1) * ck)
        la_c = la[sl]
        hi = la_c.astype(BF16)
        lo = (la_c - hi.astype(F32)).astype(BF16)
        b = jnp.dot(tri, hi, preferred_element_type=F32) + jnp.dot(tri, lo, preferred_element_type=F32)
        b_end = b[end:end + 1, :]
        kc = k_ref[sl, :].astype(F32)
        qd = (q_ref[sl, :].astype(F32) * scale * jnp.exp(b)).astype(BF16)
        kinv = (kc * jnp.exp(-b)).astype(BF16)
        ktail = (kc * jnp.exp(b_end - b)).astype(BF16)
        dec = jnp.exp(b_end)
        for h in range(GLA_HEADS):
            ks = slice(h * GLA_DK, (h + 1) * GLA_DK)
            vs = slice(h * GLA_DV, (h + 1) * GLA_DV)
            qh = qd[:, ks]
            att = lax.dot_general(qh, kinv[:, ks], NT_DIMS, preferred_element_type=F32)
            att = jnp.where(tri_mask, att, 0.0).astype(BF16)
            vh = v_ref[sl, vs]
            st = st_ref[h]
            o = (jnp.dot(att, vh, preferred_element_type=F32)
                 + lax.dot_general(qh, st.astype(BF16), NT_DIMS, preferred_element_type=F32))
            st_ref[h] = dec[:, ks] * st + lax.dot_general(vh, ktail[:, ks], TN_DIMS, preferred_element_type=F32)
            if final:
                o = _rms(o + of_ref[sl, vs], gn_ref[...])
                rr = r_ref[sl, vs].astype(F32)
                o = o * (rr * _sigmoid(rr))
            o_ref[sl, vs] = o.astype(o_ref.dtype)


def _gla_direction(proj, wg, bg, batch, seq, reverse, o_fwd=None, g_out=None, tb=256):
    nb = seq // tb
    m = batch * seq
    final = o_fwd is not None

    def row(b, i):
        return b * nb + (nb - 1 - i if reverse else i)

    def col(block_cols, off):
        return lambda b, i: (row(b, i), off // block_cols)

    hk = GLA_HEADS * GLA_DK
    hv = GLA_HEADS * GLA_DV
    const = lambda b, i: (0, 0)
    in_specs = [pl.BlockSpec((tb, hk), col(hk, COL_AQ)),
                pl.BlockSpec((tb, hk), col(hk, COL_AK)),
                pl.BlockSpec((tb, hv), col(hv, COL_AV)),
                pl.BlockSpec((tb, LANES), col(LANES, COL_SMALL)),
                pl.BlockSpec((LANES, hk), const),
                pl.BlockSpec((1, hk), const)]
    args = [proj, proj, proj, proj, wg, bg]
    if final:
        in_specs += [pl.BlockSpec((tb, hv), col(hv, 0)),
                     pl.BlockSpec((tb, hv), col(hv, COL_AR)),
                     pl.BlockSpec((1, GLA_DV), const)]
        args += [o_fwd, proj, g_out]
    return pl.pallas_call(
        functools.partial(_gla_kernel, tb=tb, reverse=reverse, final=final),
        grid=(batch, nb),
        in_specs=in_specs,
        out_specs=pl.BlockSpec((tb, hv), col(hv, 0)),
        out_shape=jax.ShapeDtypeStruct((m, hv), BF16 if final else F32),
        scratch_shapes=[pltpu.VMEM((GLA_HEADS, GLA_DV, GLA_DK), F32)],
        compiler_params=_cparams("parallel", "arbitrary"),
        name="gla_bwd" if reverse else "gla_fwd",
    )(*args)


def _rope(y, cos, sin_signed):
    return y * cos + pltpu.roll(y, LANES // 2, 1) * sin_signed


def _mla_prep_kernel(bq_ref, bkv_ref, sm_ref, qln_ref, wuq_ref, kvln_ref, wuk_ref, wuv_ref,
                     qn_ref, kn_ref, cos_ref, sin_ref, q_out, k_out, v_out):
    hq = _rms(bq_ref[...].astype(F32), qln_ref[...]).astype(BF16)
    hkv = _rms(bkv_ref[...].astype(F32), kvln_ref[...]).astype(BF16)
    q = jnp.dot(hq, wuq_ref[...], preferred_element_type=F32)
    kn = jnp.dot(hkv, wuk_ref[...], preferred_element_type=F32)
    v = jnp.dot(hkv, wuv_ref[...], preferred_element_type=F32)
    tm = q.shape[0]
    lane = lax.broadcasted_iota(jnp.int32, (tm, LANES), 1)
    sm = sm_ref[...].astype(F32)
    in_rope = (lane < ROPE_LO + ROPE_HALF) | ((lane >= ROPE_HI) & (lane < ROPE_HI + ROPE_HALF))
    kpe = jnp.where(in_rope, sm, 0.0)
    cos = cos_ref[...]
    sin = sin_ref[...]
    scale = MLA_QK ** -0.5 * LOG2_E
    ones = jnp.ones((tm, LANES), v_out.dtype)
    for h in range(MLA_HEADS):
        hs = slice(h * LANES, (h + 1) * LANES)
        qh = _rope(_rms_head_mxu(q[:, hs], qn_ref[...], MLA_QK), cos, sin)
        q_out[:, hs] = (qh * scale).astype(q_out.dtype)
        kh = _rope(_rms_head_mxu(kn[:, hs] + kpe, kn_ref[...], MLA_QK), cos, sin)
        k_out[:, hs] = kh.astype(k_out.dtype)
        v_out[:, 2 * h * LANES:(2 * h + 1) * LANES] = v[:, hs].astype(v_out.dtype)
        v_out[:, (2 * h + 1) * LANES:(2 * h + 2) * LANES] = ones


def _mla_prep(proj, p, cos, sin, seq, tm=512):
    m = proj.shape[0]
    ns = seq // tm
    const = lambda i: (0, 0)
    wide = MLA_HEADS * LANES
    out = jax.ShapeDtypeStruct((m, wide), BF16)
    return pl.pallas_call(
        _mla_prep_kernel,
        grid=(m // tm,),
        in_specs=[pl.BlockSpec((tm, MLA_Q_RANK), lambda i: (i, COL_BQ // MLA_Q_RANK)),
                  pl.BlockSpec((tm, MLA_KV_RANK), lambda i: (i, COL_BKV // MLA_KV_RANK)),
                  pl.BlockSpec((tm, LANES), lambda i: (i, COL_SMALL // LANES)),
                  pl.BlockSpec((1, MLA_Q_RANK), const),
                  pl.BlockSpec((MLA_Q_RANK, wide), const),
                  pl.BlockSpec((1, MLA_KV_RANK), const),
                  pl.BlockSpec((MLA_KV_RANK, wide), const),
                  pl.BlockSpec((MLA_KV_RANK, wide), const),
                  pl.BlockSpec((1, LANES), const),
                  pl.BlockSpec((1, LANES), const),
                  pl.BlockSpec((tm, LANES), lambda i: (i % ns, 0)),
                  pl.BlockSpec((tm, LANES), lambda i: (i % ns, 0))],
        out_specs=[pl.BlockSpec((tm, wide), lambda i: (i, 0)), pl.BlockSpec((tm, wide), lambda i: (i, 0)),
                   pl.BlockSpec((tm, 2 * wide), lambda i: (i, 0))],
        out_shape=[out, out, jax.ShapeDtypeStruct((m, 2 * wide), BF16)],
        compiler_params=_cparams("parallel"),
        name="mla_prep",
    )(proj, proj, proj, p["mla_q_lat_norm"], p["mla_w_uq"], p["mla_kv_lat_norm"], p["mla_w_uk"],
      p["mla_w_uv"], p["mla_q_norm"], p["mla_k_norm"], cos, sin)


def _mla_attn_kernel(q_ref, k_ref, v_ref, o_ref, *, tk, unroll, sub):
    nsub = q_ref.shape[0] // sub
    nk = k_ref.shape[0] // tk
    qs = [q_ref[i * sub:(i + 1) * sub, :] for i in range(nsub)]

    def body(t, carry):
        start = pl.multiple_of(t * tk, tk)
        out = []
        for q, (m, acc) in zip(qs, carry):
            s = lax.dot_general(q, k_ref[pl.ds(start, tk), :], NT_DIMS, preferred_element_type=F32)
            m_new = jnp.maximum(m, jnp.max(s, axis=-1, keepdims=True))
            p = jnp.exp2(s - m_new).astype(BF16)
            acc = jnp.exp2(m - m_new) * acc + jnp.dot(p, v_ref[pl.ds(start, tk), :], preferred_element_type=F32)
            out.append((m_new, acc))
        return tuple(out)

    init = tuple((jnp.full((sub, 1), -jnp.inf, F32), jnp.zeros((sub, 2 * LANES), F32)) for _ in range(nsub))
    final = lax.fori_loop(0, nk, body, init, unroll=unroll)
    for i, (_, acc) in enumerate(final):
        o_ref[i * sub:(i + 1) * sub, :] = (acc[:, :LANES] / acc[:, LANES:]).astype(o_ref.dtype)


MLA_KV_TILE = 256
MLA_TILES_IN_FLIGHT = 64
MLA_SUB_ROWS = (256, 512)


def _mla_attn(q, k, v, batch, seq, tk=MLA_KV_TILE):
    unroll = min(MLA_TILES_IN_FLIGHT, seq // tk)
    tq = min(seq, MLA_SUB_ROWS[0] * (MLA_TILES_IN_FLIGHT // unroll))
    sub = min(tq, MLA_SUB_ROWS[unroll < MLA_TILES_IN_FLIGHT])
    nq = seq // tq
    return pl.pallas_call(
        functools.partial(_mla_attn_kernel, tk=tk, unroll=unroll, sub=sub),
        grid=(batch, MLA_HEADS, nq),
        in_specs=[pl.BlockSpec((tq, LANES), lambda b, h, i: (b * nq + i, h)),
                  pl.BlockSpec((seq, LANES), lambda b, h, i: (b, h)),
                  pl.BlockSpec((seq, 2 * LANES), lambda b, h, i: (b, h))],
        out_specs=pl.BlockSpec((tq, LANES), lambda b, h, i: (b * nq + i, h)),
        out_shape=jax.ShapeDtypeStruct(q.shape, BF16),
        compiler_params=_cparams("parallel", "parallel", "arbitrary"),
        name="mla_attn",
    )(q, k, v)


DIL_SUB_BLOCKS = 2
DIL_MAX_Q_BLOCKS = 4


def _dil_attn_kernel(*refs, sub_len, q_blocks):
    k_blocks = q_blocks + 2
    q_ref = refs[0]
    k_refs = refs[1:1 + k_blocks]
    v_refs = refs[1 + k_blocks:1 + 2 * k_blocks]
    o_ref, lse_ref = refs[1 + 2 * k_blocks:]
    piece_rows = q_ref.shape[1]
    tl = DIL_SUB_BLOCKS * HALO
    nkeys = (DIL_SUB_BLOCKS + 2) * HALO

    def row_runs(start, n):
        runs = []
        while n > 0:
            off = start % piece_rows
            take = min(n, piece_rows - off)
            runs.append((start // piece_rows, slice(off, off + take)))
            start, n = start + take, n - take
        return runs

    row = lax.broadcasted_iota(jnp.int32, (tl, nkeys), 0)
    col = lax.broadcasted_iota(jnp.int32, (tl, nkeys), 1)
    rel = col - row
    in_band = (rel >= 0) & (rel <= 2 * DIL_RADIUS)
    lane = lax.broadcasted_iota(jnp.int32, (tl, LANES), 1)
    for sub in range(q_blocks // DIL_SUB_BLOCKS):
        q_runs = row_runs(sub * tl, tl)
        pos = (pl.program_id(2) * q_blocks + sub * DIL_SUB_BLOCKS - 1) * HALO + col
        valid = in_band & (pos >= 0) & (pos < sub_len)
        kb = range(sub * DIL_SUB_BLOCKS, sub * DIL_SUB_BLOCKS + DIL_SUB_BLOCKS + 2)
        lse_all = jnp.zeros((tl, LANES), F32)
        for h in range(DIL_HEADS):
            hs = slice(h * LANES, (h + 1) * LANES)
            qq = jnp.concatenate([q_ref[p, rs, hs] for p, rs in q_runs], axis=0)
            kk = jnp.concatenate([k_refs[n][:, hs] for n in kb], axis=0)
            vv = jnp.concatenate([v_refs[n][:, hs] for n in kb], axis=0)
            s = lax.dot_general(qq, kk, NT_DIMS, preferred_element_type=F32)
            s = jnp.where(valid, s, -jnp.inf)
            m = jnp.max(s, axis=-1, keepdims=True)
            p = jnp.exp2(s - m)
            l = jnp.sum(p, axis=-1, keepdims=True)
            o = jnp.dot(p.astype(BF16), vv, preferred_element_type=F32) / l
            done = 0
            for pc, rs in q_runs:
                o_ref[pc, rs, hs] = o[done:done + rs.stop - rs.start].astype(o_ref.dtype)
                done += rs.stop - rs.start
            lse_all = jnp.where(lane == h, m + jnp.log2(l), lse_all)
        done = 0
        for pc, rs in q_runs:
            lse_ref[pc, rs, :] = lse_all[done:done + rs.stop - rs.start]
            done += rs.stop - rs.start


def _dil_attn(proj, group, batch, seq):
    _, dil = DIL_GROUPS[group]
    sub_len = seq // dil
    n_blocks = sub_len // HALO
    chunk_blocks = PERM_TILE // dil // HALO
    tile_blocks = PERM_TILE // HALO
    wide = DIL_HEADS * DIL_HEAD_DIM

    def row_block(b, r, n):
        n = jnp.clip(n, 0, n_blocks - 1)
        return (b * (seq // HALO) + (n // chunk_blocks) * tile_blocks + r * chunk_blocks + n % chunk_blocks)

    q_blocks = min(DIL_MAX_Q_BLOCKS, n_blocks)
    k_blocks = q_blocks + 2

    def halo_spec(comp, offset):
        return pl.BlockSpec((HALO, wide), lambda b, r, i: (row_block(b, r, q_blocks * i + offset),
                                                          _dil_tile(comp, group)))

    tl = q_blocks * HALO
    chunk = PERM_TILE // dil
    pieces = max(1, tl // chunk)
    piece_rows = tl // pieces
    subs = chunk // piece_rows
    m = batch * seq
    lead = m // (PERM_TILE * pieces)
    per_seq = seq // (PERM_TILE * pieces)

    def query_view(width):
        return (lead, pieces, dil, chunk, width)

    def query_spec(width, col_block):
        return pl.BlockSpec((None, pieces, None, piece_rows, width),
                            lambda b, r, i: (b * per_seq + i // subs, 0, r, i % subs, col_block))

    o, lse = pl.pallas_call(
        functools.partial(_dil_attn_kernel, sub_len=sub_len, q_blocks=q_blocks),
        grid=(batch, dil, n_blocks // q_blocks),
        in_specs=([query_spec(wide, _dil_tile(0, group))]
                  + [halo_spec(1, n - 1) for n in range(k_blocks)]
                  + [halo_spec(2, n - 1) for n in range(k_blocks)]),
        out_specs=[query_spec(wide, 0), query_spec(LANES, 0)],
        out_shape=[jax.ShapeDtypeStruct(query_view(wide), BF16), jax.ShapeDtypeStruct(query_view(LANES), F32)],
        compiler_params=_cparams("parallel", "parallel", "arbitrary"),
        name=f"dil_attn_g{group}",
    )(proj.reshape(query_view(IN_COLS_PAD)), *([proj] * (2 * k_blocks)))
    return o.reshape(m, wide), lse.reshape(m, LANES)


def _combine_kernel(x_ref, a_ref, b_ref, o0_ref, o1_ref, o2_ref, l0_ref, l1_ref, l2_ref, gate_ref,
                    wa_ref, wb_ref, wc_ref, wo_ref, out_ref, c_ref, on_refs, ln_refs):
    tm = x_ref.shape[0]
    o_tok, l_tok = [], []
    for g, (o_ref, l_ref) in enumerate(((o0_ref, l0_ref), (o1_ref, l1_ref), (o2_ref, l2_ref))):
        dil = DIL_GROUPS[g][1]
        if dil == 1:
            o_tok.append(lambda h, o_ref=o_ref: o_ref[0, 0, :, h * LANES:(h + 1) * LANES].astype(F32))
            l_tok.append(l_ref[0, 0])
            continue
        for r in range(dil):
            rows = pl.ds(r, tm // dil, stride=dil)
            ln_refs[g - 1, rows, :] = l_ref[0, r]
            for h in range(DIL_HEADS):
                on_refs[g - 1, h, rows, :] = o_ref[0, r, :, h * LANES:(h + 1) * LANES].astype(F32)
        o_tok.append(lambda h, g=g: on_refs[g - 1, h])
        l_tok.append(ln_refs[g - 1])
    l0, l1, l2 = l_tok
    mx = jnp.maximum(jnp.maximum(l0, l1), l2)
    e0, e1, e2 = jnp.exp2(l0 - mx), jnp.exp2(l1 - mx), jnp.exp2(l2 - mx)
    inv = 1.0 / (e0 + e1 + e2)
    w0, w1, w2 = e0 * inv, e1 * inv, e2 * inv
    for h in range(DIL_HEADS):
        hs = slice(h * LANES, (h + 1) * LANES)
        c = w0[:, h:h + 1] * o_tok[0](h) + w1[:, h:h + 1] * o_tok[1](h) + w2[:, h:h + 1] * o_tok[2](h)
        c_ref[:, hs] = c.astype(BF16)
    ya = jnp.dot(a_ref[...], wa_ref[...], preferred_element_type=F32)
    yb = jnp.dot(b_ref[...], wb_ref[...], preferred_element_type=F32)
    yc = jnp.dot(c_ref[...], wc_ref[...], preferred_element_type=F32)
    d = D_MODEL
    mix = (_sigmoid(gate_ref[:, 0:d].astype(F32)) * ya
           + _sigmoid(gate_ref[:, d:2 * d].astype(F32)) * yb
           + _sigmoid(gate_ref[:, 2 * d:3 * d].astype(F32)) * yc)
    out_ref[...] = x_ref[...] + jnp.dot(mix.astype(BF16), wo_ref[...], preferred_element_type=F32)


def _combine(x, gla_o, mla_o, dil_o, dil_lse, proj, p, batch, seq, tm=256):
    m = x.shape[0]
    d = D_MODEL
    ns = PERM_TILE // tm
    rowblk = lambda i: (i, 0)
    const = lambda i: (0, 0)
    wspec = pl.BlockSpec((d, d), const)

    def residue_major(arr, group, width):
        dil = DIL_GROUPS[group][1]
        view = arr.reshape(m // PERM_TILE, dil, PERM_TILE // dil, width)
        return view, pl.BlockSpec((1, dil, tm // dil, width), lambda i: (i // ns, 0, i % ns, 0))

    o_views, o_specs = zip(*[residue_major(dil_o[g], g, d) for g in range(len(DIL_GROUPS))])
    l_views, l_specs = zip(*[residue_major(dil_lse[g], g, LANES) for g in range(len(DIL_GROUPS))])
    n_perm = len(DIL_GROUPS) - 1
    return pl.pallas_call(
        _combine_kernel,
        grid=(m // tm,),
        in_specs=[pl.BlockSpec((tm, d), rowblk), pl.BlockSpec((tm, d), rowblk), pl.BlockSpec((tm, d), rowblk),
                  *o_specs, *l_specs,
                  pl.BlockSpec((tm, 3 * d), lambda i: (i, COL_GATE // (3 * d))),
                  wspec, wspec, wspec, wspec],
        out_specs=pl.BlockSpec((tm, d), rowblk),
        out_shape=jax.ShapeDtypeStruct((m, d), F32),
        scratch_shapes=[pltpu.VMEM((tm, d), BF16), pltpu.VMEM((n_perm, DIL_HEADS, tm, LANES), F32),
                        pltpu.VMEM((n_perm, tm, LANES), F32)],
        compiler_params=_cparams("parallel"),
        name="combine",
    )(x, gla_o, mla_o, *o_views, *l_views, proj,
      p["w_branch_a"], p["w_branch_b"], p["w_branch_c"], p["w_out"])


def _xattn_kernel(x_ref, g_ref, wq_ref, qn_ref, kv_ref, kn_ref, wo_ref, out_ref, o_ref):
    x = x_ref[...]
    h = _rms(x, g_ref[...]).astype(BF16)
    q = jnp.dot(h, wq_ref[...], preferred_element_type=F32)
    scale = X_HEAD_DIM ** -0.5 * LOG2_E
    for hd in range(X_HEADS):
        hs = slice(hd * X_HEAD_DIM, (hd + 1) * X_HEAD_DIM)
        vs = slice(D_MODEL + hd * X_HEAD_DIM, D_MODEL + (hd + 1) * X_HEAD_DIM)
        qh = (_rms(q[:, hs], qn_ref[...]) * scale).astype(BF16)
        kh = _rms(kv_ref[:, hs].astype(F32), kn_ref[...]).astype(BF16)
        s = lax.dot_general(qh, kh, NT_DIMS, preferred_element_type=F32)
        p = jnp.exp2(s - jnp.max(s, axis=-1, keepdims=True))
        l = jnp.sum(p, axis=-1, keepdims=True)
        o = jnp.dot(p.astype(BF16), kv_ref[:, vs], preferred_element_type=F32) / l
        o_ref[:, hs] = o.astype(BF16)
    out_ref[...] = x + jnp.dot(o_ref[...], wo_ref[...], preferred_element_type=F32)


def _xattn(x, kv, p, seq, tm=512):
    m = x.shape[0]
    d = D_MODEL
    per_seq = seq // tm
    const = lambda i: (0, 0)
    return pl.pallas_call(
        _xattn_kernel,
        grid=(m // tm,),
        in_specs=[pl.BlockSpec((tm, d), lambda i: (i, 0)),
                  pl.BlockSpec((1, d), const),
                  pl.BlockSpec((d, d), const),
                  pl.BlockSpec((1, X_HEAD_DIM), const),
                  pl.BlockSpec((N_MEM, 2 * d), lambda i: (i // per_seq, 0)),
                  pl.BlockSpec((1, X_HEAD_DIM), const),
                  pl.BlockSpec((d, d), const)],
        out_specs=pl.BlockSpec((tm, d), lambda i: (i, 0)),
        out_shape=jax.ShapeDtypeStruct((m, d), F32),
        scratch_shapes=[pltpu.VMEM((tm, d), BF16)],
        compiler_params=_cparams("parallel"),
        name="xattn",
    )(x, p["norm_xattn"], p["w_xq"], p["xq_norm"], kv, p["xk_norm"], p["w_xo"])


def _ffn_kernel(x_ref, g_ref, w1_ref, w2_ref, out_ref, h_ref, acc_ref):
    j = pl.program_id(1)

    @pl.when(j == 0)
    def _():
        h_ref[...] = _rms(x_ref[...], g_ref[...]).astype(BF16)
        acc_ref[...] = x_ref[...]

    u = jnp.maximum(jnp.dot(h_ref[...], w1_ref[...], preferred_element_type=F32), 0.0)
    acc_ref[...] += jnp.dot((u * u).astype(BF16), w2_ref[...], preferred_element_type=F32)

    @pl.when(j == pl.num_programs(1) - 1)
    def _():
        out_ref[...] = acc_ref[...]


def _ffn(x, p, tm=1024, tf=1024):
    m = x.shape[0]
    d = D_MODEL
    return pl.pallas_call(
        _ffn_kernel,
        grid=(m // tm, D_FF // tf),
        in_specs=[pl.BlockSpec((tm, d), lambda i, j: (i, 0)),
                  pl.BlockSpec((1, d), lambda i, j: (0, 0)),
                  pl.BlockSpec((d, tf), lambda i, j: (0, j)),
                  pl.BlockSpec((tf, d), lambda i, j: (j, 0))],
        out_specs=pl.BlockSpec((tm, d), lambda i, j: (i, 0)),
        out_shape=jax.ShapeDtypeStruct((m, d), F32),
        scratch_shapes=[pltpu.VMEM((tm, d), BF16), pltpu.VMEM((tm, d), F32)],
        compiler_params=_cparams("parallel", "arbitrary"),
        name="ffn",
    )(x, p["norm_ffn"], p["w_ff1"], p["w_ff2"])


def _rope_tables(positions):
    seq = len(positions)
    inv_freq = 1.0 / (ROPE_THETA ** (np.arange(0, 2 * ROPE_HALF, 2, dtype=np.float64) / (2 * ROPE_HALF)))
    ang = np.asarray(positions, np.float64)[:, None] * inv_freq[None, :]
    cos, sin = jnp.asarray(np.cos(ang), F32), jnp.asarray(np.sin(ang), F32)
    gap = ROPE_HI - ROPE_LO - ROPE_HALF
    tail = LANES - ROPE_HI - ROPE_HALF
    fill = lambda n, v: jnp.full((seq, n), v, F32)
    cos_t = jnp.concatenate([fill(ROPE_LO, 1.0), cos, fill(gap, 1.0), cos, fill(tail, 1.0)], axis=1)
    sin_t = jnp.concatenate([fill(ROPE_LO, 0.0), -sin, fill(gap, 0.0), sin, fill(tail, 0.0)], axis=1)
    return cos_t, sin_t


def _lanes_after(*taken):
    used = set(int(i) for t in taken for i in t)
    return [i for i in range(LANES) if i not in used]


_ROPE_LANES = list(range(ROPE_LO, ROPE_LO + ROPE_HALF)) + list(range(ROPE_HI, ROPE_HI + ROPE_HALF))
_MLA_LANE_OF_DIM = np.array(_lanes_after(_ROPE_LANES)[:MLA_NOPE] + _ROPE_LANES, np.int32)
_DIL_LANE_OF_DIM = np.array(_ROPE_LANES + _lanes_after(_ROPE_LANES), np.int32)


def _to_lanes(arr, lane_of_dim):
    dim_of_lane = np.full(LANES, -1)
    dim_of_lane[lane_of_dim] = np.arange(len(lane_of_dim))
    pieces, lane = [], 0
    while lane < LANES:
        run = 1
        while (lane + run < LANES and
               (dim_of_lane[lane + run] == dim_of_lane[lane] + run if dim_of_lane[lane] >= 0
                else dim_of_lane[lane + run] < 0)):
            run += 1
        start = int(dim_of_lane[lane])
        pieces.append(arr[..., start:start + run] if start >= 0
                      else jnp.zeros(arr.shape[:-1] + (run,), arr.dtype))
        lane += run
    return jnp.concatenate(pieces, axis=-1)


def _prep_layer(w):
    d = D_MODEL
    hk = GLA_HEADS * GLA_DK
    hv = GLA_HEADS * GLA_DV
    dil_w = len(DIL_GROUPS) * DIL_HEADS * DIL_HEAD_DIM
    sizes = (hk, hk, hv, hv, GLA_GATE_RANK, GLA_GATE_RANK, MLA_Q_RANK, MLA_KV_RANK, MLA_ROPE,
             dil_w, dil_w, dil_w, 3 * d)
    offs = [0]
    for s in sizes:
        offs.append(offs[-1] + s)
    (a_q, a_k, a_v, a_r, a_gf, a_gb, b_q, b_kv, b_kpe, c_q, c_k, c_v, gates) = [
        w["w_in"][:, offs[i]:offs[i + 1]] for i in range(len(sizes))]
    z = lambda n: jnp.zeros((d, n), F32)
    small_lane_of_dim = np.concatenate([_MLA_LANE_OF_DIM[MLA_NOPE:], SMALL_GF + np.arange(GLA_GATE_RANK),
                                        SMALL_GB + np.arange(GLA_GATE_RANK)])
    small = _to_lanes(jnp.concatenate([b_kpe, a_gf, a_gb], axis=1), small_lane_of_dim)

    def dil_heads(cols):
        per_head = cols.reshape(d, len(DIL_GROUPS) * DIL_HEADS, DIL_HEAD_DIM)
        return _to_lanes(per_head, _DIL_LANE_OF_DIM).reshape(d, dil_w)

    gw = DIL_HEADS * DIL_HEAD_DIM
    dil_tiles = [t[:, g * gw:(g + 1) * gw] for g in range(len(DIL_GROUPS))
                 for t in (dil_heads(c_q), dil_heads(c_k), c_v)]
    w_in = jnp.concatenate(dil_tiles + [gates, a_q, a_k, a_v, a_r, b_q, b_kv, small,
                                        z(IN_COLS_PAD - COL_SMALL - LANES)], axis=1).astype(BF16)

    def gate_w(wg, lane0):
        rows = lambda n: jnp.zeros((n, hk), F32)
        return jnp.concatenate([rows(lane0), wg, rows(LANES - lane0 - GLA_GATE_RANK)], axis=0).astype(BF16)

    wgf = gate_w(w["gla_w_gate_f"], SMALL_GF)
    wgb = gate_w(w["gla_w_gate_b"], SMALL_GB)

    w_uq = _to_lanes(w["mla_w_uq"].reshape(MLA_Q_RANK, MLA_HEADS, MLA_QK), _MLA_LANE_OF_DIM)
    w_uq = w_uq.reshape(MLA_Q_RANK, MLA_HEADS * LANES)
    w_ukv = w["mla_w_ukv"].reshape(MLA_KV_RANK, MLA_HEADS, MLA_NOPE + MLA_V)
    w_uk = _to_lanes(w_ukv[:, :, :MLA_NOPE], _MLA_LANE_OF_DIM[:MLA_NOPE])
    w_uk = w_uk.reshape(MLA_KV_RANK, MLA_HEADS * LANES)
    w_uv = w_ukv[:, :, MLA_NOPE:].reshape(MLA_KV_RANK, MLA_HEADS * MLA_V)
    row = lambda v: v.reshape(1, -1).astype(F32)
    return {
        "norm_mix": row(w["norm_mix"]), "w_in": w_in,
        "gla_wgf": wgf, "gla_bgf": row(w["gla_b_gate_f"]),
        "gla_wgb": wgb, "gla_bgb": row(w["gla_b_gate_b"]),
        "gla_out_norm": row(w["gla_out_norm"]), "w_branch_a": w["w_branch_a"].astype(BF16),
        "mla_q_lat_norm": row(w["mla_q_lat_norm"]), "mla_w_uq": w_uq.astype(BF16),
        "mla_kv_lat_norm": row(w["mla_kv_lat_norm"]), "mla_w_uk": w_uk.astype(BF16),
        "mla_w_uv": w_uv.astype(BF16),
        "mla_q_norm": _to_lanes(w["mla_q_norm"], _MLA_LANE_OF_DIM).reshape(1, LANES),
        "mla_k_norm": _to_lanes(w["mla_k_norm"], _MLA_LANE_OF_DIM).reshape(1, LANES),
        "w_branch_b": w["w_branch_b"].astype(BF16),
        "dil_qk_norm": _to_lanes(jnp.stack([w["dil_q_norm"], w["dil_k_norm"]]),
                                 _DIL_LANE_OF_DIM).reshape(2, 1, LANES),
        "w_branch_c": w["w_branch_c"].astype(BF16), "w_out": w["w_out"].astype(BF16),
        "norm_xattn": row(w["norm_xattn"]), "norm_mem": row(w["norm_mem"]),
        "w_xq": w["w_xq"].astype(BF16), "w_xkv": w["w_xkv"].astype(BF16),
        "xq_norm": row(w["xq_norm"]), "xk_norm": row(w["xk_norm"]), "w_xo": w["w_xo"].astype(BF16),
        "norm_ffn": row(w["norm_ffn"]), "w_ff1": w["w_ff1"].astype(BF16), "w_ff2": w["w_ff2"].astype(BF16),
    }


def _layer(x, mem, p, tables, batch, seq):
    (cos_m, sin_m), dil_tables = tables
    proj = _in_proj(x, p, dil_tables, seq)
    o_fwd = _gla_direction(proj, p["gla_wgf"], p["gla_bgf"], batch, seq, False)
    gla_o = _gla_direction(proj, p["gla_wgb"], p["gla_bgb"], batch, seq, True,
                           o_fwd=o_fwd, g_out=p["gla_out_norm"])
    q_m, k_m, v_m = _mla_prep(proj, p, cos_m, sin_m, seq)
    mla_o = _mla_attn(q_m, k_m, v_m, batch, seq)
    dil = [_dil_attn(proj, g, batch, seq) for g in range(len(DIL_GROUPS))]
    x = _combine(x, gla_o, mla_o, [o for o, _ in dil], [l for _, l in dil], proj, p, batch, seq)
    kv = _norm_matmul(mem, p["norm_mem"], p["w_xkv"], BF16, tm=N_MEM, tn=1024)
    x = _xattn(x, kv, p, seq)
    return _ffn(x, p)


def _trunk(x, mem, layers):
    batch, seq, d = x.shape
    natural = np.arange(seq)
    row_pos = [natural.reshape(seq // PERM_TILE, PERM_TILE // dil, dil).transpose(0, 2, 1).reshape(seq)
               for _, dil in DIL_GROUPS]
    dil_cos, dil_sin = zip(*[_rope_tables(pos) for pos in row_pos])
    tables = (_rope_tables(natural), (jnp.stack(dil_cos), jnp.stack(dil_sin)))
    xf = x.reshape(batch * seq, d)
    memf = mem.reshape(batch * mem.shape[1], d)
    for p in layers:
        xf = _layer(xf, memf, p, tables, batch, seq)
    return xf.reshape(batch, seq, d)


def kernel(x_prompt, x_sample, mem_prompt, mem_sample, norm_mix, w_in, gla_w_gate_f, gla_b_gate_f, gla_w_gate_b, gla_b_gate_b, gla_out_norm, w_branch_a, mla_q_lat_norm, mla_w_uq, mla_kv_lat_norm, mla_w_ukv, mla_q_norm, mla_k_norm, w_branch_b, dil_q_norm, dil_k_norm, w_branch_c, w_out, norm_xattn, norm_mem, w_xq, w_xkv, xq_norm, xk_norm, w_xo, norm_ffn, w_ff1, w_ff2):
    stacked = dict(norm_mix=norm_mix, w_in=w_in, gla_w_gate_f=gla_w_gate_f, gla_b_gate_f=gla_b_gate_f,
                   gla_w_gate_b=gla_w_gate_b, gla_b_gate_b=gla_b_gate_b, gla_out_norm=gla_out_norm,
                   w_branch_a=w_branch_a, mla_q_lat_norm=mla_q_lat_norm, mla_w_uq=mla_w_uq,
                   mla_kv_lat_norm=mla_kv_lat_norm, mla_w_ukv=mla_w_ukv, mla_q_norm=mla_q_norm,
                   mla_k_norm=mla_k_norm, w_branch_b=w_branch_b, dil_q_norm=dil_q_norm, dil_k_norm=dil_k_norm,
                   w_branch_c=w_branch_c, w_out=w_out, norm_xattn=norm_xattn, norm_mem=norm_mem, w_xq=w_xq,
                   w_xkv=w_xkv, xq_norm=xq_norm, xk_norm=xk_norm, w_xo=w_xo, norm_ffn=norm_ffn,
                   w_ff1=w_ff1, w_ff2=w_ff2)
    layers = [_prep_layer({k: v[l] for k, v in stacked.items()}) for l in range(DEPTH)]
    return (_trunk(x_prompt, mem_prompt, layers), _trunk(x_sample, mem_sample, layers))
```

```python
import functools

import jax
import jax.numpy as jnp
import numpy as np
from jax import lax
from jax.experimental import pallas as pl
from jax.experimental.pallas import tpu as pltpu

F32 = jnp.float32
BF16 = jnp.bfloat16

D_MODEL = 1024
DEPTH = 2
N_MEM = 256
ROPE_THETA = 500000.0
NORM_EPS = 1e-6
GLA_HEADS = 4
GLA_DK = 128
GLA_DV = 256
GLA_GATE_RANK = 16
GLA_TAU = 16.0
GLA_CHUNK = 64
MLA_HEADS = 8
MLA_Q_RANK = 256
MLA_KV_RANK = 128
MLA_NOPE = 64
MLA_ROPE = 32
MLA_QK = 96
MLA_V = 128
DIL_GROUPS = ((128, 1), (512, 4), (2048, 16))
DIL_HEADS = 8
DIL_HEAD_DIM = 128
DIL_ROT = 32
DIL_RADIUS = 64
X_HEADS = 4
X_HEAD_DIM = 256
D_FF = 4096

LANES = 128
ROPE_HALF = 16
VMEM_LIMIT_BYTES = 56 * 1024 * 1024

COL_GATE = 9216
PERM_TILE = 1024
HALO = 64
QK_ROWS = 256
COL_AQ, COL_AK, COL_AV, COL_AR = 12288, 12800, 13312, 14336
COL_BQ, COL_BKV, COL_SMALL = 15360, 15616, 15744
IN_COLS_PAD = 16384
SMALL_GF, SMALL_GB = 16, 32
ROPE_LO, ROPE_HI = 0, 64

NT_DIMS = (((1,), (1,)), ((), ()))
LOG2_E = 1.4426950408889634


def _cparams(*sem):
    return pltpu.CompilerParams(dimension_semantics=sem, vmem_limit_bytes=VMEM_LIMIT_BYTES)


def _rms(x, g, n=None):
    ss = jnp.sum(x * x, axis=-1, keepdims=True) * (1.0 / (n or x.shape[-1]))
    return x * lax.rsqrt(ss + NORM_EPS) * g


def _rms_head_mxu(x, g, n):
    ones = jnp.ones((LANES, LANES), BF16)
    ss = jnp.dot((x * x).astype(BF16), ones, preferred_element_type=F32)
    return x * lax.rsqrt(ss * (1.0 / n) + NORM_EPS) * g


def _log_sigmoid(z):
    return jnp.minimum(z, 0.0) - jnp.log(1.0 + jnp.exp(-jnp.abs(z)))


def _sigmoid(z):
    return 1.0 / (1.0 + jnp.exp(-z))


def _norm_matmul_kernel(x_ref, g_ref, w_ref, o_ref, h_ref):
    @pl.when(pl.program_id(1) == 0)
    def _():
        h_ref[...] = _rms(x_ref[...].astype(F32), g_ref[...]).astype(BF16)

    o_ref[...] = jnp.dot(h_ref[...], w_ref[...], preferred_element_type=F32).astype(o_ref.dtype)


def _norm_matmul(x, g, w, out_dtype, tm, tn):
    m, k = x.shape
    n = w.shape[1]
    return pl.pallas_call(
        _norm_matmul_kernel,
        grid=(m // tm, n // tn),
        in_specs=[pl.BlockSpec((tm, k), lambda i, j: (i, 0)),
                  pl.BlockSpec((1, k), lambda i, j: (0, 0)),
                  pl.BlockSpec((k, tn), lambda i, j: (0, j))],
        out_specs=pl.BlockSpec((tm, tn), lambda i, j: (i, j)),
        out_shape=jax.ShapeDtypeStruct((m, n), out_dtype),
        scratch_shapes=[pltpu.VMEM((tm, k), BF16)],
        compiler_params=_cparams("parallel", "arbitrary"),
        name="norm_matmul",
    )(x, g, w)


def _dil_tile(comp, group):
    return len(DIL_GROUPS) * group + comp


def _in_proj_kernel(x_ref, g_ref, w_ref, nw_ref, cos_ref, sin_ref, o_ref, h_ref, hf_ref):
    j = pl.program_id(1)
    n_dil_tiles = 3 * len(DIL_GROUPS)
    wide = DIL_HEADS * DIL_HEAD_DIM
    nc = D_MODEL // LANES

    @pl.when(j == 0)
    def _():
        h = _rms(x_ref[...], g_ref[...])
        h_ref[0] = h.astype(BF16)
        for c in range(nc):
            hf_ref[c] = h[:, c * LANES:(c + 1) * LANES]
        for g, (_, dil) in enumerate(DIL_GROUPS):
            if dil == 1:
                continue
            rows = PERM_TILE // dil
            for r in range(dil):
                for c in range(nc):
                    h_ref[g, r * rows:(r + 1) * rows, c * LANES:(c + 1) * LANES] = (
                        hf_ref[c, pl.ds(r, rows, stride=dil), :].astype(BF16))

    is_dil = j < n_dil_tiles
    group = jnp.where(is_dil, j // 3, 0)
    comp = j % 3
    is_qk = is_dil & (comp < 2)

    @pl.when(is_qk)
    def _():
        nw = nw_ref[0]
        scale = jnp.where(comp == 0, DIL_HEAD_DIM ** -0.5 * LOG2_E, 1.0).astype(F32)
        for c in range(PERM_TILE // QK_ROWS):
            rs = slice(c * QK_ROWS, (c + 1) * QK_ROWS)
            acc = jnp.dot(h_ref[group, rs, :], w_ref[...], preferred_element_type=F32)
            cos = cos_ref[0, rs, :]
            sin = sin_ref[0, rs, :]
            for h in range(wide // LANES):
                hs = slice(h * LANES, (h + 1) * LANES)
                o_ref[rs, hs] = (_rope(_rms(acc[:, hs], nw), cos, sin) * scale).astype(o_ref.dtype)

    @pl.when(jnp.logical_not(is_qk))
    def _():
        o_ref[...] = jnp.dot(h_ref[group], w_ref[...], preferred_element_type=F32).astype(o_ref.dtype)


def _in_proj(x, p, tables, seq):
    m, k = x.shape
    tn = DIL_HEADS * DIL_HEAD_DIM
    ng = len(DIL_GROUPS)
    per_seq = seq // PERM_TILE
    cos, sin = tables
    group_of = lambda j: jnp.where(j < 3 * ng, j // 3, 0)
    table_spec = pl.BlockSpec((1, PERM_TILE, LANES), lambda i, j: (group_of(j), i % per_seq, 0))
    return pl.pallas_call(
        _in_proj_kernel,
        grid=(m // PERM_TILE, IN_COLS_PAD // tn),
        in_specs=[pl.BlockSpec((PERM_TILE, k), lambda i, j: (i, 0)),
                  pl.BlockSpec((1, k), lambda i, j: (0, 0)),
                  pl.BlockSpec((k, tn), lambda i, j: (0, j)),
                  pl.BlockSpec((1, 1, LANES), lambda i, j: (jnp.minimum(j % 3, 1), 0, 0)),
                  table_spec, table_spec],
        out_specs=pl.BlockSpec((PERM_TILE, tn), lambda i, j: (i, j)),
        out_shape=jax.ShapeDtypeStruct((m, IN_COLS_PAD), BF16),
        scratch_shapes=[pltpu.VMEM((ng, PERM_TILE, k), BF16),
                        pltpu.VMEM((k // LANES, PERM_TILE, LANES), F32)],
        compiler_params=_cparams("parallel", "arbitrary"),
        name="in_proj",
    )(x, p["norm_mix"], p["w_in"], p["dil_qk_norm"], cos, sin)


TN_DIMS = (((0,), (0,)), ((), ()))


def _gla_kernel(*refs, tb, reverse, final):
    if final:
        q_ref, k_ref, v_ref, sm_ref, wg_ref, bg_ref, of_ref, r_ref, gn_ref, o_ref, st_ref = refs
    else:
        q_ref, k_ref, v_ref, sm_ref, wg_ref, bg_ref, o_ref, st_ref = refs

    @pl.when(pl.program_id(1) == 0)
    def _():
        st_ref[...] = jnp.zeros_like(st_ref)

    ck = GLA_CHUNK
    la = _log_sigmoid(jnp.dot(sm_ref[...], wg_ref[...], preferred_element_type=F32) + bg_ref[...]) * (1.0 / GLA_TAU)

    r = lax.broadcasted_iota(jnp.int32, (ck, ck), 0)
    c = lax.broadcasted_iota(jnp.int32, (ck, ck), 1)
    tri_mask = (r <= c) if reverse else (r >= c)
    tri = jnp.where(tri_mask, 1.0, 0.0).astype(BF16)
    end = 0 if reverse else ck - 1
    scale = GLA_DK ** -0.5

    n_chunks = tb // ck
    order = range(n_chunks - 1, -1, -1) if reverse else range(n_chunks)
    for ci in order:
        sl = slice(ci * ck, (ci + 1) * ck)
        la_c = la[sl]
        hi = la_c.astype(BF16)
        lo = (la_c - hi.astype(F32)).astype(BF16)
        b = jnp.dot(tri, hi, preferred_element_type=F32) + jnp.dot(tri, lo, preferred_element_type=F32)
        b_end = b[end:end + 1, :]
        kc = k_ref[sl, :].astype(F32)
        qd = (q_ref[sl, :].astype(F32) * scale * jnp.exp(b)).astype(BF16)
        kinv = (kc * jnp.exp(-b)).astype(BF16)
        ktail = (kc * jnp.exp(b_end - b)).astype(BF16)
        dec = jnp.exp(b_end)
        for h in range(GLA_HEADS):
            ks = slice(h * GLA_DK, (h + 1) * GLA_DK)
            vs = slice(h * GLA_DV, (h + 1) * GLA_DV)
            qh = qd[:, ks]
            att = lax.dot_general(qh, kinv[:, ks], NT_DIMS, preferred_element_type=F32)
            att = jnp.where(tri_mask, att, 0.0).astype(BF16)
            vh = v_ref[sl, vs]
            st = st_ref[h]
            o = (jnp.dot(att, vh, preferred_element_type=F32)
                 + lax.dot_general(qh, st.astype(BF16), NT_DIMS, preferred_element_type=F32))
            st_ref[h] = dec[:, ks] * st + lax.dot_general(vh, ktail[:, ks], TN_DIMS, preferred_element_type=F32)
            if final:
                o = _rms(o + of_ref[sl, vs], gn_ref[...])
                rr = r_ref[sl, vs].astype(F32)
                o = o * (rr * _sigmoid(rr))
            o_ref[sl, vs] = o.astype(o_ref.dtype)


def _gla_direction(proj, wg, bg, batch, seq, reverse, o_fwd=None, g_out=None, tb=256):
    nb = seq // tb
    m = batch * seq
    final = o_fwd is not None

    def row(b, i):
        return b * nb + (nb - 1 - i if reverse else i)

    def col(block_cols, off):
        return lambda b, i: (row(b, i), off // block_cols)

    hk = GLA_HEADS * GLA_DK
    hv = GLA_HEADS * GLA_DV
    const = lambda b, i: (0, 0)
    in_specs = [pl.BlockSpec((tb, hk), col(hk, COL_AQ)),
                pl.BlockSpec((tb, hk), col(hk, COL_AK)),
                pl.BlockSpec((tb, hv), col(hv, COL_AV)),
                pl.BlockSpec((tb, LANES), col(LANES, COL_SMALL)),
                pl.BlockSpec((LANES, hk), const),
                pl.BlockSpec((1, hk), const)]
    args = [proj, proj, proj, proj, wg, bg]
    if final:
        in_specs += [pl.BlockSpec((tb, hv), col(hv, 0)),
                     pl.BlockSpec((tb, hv), col(hv, COL_AR)),
                     pl.BlockSpec((1, GLA_DV), const)]
        args += [o_fwd, proj, g_out]
    return pl.pallas_call(
        functools.partial(_gla_kernel, tb=tb, reverse=reverse, final=final),
        grid=(batch, nb),
        in_specs=in_specs,
        out_specs=pl.BlockSpec((tb, hv), col(hv, 0)),
        out_shape=jax.ShapeDtypeStruct((m, hv), BF16 if final else F32),
        scratch_shapes=[pltpu.VMEM((GLA_HEADS, GLA_DV, GLA_DK), F32)],
        compiler_params=_cparams("parallel", "arbitrary"),
        name="gla_bwd" if reverse else "gla_fwd",
    )(*args)


def _rope(y, cos, sin_signed):
    return y * cos + pltpu.roll(y, LANES // 2, 1) * sin_signed


def _mla_prep_kernel(bq_ref, bkv_ref, sm_ref, qln_ref, wuq_ref, kvln_ref, wuk_ref, wuv_ref,
                     qn_ref, kn_ref, cos_ref, sin_ref, q_out, k_out, v_out):
    hq = _rms(bq_ref[...].astype(F32), qln_ref[...]).astype(BF16)
    hkv = _rms(bkv_ref[...].astype(F32), kvln_ref[...]).astype(BF16)
    q = jnp.dot(hq, wuq_ref[...], preferred_element_type=F32)
    kn = jnp.dot(hkv, wuk_ref[...], preferred_element_type=F32)
    v = jnp.dot(hkv, wuv_ref[...], preferred_element_type=F32)
    tm = q.shape[0]
    lane = lax.broadcasted_iota(jnp.int32, (tm, LANES), 1)
    sm = sm_ref[...].astype(F32)
    in_rope = (lane < ROPE_LO + ROPE_HALF) | ((lane >= ROPE_HI) & (lane < ROPE_HI + ROPE_HALF))
    kpe = jnp.where(in_rope, sm, 0.0)
    cos = cos_ref[...]
    sin = sin_ref[...]
    scale = MLA_QK ** -0.5 * LOG2_E
    ones = jnp.ones((tm, LANES), v_out.dtype)
    for h in range(MLA_HEADS):
        hs = slice(h * LANES, (h + 1) * LANES)
        qh = _rope(_rms_head_mxu(q[:, hs], qn_ref[...], MLA_QK), cos, sin)
        q_out[:, hs] = (qh * scale).astype(q_out.dtype)
        kh = _rope(_rms_head_mxu(kn[:, hs] + kpe, kn_ref[...], MLA_QK), cos, sin)
        k_out[:, hs] = kh.astype(k_out.dtype)
        v_out[:, 2 * h * LANES:(2 * h + 1) * LANES] = v[:, hs].astype(v_out.dtype)
        v_out[:, (2 * h + 1) * LANES:(2 * h + 2) * LANES] = ones


def _mla_prep(proj, p, cos, sin, seq, tm=512):
    m = proj.shape[0]
    ns = seq // tm
    const = lambda i: (0, 0)
    wide = MLA_HEADS * LANES
    out = jax.ShapeDtypeStruct((m, wide), BF16)
    return pl.pallas_call(
        _mla_prep_kernel,
        grid=(m // tm,),
        in_specs=[pl.BlockSpec((tm, MLA_Q_RANK), lambda i: (i, COL_BQ // MLA_Q_RANK)),
                  pl.BlockSpec((tm, MLA_KV_RANK), lambda i: (i, COL_BKV // MLA_KV_RANK)),
                  pl.BlockSpec((tm, LANES), lambda i: (i, COL_SMALL // LANES)),
                  pl.BlockSpec((1, MLA_Q_RANK), const),
                  pl.BlockSpec((MLA_Q_RANK, wide), const),
                  pl.BlockSpec((1, MLA_KV_RANK), const),
                  pl.BlockSpec((MLA_KV_RANK, wide), const),
                  pl.BlockSpec((MLA_KV_RANK, wide), const),
                  pl.BlockSpec((1, LANES), const),
                  pl.BlockSpec((1, LANES), const),
                  pl.BlockSpec((tm, LANES), lambda i: (i % ns, 0)),
                  pl.BlockSpec((tm, LANES), lambda i: (i % ns, 0))],
        out_specs=[pl.BlockSpec((tm, wide), lambda i: (i, 0)), pl.BlockSpec((tm, wide), lambda i: (i, 0)),
                   pl.BlockSpec((tm, 2 * wide), lambda i: (i, 0))],
        out_shape=[out, out, jax.ShapeDtypeStruct((m, 2 * wide), BF16)],
        compiler_params=_cparams("parallel"),
        name="mla_prep",
    )(proj, proj, proj, p["mla_q_lat_norm"], p["mla_w_uq"], p["mla_kv_lat_norm"], p["mla_w_uk"],
      p["mla_w_uv"], p["mla_q_norm"], p["mla_k_norm"], cos, sin)


def _mla_attn_kernel(q_ref, k_ref, v_ref, o_ref, *, tk, unroll, sub):
    nsub = q_ref.shape[0] // sub
    nk = k_ref.shape[0] // tk
    qs = [q_ref[i * sub:(i + 1) * sub, :] for i in range(nsub)]

    def body(t, carry):
        start = pl.multiple_of(t * tk, tk)
        out = []
        for q, (m, acc) in zip(qs, carry):
            s = lax.dot_general(q, k_ref[pl.ds(start, tk), :], NT_DIMS, preferred_element_type=F32)
            m_new = jnp.maximum(m, jnp.max(s, axis=-1, keepdims=True))
            p = jnp.exp2(s - m_new).astype(BF16)
            acc = jnp.exp2(m - m_new) * acc + jnp.dot(p, v_ref[pl.ds(start, tk), :], preferred_element_type=F32)
            out.append((m_new, acc))
        return tuple(out)

    init = tuple((jnp.full((sub, 1), -jnp.inf, F32), jnp.zeros((sub, 2 * LANES), F32)) for _ in range(nsub))
    final = lax.fori_loop(0, nk, body, init, unroll=unroll)
    for i, (_, acc) in enumerate(final):
        o_ref[i * sub:(i + 1) * sub, :] = (acc[:, :LANES] / acc[:, LANES:]).astype(o_ref.dtype)


MLA_KV_TILE = 256
MLA_TILES_IN_FLIGHT = 64
MLA_SUB_ROWS = (256, 512)


def _mla_attn(q, k, v, batch, seq, tk=MLA_KV_TILE):
    unroll = min(MLA_TILES_IN_FLIGHT, seq // tk)
    tq = min(seq, MLA_SUB_ROWS[0] * (MLA_TILES_IN_FLIGHT // unroll))
    sub = min(tq, MLA_SUB_ROWS[unroll < MLA_TILES_IN_FLIGHT])
    nq = seq // tq
    return pl.pallas_call(
        functools.partial(_mla_attn_kernel, tk=tk, unroll=unroll, sub=sub),
        grid=(batch, MLA_HEADS, nq),
        in_specs=[pl.BlockSpec((tq, LANES), lambda b, h, i: (b * nq + i, h)),
                  pl.BlockSpec((seq, LANES), lambda b, h, i: (b, h)),
                  pl.BlockSpec((seq, 2 * LANES), lambda b, h, i: (b, h))],
        out_specs=pl.BlockSpec((tq, LANES), lambda b, h, i: (b * nq + i, h)),
        out_shape=jax.ShapeDtypeStruct(q.shape, BF16),
        compiler_params=_cparams("parallel", "parallel", "arbitrary"),
        name="mla_attn",
    )(q, k, v)


DIL_SUB_BLOCKS = 2
DIL_MAX_Q_BLOCKS = 8


def _dil_attn_kernel(*refs, sub_len, q_blocks):
    k_blocks = q_blocks + 2
    q_ref = refs[0]
    k_refs = refs[1:1 + k_blocks]
    v_refs = refs[1 + k_blocks:1 + 2 * k_blocks]
    o_ref, lse_ref = refs[1 + 2 * k_blocks:]
    piece_rows = q_ref.shape[1]
    tl = DIL_SUB_BLOCKS * HALO
    nkeys = (DIL_SUB_BLOCKS + 2) * HALO

    def row_runs(start, n):
        runs = []
        while n > 0:
            off = start % piece_rows
            take = min(n, piece_rows - off)
            runs.append((start // piece_rows, slice(off, off + take)))
            start, n = start + take, n - take
        return runs

    row = lax.broadcasted_iota(jnp.int32, (tl, nkeys), 0)
    col = lax.broadcasted_iota(jnp.int32, (tl, nkeys), 1)
    rel = col - row
    in_band = (rel >= 0) & (rel <= 2 * DIL_RADIUS)
    lane = lax.broadcasted_iota(jnp.int32, (tl, LANES), 1)
    for sub in range(q_blocks // DIL_SUB_BLOCKS):
        q_runs = row_runs(sub * tl, tl)
        pos = (pl.program_id(2) * q_blocks + sub * DIL_SUB_BLOCKS - 1) * HALO + col
        valid = in_band & (pos >= 0) & (pos < sub_len)
        kb = range(sub * DIL_SUB_BLOCKS, sub * DIL_SUB_BLOCKS + DIL_SUB_BLOCKS + 2)
        lse_all = jnp.zeros((tl, LANES), F32)
        for h in range(DIL_HEADS):
            hs = slice(h * LANES, (h + 1) * LANES)
            qq = jnp.concatenate([q_ref[p, rs, hs] for p, rs in q_runs], axis=0)
            kk = jnp.concatenate([k_refs[n][:, hs] for n in kb], axis=0)
            vv = jnp.concatenate([v_refs[n][:, hs] for n in kb], axis=0)
            s = lax.dot_general(qq, kk, NT_DIMS, preferred_element_type=F32)
            s = jnp.where(valid, s, -jnp.inf)
            m = jnp.max(s, axis=-1, keepdims=True)
            p = jnp.exp2(s - m)
            l = jnp.sum(p, axis=-1, keepdims=True)
            o = jnp.dot(p.astype(BF16), vv, preferred_element_type=F32) / l
            done = 0
            for pc, rs in q_runs:
                o_ref[pc, rs, hs] = o[done:done + rs.stop - rs.start].astype(o_ref.dtype)
                done += rs.stop - rs.start
            lse_all = jnp.where(lane == h, m + jnp.log2(l), lse_all)
        done = 0
        for pc, rs in q_runs:
            lse_ref[pc, rs, :] = lse_all[done:done + rs.stop - rs.start]
            done += rs.stop - rs.start


def _dil_attn(proj, group, batch, seq):
    _, dil = DIL_GROUPS[group]
    sub_len = seq // dil
    n_blocks = sub_len // HALO
    chunk_blocks = PERM_TILE // dil // HALO
    tile_blocks = PERM_TILE // HALO
    wide = DIL_HEADS * DIL_HEAD_DIM

    def row_block(b, r, n):
        n = jnp.clip(n, 0, n_blocks - 1)
        return (b * (seq // HALO) + (n // chunk_blocks) * tile_blocks + r * chunk_blocks + n % chunk_blocks)

    q_blocks = min(DIL_MAX_Q_BLOCKS, n_blocks)
    k_blocks = q_blocks + 2

    def halo_spec(comp, offset):
        return pl.BlockSpec((HALO, wide), lambda b, r, i: (row_block(b, r, q_blocks * i + offset),
                                                          _dil_tile(comp, group)))

    tl = q_blocks * HALO
    chunk = PERM_TILE // dil
    pieces = max(1, tl // chunk)
    piece_rows = tl // pieces
    subs = chunk // piece_rows
    m = batch * seq
    lead = m // (PERM_TILE * pieces)
    per_seq = seq // (PERM_TILE * pieces)

    def query_view(width):
        return (lead, pieces, dil, chunk, width)

    def query_spec(width, col_block):
        return pl.BlockSpec((None, pieces, None, piece_rows, width),
                            lambda b, r, i: (b * per_seq + i // subs, 0, r, i % subs, col_block))

    o, lse = pl.pallas_call(
        functools.partial(_dil_attn_kernel, sub_len=sub_len, q_blocks=q_blocks),
        grid=(batch, dil, n_blocks // q_blocks),
        in_specs=([query_spec(wide, _dil_tile(0, group))]
                  + [halo_spec(1, n - 1) for n in range(k_blocks)]
                  + [halo_spec(2, n - 1) for n in range(k_blocks)]),
        out_specs=[query_spec(wide, 0), query_spec(LANES, 0)],
        out_shape=[jax.ShapeDtypeStruct(query_view(wide), BF16), jax.ShapeDtypeStruct(query_view(LANES), F32)],
        compiler_params=_cparams("parallel", "parallel", "arbitrary"),
        name=f"dil_attn_g{group}",
    )(proj.reshape(query_view(IN_COLS_PAD)), *([proj] * (2 * k_blocks)))
    return o.reshape(m, wide), lse.reshape(m, LANES)


def _combine_kernel(x_ref, a_ref, b_ref, o0_ref, o1_ref, o2_ref, l0_ref, l1_ref, l2_ref, gate_ref,
                    wa_ref, wb_ref, wc_ref, wo_ref, out_ref, c_ref, on_refs, ln_refs):
    tm = x_ref.shape[0]
    o_tok, l_tok = [], []
    for g, (o_ref, l_ref) in enumerate(((o0_ref, l0_ref), (o1_ref, l1_ref), (o2_ref, l2_ref))):
        dil = DIL_GROUPS[g][1]
        if dil == 1:
            o_tok.append(lambda h, o_ref=o_ref: o_ref[0, 0, :, h * LANES:(h + 1) * LANES].astype(F32))
            l_tok.append(l_ref[0, 0])
            continue
        for r in range(dil):
            rows = pl.ds(r, tm // dil, stride=dil)
            ln_refs[g - 1, rows, :] = l_ref[0, r]
            for h in range(DIL_HEADS):
                on_refs[g - 1, h, rows, :] = o_ref[0, r, :, h * LANES:(h + 1) * LANES].astype(F32)
        o_tok.append(lambda h, g=g: on_refs[g - 1, h])
        l_tok.append(ln_refs[g - 1])
    l0, l1, l2 = l_tok
    mx = jnp.maximum(jnp.maximum(l0, l1), l2)
    e0, e1, e2 = jnp.exp2(l0 - mx), jnp.exp2(l1 - mx), jnp.exp2(l2 - mx)
    inv = 1.0 / (e0 + e1 + e2)
    w0, w1, w2 = e0 * inv, e1 * inv, e2 * inv
    for h in range(DIL_HEADS):
        hs = slice(h * LANES, (h + 1) * LANES)
        c = w0[:, h:h + 1] * o_tok[0](h) + w1[:, h:h + 1] * o_tok[1](h) + w2[:, h:h + 1] * o_tok[2](h)
        c_ref[:, hs] = c.astype(BF16)
    ya = jnp.dot(a_ref[...], wa_ref[...], preferred_element_type=F32)
    yb = jnp.dot(b_ref[...], wb_ref[...], preferred_element_type=F32)
    yc = jnp.dot(c_ref[...], wc_ref[...], preferred_element_type=F32)
    d = D_MODEL
    mix = (_sigmoid(gate_ref[:, 0:d].astype(F32)) * ya
           + _sigmoid(gate_ref[:, d:2 * d].astype(F32)) * yb
           + _sigmoid(gate_ref[:, 2 * d:3 * d].astype(F32)) * yc)
    out_ref[...] = x_ref[...] + jnp.dot(mix.astype(BF16), wo_ref[...], preferred_element_type=F32)


def _combine(x, gla_o, mla_o, dil_o, dil_lse, proj, p, batch, seq, tm=256):
    m = x.shape[0]
    d = D_MODEL
    ns = PERM_TILE // tm
    rowblk = lambda i: (i, 0)
    const = lambda i: (0, 0)
    wspec = pl.BlockSpec((d, d), const)

    def residue_major(arr, group, width):
        dil = DIL_GROUPS[group][1]
        view = arr.reshape(m // PERM_TILE, dil, PERM_TILE // dil, width)
        return view, pl.BlockSpec((1, dil, tm // dil, width), lambda i: (i // ns, 0, i % ns, 0))

    o_views, o_specs = zip(*[residue_major(dil_o[g], g, d) for g in range(len(DIL_GROUPS))])
    l_views, l_specs = zip(*[residue_major(dil_lse[g], g, LANES) for g in range(len(DIL_GROUPS))])
    n_perm = len(DIL_GROUPS) - 1
    return pl.pallas_call(
        _combine_kernel,
        grid=(m // tm,),
        in_specs=[pl.BlockSpec((tm, d), rowblk), pl.BlockSpec((tm, d), rowblk), pl.BlockSpec((tm, d), rowblk),
                  *o_specs, *l_specs,
                  pl.BlockSpec((tm, 3 * d), lambda i: (i, COL_GATE // (3 * d))),
                  wspec, wspec, wspec, wspec],
        out_specs=pl.BlockSpec((tm, d), rowblk),
        out_shape=jax.ShapeDtypeStruct((m, d), F32),
        scratch_shapes=[pltpu.VMEM((tm, d), BF16), pltpu.VMEM((n_perm, DIL_HEADS, tm, LANES), F32),
                        pltpu.VMEM((n_perm, tm, LANES), F32)],
        compiler_params=_cparams("parallel"),
        name="combine",
    )(x, gla_o, mla_o, *o_views, *l_views, proj,
      p["w_branch_a"], p["w_branch_b"], p["w_branch_c"], p["w_out"])


def _xattn_kernel(x_ref, g_ref, wq_ref, qn_ref, kv_ref, kn_ref, wo_ref, out_ref, o_ref):
    x = x_ref[...]
    h = _rms(x, g_ref[...]).astype(BF16)
    q = jnp.dot(h, wq_ref[...], preferred_element_type=F32)
    scale = X_HEAD_DIM ** -0.5 * LOG2_E
    for hd in range(X_HEADS):
        hs = slice(hd * X_HEAD_DIM, (hd + 1) * X_HEAD_DIM)
        vs = slice(D_MODEL + hd * X_HEAD_DIM, D_MODEL + (hd + 1) * X_HEAD_DIM)
        qh = (_rms(q[:, hs], qn_ref[...]) * scale).astype(BF16)
        kh = _rms(kv_ref[:, hs].astype(F32), kn_ref[...]).astype(BF16)
        s = lax.dot_general(qh, kh, NT_DIMS, preferred_element_type=F32)
        p = jnp.exp2(s - jnp.max(s, axis=-1, keepdims=True))
        l = jnp.sum(p, axis=-1, keepdims=True)
        o = jnp.dot(p.astype(BF16), kv_ref[:, vs], preferred_element_type=F32) / l
        o_ref[:, hs] = o.astype(BF16)
    out_ref[...] = x + jnp.dot(o_ref[...], wo_ref[...], preferred_element_type=F32)


def _xattn(x, kv, p, seq, tm=512):
    m = x.shape[0]
    d = D_MODEL
    per_seq = seq // tm
    const = lambda i: (0, 0)
    return pl.pallas_call(
        _xattn_kernel,
        grid=(m // tm,),
        in_specs=[pl.BlockSpec((tm, d), lambda i: (i, 0)),
                  pl.BlockSpec((1, d), const),
                  pl.BlockSpec((d, d), const),
                  pl.BlockSpec((1, X_HEAD_DIM), const),
                  pl.BlockSpec((N_MEM, 2 * d), lambda i: (i // per_seq, 0)),
                  pl.BlockSpec((1, X_HEAD_DIM), const),
                  pl.BlockSpec((d, d), const)],
        out_specs=pl.BlockSpec((tm, d), lambda i: (i, 0)),
        out_shape=jax.ShapeDtypeStruct((m, d), F32),
        scratch_shapes=[pltpu.VMEM((tm, d), BF16)],
        compiler_params=_cparams("parallel"),
        name="xattn",
    )(x, p["norm_xattn"], p["w_xq"], p["xq_norm"], kv, p["xk_norm"], p["w_xo"])


def _ffn_kernel(x_ref, g_ref, w1_ref, w2_ref, out_ref, h_ref, acc_ref):
    j = pl.program_id(1)

    @pl.when(j == 0)
    def _():
        h_ref[...] = _rms(x_ref[...], g_ref[...]).astype(BF16)
        acc_ref[...] = x_ref[...]

    u = jnp.maximum(jnp.dot(h_ref[...], w1_ref[...], preferred_element_type=F32), 0.0)
    acc_ref[...] += jnp.dot((u * u).astype(BF16), w2_ref[...], preferred_element_type=F32)

    @pl.when(j == pl.num_programs(1) - 1)
    def _():
        out_ref[...] = acc_ref[...]


def _ffn(x, p, tm=1024, tf=1024):
    m = x.shape[0]
    d = D_MODEL
    return pl.pallas_call(
        _ffn_kernel,
        grid=(m // tm, D_FF // tf),
        in_specs=[pl.BlockSpec((tm, d), lambda i, j: (i, 0)),
                  pl.BlockSpec((1, d), lambda i, j: (0, 0)),
                  pl.BlockSpec((d, tf), lambda i, j: (0, j)),
                  pl.BlockSpec((tf, d), lambda i, j: (j, 0))],
        out_specs=pl.BlockSpec((tm, d), lambda i, j: (i, 0)),
        out_shape=jax.ShapeDtypeStruct((m, d), F32),
        scratch_shapes=[pltpu.VMEM((tm, d), BF16), pltpu.VMEM((tm, d), F32)],
        compiler_params=_cparams("parallel", "arbitrary"),
        name="ffn",
    )(x, p["norm_ffn"], p["w_ff1"], p["w_ff2"])


def _rope_tables(positions):
    seq = len(positions)
    inv_freq = 1.0 / (ROPE_THETA ** (np.arange(0, 2 * ROPE_HALF, 2, dtype=np.float64) / (2 * ROPE_HALF)))
    ang = np.asarray(positions, np.float64)[:, None] * inv_freq[None, :]
    cos, sin = jnp.asarray(np.cos(ang), F32), jnp.asarray(np.sin(ang), F32)
    gap = ROPE_HI - ROPE_LO - ROPE_HALF
    tail = LANES - ROPE_HI - ROPE_HALF
    fill = lambda n, v: jnp.full((seq, n), v, F32)
    cos_t = jnp.concatenate([fill(ROPE_LO, 1.0), cos, fill(gap, 1.0), cos, fill(tail, 1.0)], axis=1)
    sin_t = jnp.concatenate([fill(ROPE_LO, 0.0), -sin, fill(gap, 0.0), sin, fill(tail, 0.0)], axis=1)
    return cos_t, sin_t


def _lanes_after(*taken):
    used = set(int(i) for t in taken for i in t)
    return [i for i in range(LANES) if i not in used]


_ROPE_LANES = list(range(ROPE_LO, ROPE_LO + ROPE_HALF)) + list(range(ROPE_HI, ROPE_HI + ROPE_HALF))
_MLA_LANE_OF_DIM = np.array(_lanes_after(_ROPE_LANES)[:MLA_NOPE] + _ROPE_LANES, np.int32)
_DIL_LANE_OF_DIM = np.array(_ROPE_LANES + _lanes_after(_ROPE_LANES), np.int32)


def _to_lanes(arr, lane_of_dim):
    dim_of_lane = np.full(LANES, -1)
    dim_of_lane[lane_of_dim] = np.arange(len(lane_of_dim))
    pieces, lane = [], 0
    while lane < LANES:
        run = 1
        while (lane + run < LANES and
               (dim_of_lane[lane + run] == dim_of_lane[lane] + run if dim_of_lane[lane] >= 0
                else dim_of_lane[lane + run] < 0)):
            run += 1
        start = int(dim_of_lane[lane])
        pieces.append(arr[..., start:start + run] if start >= 0
                      else jnp.zeros(arr.shape[:-1] + (run,), arr.dtype))
        lane += run
    return jnp.concatenate(pieces, axis=-1)


def _prep_layer(w):
    d = D_MODEL
    hk = GLA_HEADS * GLA_DK
    hv = GLA_HEADS * GLA_DV
    dil_w = len(DIL_GROUPS) * DIL_HEADS * DIL_HEAD_DIM
    sizes = (hk, hk, hv, hv, GLA_GATE_RANK, GLA_GATE_RANK, MLA_Q_RANK, MLA_KV_RANK, MLA_ROPE,
             dil_w, dil_w, dil_w, 3 * d)
    offs = [0]
    for s in sizes:
        offs.append(offs[-1] + s)
    (a_q, a_k, a_v, a_r, a_gf, a_gb, b_q, b_kv, b_kpe, c_q, c_k, c_v, gates) = [
        w["w_in"][:, offs[i]:offs[i + 1]] for i in range(len(sizes))]
    z = lambda n: jnp.zeros((d, n), F32)
    small_lane_of_dim = np.concatenate([_MLA_LANE_OF_DIM[MLA_NOPE:], SMALL_GF + np.arange(GLA_GATE_RANK),
                                        SMALL_GB + np.arange(GLA_GATE_RANK)])
    small = _to_lanes(jnp.concatenate([b_kpe, a_gf, a_gb], axis=1), small_lane_of_dim)

    def dil_heads(cols):
        per_head = cols.reshape(d, len(DIL_GROUPS) * DIL_HEADS, DIL_HEAD_DIM)
        return _to_lanes(per_head, _DIL_LANE_OF_DIM).reshape(d, dil_w)

    gw = DIL_HEADS * DIL_HEAD_DIM
    dil_tiles = [t[:, g * gw:(g + 1) * gw] for g in range(len(DIL_GROUPS))
                 for t in (dil_heads(c_q), dil_heads(c_k), c_v)]
    w_in = jnp.concatenate(dil_tiles + [gates, a_q, a_k, a_v, a_r, b_q, b_kv, small,
                                        z(IN_COLS_PAD - COL_SMALL - LANES)], axis=1).astype(BF16)

    def gate_w(wg, lane0):
        rows = lambda n: jnp.zeros((n, hk), F32)
        return jnp.concatenate([rows(lane0), wg, rows(LANES - lane0 - GLA_GATE_RANK)], axis=0).astype(BF16)

    wgf = gate_w(w["gla_w_gate_f"], SMALL_GF)
    wgb = gate_w(w["gla_w_gate_b"], SMALL_GB)

    w_uq = _to_lanes(w["mla_w_uq"].reshape(MLA_Q_RANK, MLA_HEADS, MLA_QK), _MLA_LANE_OF_DIM)
    w_uq = w_uq.reshape(MLA_Q_RANK, MLA_HEADS * LANES)
    w_ukv = w["mla_w_ukv"].reshape(MLA_KV_RANK, MLA_HEADS, MLA_NOPE + MLA_V)
    w_uk = _to_lanes(w_ukv[:, :, :MLA_NOPE], _MLA_LANE_OF_DIM[:MLA_NOPE])
    w_uk = w_uk.reshape(MLA_KV_RANK, MLA_HEADS * LANES)
    w_uv = w_ukv[:, :, MLA_NOPE:].reshape(MLA_KV_RANK, MLA_HEADS * MLA_V)
    row = lambda v: v.reshape(1, -1).astype(F32)
    return {
        "norm_mix": row(w["norm_mix"]), "w_in": w_in,
        "gla_wgf": wgf, "gla_bgf": row(w["gla_b_gate_f"]),
        "gla_wgb": wgb, "gla_bgb": row(w["gla_b_gate_b"]),
        "gla_out_norm": row(w["gla_out_norm"]), "w_branch_a": w["w_branch_a"].astype(BF16),
        "mla_q_lat_norm": row(w["mla_q_lat_norm"]), "mla_w_uq": w_uq.astype(BF16),
        "mla_kv_lat_norm": row(w["mla_kv_lat_norm"]), "mla_w_uk": w_uk.astype(BF16),
        "mla_w_uv": w_uv.astype(BF16),
        "mla_q_norm": _to_lanes(w["mla_q_norm"], _MLA_LANE_OF_DIM).reshape(1, LANES),
        "mla_k_norm": _to_lanes(w["mla_k_norm"], _MLA_LANE_OF_DIM).reshape(1, LANES),
        "w_branch_b": w["w_branch_b"].astype(BF16),
        "dil_qk_norm": _to_lanes(jnp.stack([w["dil_q_norm"], w["dil_k_norm"]]),
                                 _DIL_LANE_OF_DIM).reshape(2, 1, LANES),
        "w_branch_c": w["w_branch_c"].astype(BF16), "w_out": w["w_out"].astype(BF16),
        "norm_xattn": row(w["norm_xattn"]), "norm_mem": row(w["norm_mem"]),
        "w_xq": w["w_xq"].astype(BF16), "w_xkv": w["w_xkv"].astype(BF16),
        "xq_norm": row(w["xq_norm"]), "xk_norm": row(w["xk_norm"]), "w_xo": w["w_xo"].astype(BF16),
        "norm_ffn": row(w["norm_ffn"]), "w_ff1": w["w_ff1"].astype(BF16), "w_ff2": w["w_ff2"].astype(BF16),
    }


def _layer(x, mem, p, tables, batch, seq):
    (cos_m, sin_m), dil_tables = tables
    proj = _in_proj(x, p, dil_tables, seq)
    o_fwd = _gla_direction(proj, p["gla_wgf"], p["gla_bgf"], batch, seq, False)
    gla_o = _gla_direction(proj, p["gla_wgb"], p["gla_bgb"], batch, seq, True,
                           o_fwd=o_fwd, g_out=p["gla_out_norm"])
    q_m, k_m, v_m = _mla_prep(proj, p, cos_m, sin_m, seq)
    mla_o = _mla_attn(q_m, k_m, v_m, batch, seq)
    dil = [_dil_attn(proj, g, batch, seq) for g in range(len(DIL_GROUPS))]
    x = _combine(x, gla_o, mla_o, [o for o, _ in dil], [l for _, l in dil], proj, p, batch, seq)
    kv = _norm_matmul(mem, p["norm_mem"], p["w_xkv"], BF16, tm=N_MEM, tn=1024)
    x = _xattn(x, kv, p, seq)
    return _ffn(x, p)


def _trunk(x, mem, layers):
    batch, seq, d = x.shape
    natural = np.arange(seq)
    row_pos = [natural.reshape(seq // PERM_TILE, PERM_TILE // dil, dil).transpose(0, 2, 1).reshape(seq)
               for _, dil in DIL_GROUPS]
    dil_cos, dil_sin = zip(*[_rope_tables(pos) for pos in row_pos])
    tables = (_rope_tables(natural), (jnp.stack(dil_cos), jnp.stack(dil_sin)))
    xf = x.reshape(batch * seq, d)
    memf = mem.reshape(batch * mem.shape[1], d)
    for p in layers:
        xf = _layer(xf, memf, p, tables, batch, seq)
    return xf.reshape(batch, seq, d)


def kernel(x_prompt, x_sample, mem_prompt, mem_sample, norm_mix, w_in, gla_w_gate_f, gla_b_gate_f, gla_w_gate_b, gla_b_gate_b, gla_out_norm, w_branch_a, mla_q_lat_norm, mla_w_uq, mla_kv_lat_norm, mla_w_ukv, mla_q_norm, mla_k_norm, w_branch_b, dil_q_norm, dil_k_norm, w_branch_c, w_out, norm_xattn, norm_mem, w_xq, w_xkv, xq_norm, xk_norm, w_xo, norm_ffn, w_ff1, w_ff2):
    stacked = dict(norm_mix=norm_mix, w_in=w_in, gla_w_gate_f=gla_w_gate_f, gla_b_gate_f=gla_b_gate_f,
                   gla_w_gate_b=gla_w_gate_b, gla_b_gate_b=gla_b_gate_b, gla_out_norm=gla_out_norm,
                   w_branch_a=w_branch_a, mla_q_lat_norm=mla_q_lat_norm, mla_w_uq=mla_w_uq,
                   mla_kv_lat_norm=mla_kv_lat_norm, mla_w_ukv=mla_w_ukv, mla_q_norm=mla_q_norm,
                   mla_k_norm=mla_k_norm, w_branch_b=w_branch_b, dil_q_norm=dil_q_norm, dil_k_norm=dil_k_norm,
                   w_branch_c=w_branch_c, w_out=w_out, norm_xattn=norm_xattn, norm_mem=norm_mem, w_xq=w_xq,
                   w_xkv=w_xkv, xq_norm=xq_norm, xk_norm=xk_norm, w_xo=w_xo, norm_ffn=norm_ffn,
                   w_ff1=w_ff1, w_ff2=w_ff2)
    layers = [_prep_layer({k: v[l] for k, v in stacked.items()}) for l in range(DEPTH)]
    return (_trunk(x_prompt, mem_prompt, layers), _trunk(x_sample, mem_sample, layers))
```

```python
import functools

import jax
import jax.numpy as jnp
import numpy as np
from jax import lax
from jax.experimental import pallas as pl
from jax.experimental.pallas import tpu as pltpu

F32 = jnp.float32
BF16 = jnp.bfloat16

D_MODEL = 1024
DEPTH = 2
N_MEM = 256
ROPE_THETA = 500000.0
NORM_EPS = 1e-6
GLA_HEADS = 4
GLA_DK = 128
GLA_DV = 256
GLA_GATE_RANK = 16
GLA_TAU = 16.0
GLA_CHUNK = 64
MLA_HEADS = 8
MLA_Q_RANK = 256
MLA_KV_RANK = 128
MLA_NOPE = 64
MLA_ROPE = 32
MLA_QK = 96
MLA_V = 128
DIL_GROUPS = ((128, 1), (512, 4), (2048, 16))
DIL_HEADS = 8
DIL_HEAD_DIM = 128
DIL_ROT = 32
DIL_RADIUS = 64
X_HEADS = 4
X_HEAD_DIM = 256
D_FF = 4096

LANES = 128
ROPE_HALF = 16
VMEM_LIMIT_BYTES = 56 * 1024 * 1024

COL_GATE = 9216
PERM_TILE = 1024
HALO = 64
QK_ROWS = 256
COL_AQ, COL_AK, COL_AV, COL_AR = 12288, 12800, 13312, 14336
COL_BQ, COL_BKV, COL_SMALL = 15360, 15616, 15744
IN_COLS_PAD = 16384
SMALL_GF, SMALL_GB = 16, 32
ROPE_LO, ROPE_HI = 0, 64

NT_DIMS = (((1,), (1,)), ((), ()))
LOG2_E = 1.4426950408889634


def _cparams(*sem):
    return pltpu.CompilerParams(dimension_semantics=sem, vmem_limit_bytes=VMEM_LIMIT_BYTES)


def _rms(x, g, n=None):
    ss = jnp.sum(x * x, axis=-1, keepdims=True) * (1.0 / (n or x.shape[-1]))
    return x * lax.rsqrt(ss + NORM_EPS) * g


def _rms_head_mxu(x, g, n):
    ones = jnp.ones((LANES, LANES), BF16)
    ss = jnp.dot((x * x).astype(BF16), ones, preferred_element_type=F32)
    return x * lax.rsqrt(ss * (1.0 / n) + NORM_EPS) * g


def _log_sigmoid(z):
    return jnp.minimum(z, 0.0) - jnp.log(1.0 + jnp.exp(-jnp.abs(z)))


def _sigmoid(z):
    return 1.0 / (1.0 + jnp.exp(-z))


def _norm_matmul_kernel(x_ref, g_ref, w_ref, o_ref, h_ref):
    @pl.when(pl.program_id(1) == 0)
    def _():
        h_ref[...] = _rms(x_ref[...].astype(F32), g_ref[...]).astype(BF16)

    o_ref[...] = jnp.dot(h_ref[...], w_ref[...], preferred_element_type=F32).astype(o_ref.dtype)


def _norm_matmul(x, g, w, out_dtype, tm, tn):
    m, k = x.shape
    n = w.shape[1]
    return pl.pallas_call(
        _norm_matmul_kernel,
        grid=(m // tm, n // tn),
        in_specs=[pl.BlockSpec((tm, k), lambda i, j: (i, 0)),
                  pl.BlockSpec((1, k), lambda i, j: (0, 0)),
                  pl.BlockSpec((k, tn), lambda i, j: (0, j))],
        out_specs=pl.BlockSpec((tm, tn), lambda i, j: (i, j)),
        out_shape=jax.ShapeDtypeStruct((m, n), out_dtype),
        scratch_shapes=[pltpu.VMEM((tm, k), BF16)],
        compiler_params=_cparams("parallel", "arbitrary"),
        name="norm_matmul",
    )(x, g, w)


def _dil_tile(comp, group):
    return len(DIL_GROUPS) * group + comp


def _in_proj_kernel(x_ref, g_ref, w_ref, nw_ref, cos_ref, sin_ref, o_ref, h_ref, hf_ref):
    j = pl.program_id(1)
    n_dil_tiles = 3 * len(DIL_GROUPS)
    wide = DIL_HEADS * DIL_HEAD_DIM
    nc = D_MODEL // LANES

    @pl.when(j == 0)
    def _():
        h = _rms(x_ref[...], g_ref[...])
        h_ref[0] = h.astype(BF16)
        for c in range(nc):
            hf_ref[c] = h[:, c * LANES:(c + 1) * LANES]
        for g, (_, dil) in enumerate(DIL_GROUPS):
            if dil == 1:
                continue
            rows = PERM_TILE // dil
            for r in range(dil):
                for c in range(nc):
                    h_ref[g, r * rows:(r + 1) * rows, c * LANES:(c + 1) * LANES] = (
                        hf_ref[c, pl.ds(r, rows, stride=dil), :].astype(BF16))

    is_dil = j < n_dil_tiles
    group = jnp.where(is_dil, j // 3, 0)
    comp = j % 3
    is_qk = is_dil & (comp < 2)

    @pl.when(is_qk)
    def _():
        nw = nw_ref[0]
        scale = jnp.where(comp == 0, DIL_HEAD_DIM ** -0.5 * LOG2_E, 1.0).astype(F32)
        for c in range(PERM_TILE // QK_ROWS):
            rs = slice(c * QK_ROWS, (c + 1) * QK_ROWS)
            acc = jnp.dot(h_ref[group, rs, :], w_ref[...], preferred_element_type=F32)
            cos = cos_ref[0, rs, :]
            sin = sin_ref[0, rs, :]
            for h in range(wide // LANES):
                hs = slice(h * LANES, (h + 1) * LANES)
                y = _rms_head_mxu(acc[:, hs], nw, DIL_HEAD_DIM)
                o_ref[rs, hs] = (_rope(y, cos, sin) * scale).astype(o_ref.dtype)

    @pl.when(jnp.logical_not(is_qk))
    def _():
        o_ref[...] = jnp.dot(h_ref[group], w_ref[...], preferred_element_type=F32).astype(o_ref.dtype)


def _in_proj(x, p, tables, seq):
    m, k = x.shape
    tn = DIL_HEADS * DIL_HEAD_DIM
    ng = len(DIL_GROUPS)
    per_seq = seq // PERM_TILE
    cos, sin = tables
    group_of = lambda j: jnp.where(j < 3 * ng, j // 3, 0)
    table_spec = pl.BlockSpec((1, PERM_TILE, LANES), lambda i, j: (group_of(j), i % per_seq, 0))
    return pl.pallas_call(
        _in_proj_kernel,
        grid=(m // PERM_TILE, IN_COLS_PAD // tn),
        in_specs=[pl.BlockSpec((PERM_TILE, k), lambda i, j: (i, 0)),
                  pl.BlockSpec((1, k), lambda i, j: (0, 0)),
                  pl.BlockSpec((k, tn), lambda i, j: (0, j)),
                  pl.BlockSpec((1, 1, LANES), lambda i, j: (jnp.minimum(j % 3, 1), 0, 0)),
                  table_spec, table_spec],
        out_specs=pl.BlockSpec((PERM_TILE, tn), lambda i, j: (i, j)),
        out_shape=jax.ShapeDtypeStruct((m, IN_COLS_PAD), BF16),
        scratch_shapes=[pltpu.VMEM((ng, PERM_TILE, k), BF16),
                        pltpu.VMEM((k // LANES, PERM_TILE, LANES), F32)],
        compiler_params=_cparams("parallel", "arbitrary"),
        name="in_proj",
    )(x, p["norm_mix"], p["w_in"], p["dil_qk_norm"], cos, sin)


TN_DIMS = (((0,), (0,)), ((), ()))


def _gla_kernel(*refs, tb, reverse, final):
    if final:
        q_ref, k_ref, v_ref, sm_ref, wg_ref, bg_ref, of_ref, r_ref, gn_ref, o_ref, st_ref = refs
    else:
        q_ref, k_ref, v_ref, sm_ref, wg_ref, bg_ref, o_ref, st_ref = refs

    @pl.when(pl.program_id(1) == 0)
    def _():
        st_ref[...] = jnp.zeros_like(st_ref)

    ck = GLA_CHUNK
    la = _log_sigmoid(jnp.dot(sm_ref[...], wg_ref[...], preferred_element_type=F32) + bg_ref[...]) * (1.0 / GLA_TAU)

    r = lax.broadcasted_iota(jnp.int32, (ck, ck), 0)
    c = lax.broadcasted_iota(jnp.int32, (ck, ck), 1)
    tri_mask = (r <= c) if reverse else (r >= c)
    tri = jnp.where(tri_mask, 1.0, 0.0).astype(BF16)
    end = 0 if reverse else ck - 1
    scale = GLA_DK ** -0.5

    n_chunks = tb // ck
    order = range(n_chunks - 1, -1, -1) if reverse else range(n_chunks)
    for ci in order:
        sl = slice(ci * ck, (ci + 1) * ck)
        la_c = la[sl]
        hi = la_c.astype(BF16)
        lo = (la_c - hi.astype(F32)).astype(BF16)
        b = jnp.dot(tri, hi, preferred_element_type=F32) + jnp.dot(tri, lo, preferred_element_type=F32)
        b_end = b[end:end + 1, :]
        kc = k_ref[sl, :].astype(F32)
        qd = (q_ref[sl, :].astype(F32) * scale * jnp.exp(b)).astype(BF16)
        kinv = (kc * jnp.exp(-b)).astype(BF16)
        ktail = (kc * jnp.exp(b_end - b)).astype(BF16)
        dec = jnp.exp(b_end)
        for h in range(GLA_HEADS):
            ks = slice(h * GLA_DK, (h + 1) * GLA_DK)
            vs = slice(h * GLA_DV, (h + 1) * GLA_DV)
            qh = qd[:, ks]
            att = lax.dot_general(qh, kinv[:, ks], NT_DIMS, preferred_element_type=F32)
            att = jnp.where(tri_mask, att, 0.0).astype(BF16)
            vh = v_ref[sl, vs]
            st = st_ref[h]
            o = (jnp.dot(att, vh, preferred_element_type=F32)
                 + lax.dot_general(qh, st.astype(BF16), NT_DIMS, preferred_element_type=F32))
            st_ref[h] = dec[:, ks] * st + lax.dot_general(vh, ktail[:, ks], TN_DIMS, preferred_element_type=F32)
            if final:
                o = _rms(o + of_ref[sl, vs], gn_ref[...])
                rr = r_ref[sl, vs].astype(F32)
                o = o * (rr * _sigmoid(rr))
            o_ref[sl, vs] = o.astype(o_ref.dtype)


def _gla_direction(proj, wg, bg, batch, seq, reverse, o_fwd=None, g_out=None, tb=256):
    nb = seq // tb
    m = batch * seq
    final = o_fwd is not None

    def row(b, i):
        return b * nb + (nb - 1 - i if reverse else i)

    def col(block_cols, off):
        return lambda b, i: (row(b, i), off // block_cols)

    hk = GLA_HEADS * GLA_DK
    hv = GLA_HEADS * GLA_DV
    const = lambda b, i: (0, 0)
    in_specs = [pl.BlockSpec((tb, hk), col(hk, COL_AQ)),
                pl.BlockSpec((tb, hk), col(hk, COL_AK)),
                pl.BlockSpec((tb, hv), col(hv, COL_AV)),
                pl.BlockSpec((tb, LANES), col(LANES, COL_SMALL)),
                pl.BlockSpec((LANES, hk), const),
                pl.BlockSpec((1, hk), const)]
    args = [proj, proj, proj, proj, wg, bg]
    if final:
        in_specs += [pl.BlockSpec((tb, hv), col(hv, 0)),
                     pl.BlockSpec((tb, hv), col(hv, COL_AR)),
                     pl.BlockSpec((1, GLA_DV), const)]
        args += [o_fwd, proj, g_out]
    return pl.pallas_call(
        functools.partial(_gla_kernel, tb=tb, reverse=reverse, final=final),
        grid=(batch, nb),
        in_specs=in_specs,
        out_specs=pl.BlockSpec((tb, hv), col(hv, 0)),
        out_shape=jax.ShapeDtypeStruct((m, hv), BF16 if final else F32),
        scratch_shapes=[pltpu.VMEM((GLA_HEADS, GLA_DV, GLA_DK), F32)],
        compiler_params=_cparams("parallel", "arbitrary"),
        name="gla_bwd" if reverse else "gla_fwd",
    )(*args)


def _rope(y, cos, sin_signed):
    return y * cos + pltpu.roll(y, LANES // 2, 1) * sin_signed


def _mla_prep_kernel(bq_ref, bkv_ref, sm_ref, qln_ref, wuq_ref, kvln_ref, wuk_ref, wuv_ref,
                     qn_ref, kn_ref, cos_ref, sin_ref, q_out, k_out, v_out):
    hq = _rms(bq_ref[...].astype(F32), qln_ref[...]).astype(BF16)
    hkv = _rms(bkv_ref[...].astype(F32), kvln_ref[...]).astype(BF16)
    q = jnp.dot(hq, wuq_ref[...], preferred_element_type=F32)
    kn = jnp.dot(hkv, wuk_ref[...], preferred_element_type=F32)
    v = jnp.dot(hkv, wuv_ref[...], preferred_element_type=F32)
    tm = q.shape[0]
    lane = lax.broadcasted_iota(jnp.int32, (tm, LANES), 1)
    sm = sm_ref[...].astype(F32)
    in_rope = (lane < ROPE_LO + ROPE_HALF) | ((lane >= ROPE_HI) & (lane < ROPE_HI + ROPE_HALF))
    kpe = jnp.where(in_rope, sm, 0.0)
    cos = cos_ref[...]
    sin = sin_ref[...]
    scale = MLA_QK ** -0.5 * LOG2_E
    ones = jnp.ones((tm, LANES), v_out.dtype)
    for h in range(MLA_HEADS):
        hs = slice(h * LANES, (h + 1) * LANES)
        qh = _rope(_rms_head_mxu(q[:, hs], qn_ref[...], MLA_QK), cos, sin)
        q_out[:, hs] = (qh * scale).astype(q_out.dtype)
        kh = _rope(_rms_head_mxu(kn[:, hs] + kpe, kn_ref[...], MLA_QK), cos, sin)
        k_out[:, hs] = kh.astype(k_out.dtype)
        v_out[:, 2 * h * LANES:(2 * h + 1) * LANES] = v[:, hs].astype(v_out.dtype)
        v_out[:, (2 * h + 1) * LANES:(2 * h + 2) * LANES] = ones


def _mla_prep(proj, p, cos, sin, seq, tm=512):
    m = proj.shape[0]
    ns = seq // tm
    const = lambda i: (0, 0)
    wide = MLA_HEADS * LANES
    out = jax.ShapeDtypeStruct((m, wide), BF16)
    return pl.pallas_call(
        _mla_prep_kernel,
        grid=(m // tm,),
        in_specs=[pl.BlockSpec((tm, MLA_Q_RANK), lambda i: (i, COL_BQ // MLA_Q_RANK)),
                  pl.BlockSpec((tm, MLA_KV_RANK), lambda i: (i, COL_BKV // MLA_KV_RANK)),
                  pl.BlockSpec((tm, LANES), lambda i: (i, COL_SMALL // LANES)),
                  pl.BlockSpec((1, MLA_Q_RANK), const),
                  pl.BlockSpec((MLA_Q_RANK, wide), const),
                  pl.BlockSpec((1, MLA_KV_RANK), const),
                  pl.BlockSpec((MLA_KV_RANK, wide), const),
                  pl.BlockSpec((MLA_KV_RANK, wide), const),
                  pl.BlockSpec((1, LANES), const),
                  pl.BlockSpec((1, LANES), const),
                  pl.BlockSpec((tm, LANES), lambda i: (i % ns, 0)),
                  pl.BlockSpec((tm, LANES), lambda i: (i % ns, 0))],
        out_specs=[pl.BlockSpec((tm, wide), lambda i: (i, 0)), pl.BlockSpec((tm, wide), lambda i: (i, 0)),
                   pl.BlockSpec((tm, 2 * wide), lambda i: (i, 0))],
        out_shape=[out, out, jax.ShapeDtypeStruct((m, 2 * wide), BF16)],
        compiler_params=_cparams("parallel"),
        name="mla_prep",
    )(proj, proj, proj, p["mla_q_lat_norm"], p["mla_w_uq"], p["mla_kv_lat_norm"], p["mla_w_uk"],
      p["mla_w_uv"], p["mla_q_norm"], p["mla_k_norm"], cos, sin)


def _mla_attn_kernel(q_ref, k_ref, v_ref, o_ref, *, tk, unroll, sub):
    nsub = q_ref.shape[0] // sub
    nk = k_ref.shape[0] // tk
    qs = [q_ref[i * sub:(i + 1) * sub, :] for i in range(nsub)]

    def body(t, carry):
        start = pl.multiple_of(t * tk, tk)
        out = []
        for q, (m, acc) in zip(qs, carry):
            s = lax.dot_general(q, k_ref[pl.ds(start, tk), :], NT_DIMS, preferred_element_type=F32)
            m_new = jnp.maximum(m, jnp.max(s, axis=-1, keepdims=True))
            p = jnp.exp2(s - m_new).astype(BF16)
            acc = jnp.exp2(m - m_new) * acc + jnp.dot(p, v_ref[pl.ds(start, tk), :], preferred_element_type=F32)
            out.append((m_new, acc))
        return tuple(out)

    init = tuple((jnp.full((sub, 1), -jnp.inf, F32), jnp.zeros((sub, 2 * LANES), F32)) for _ in range(nsub))
    final = lax.fori_loop(0, nk, body, init, unroll=unroll)
    for i, (_, acc) in enumerate(final):
        o_ref[i * sub:(i + 1) * sub, :] = (acc[:, :LANES] / acc[:, LANES:]).astype(o_ref.dtype)


MLA_KV_TILE = 256
MLA_TILES_IN_FLIGHT = 64
MLA_SUB_ROWS = (256, 512)


def _mla_attn(q, k, v, batch, seq, tk=MLA_KV_TILE):
    unroll = min(MLA_TILES_IN_FLIGHT, seq // tk)
    tq = min(seq, MLA_SUB_ROWS[0] * (MLA_TILES_IN_FLIGHT // unroll))
    sub = min(tq, MLA_SUB_ROWS[unroll < MLA_TILES_IN_FLIGHT])
    nq = seq // tq
    return pl.pallas_call(
        functools.partial(_mla_attn_kernel, tk=tk, unroll=unroll, sub=sub),
        grid=(batch, MLA_HEADS, nq),
        in_specs=[pl.BlockSpec((tq, LANES), lambda b, h, i: (b * nq + i, h)),
                  pl.BlockSpec((seq, LANES), lambda b, h, i: (b, h)),
                  pl.BlockSpec((seq, 2 * LANES), lambda b, h, i: (b, h))],
        out_specs=pl.BlockSpec((tq, LANES), lambda b, h, i: (b * nq + i, h)),
        out_shape=jax.ShapeDtypeStruct(q.shape, BF16),
        compiler_params=_cparams("parallel", "parallel", "arbitrary"),
        name="mla_attn",
    )(q, k, v)


DIL_SUB_BLOCKS = 2
DIL_MAX_Q_BLOCKS = 16


def _dil_attn_kernel(*refs, sub_len, q_blocks):
    k_blocks = q_blocks + 2
    q_ref = refs[0]
    k_refs = refs[1:1 + k_blocks]
    v_refs = refs[1 + k_blocks:1 + 2 * k_blocks]
    o_ref, lse_ref = refs[1 + 2 * k_blocks:]
    piece_rows = q_ref.shape[1]
    tl = DIL_SUB_BLOCKS * HALO
    nkeys = (DIL_SUB_BLOCKS + 2) * HALO

    def row_runs(start, n):
        runs = []
        while n > 0:
            off = start % piece_rows
            take = min(n, piece_rows - off)
            runs.append((start // piece_rows, slice(off, off + take)))
            start, n = start + take, n - take
        return runs

    row = lax.broadcasted_iota(jnp.int32, (tl, nkeys), 0)
    col = lax.broadcasted_iota(jnp.int32, (tl, nkeys), 1)
    rel = col - row
    in_band = (rel >= 0) & (rel <= 2 * DIL_RADIUS)
    lane = lax.broadcasted_iota(jnp.int32, (tl, LANES), 1)
    for sub in range(q_blocks // DIL_SUB_BLOCKS):
        q_runs = row_runs(sub * tl, tl)
        pos = (pl.program_id(2) * q_blocks + sub * DIL_SUB_BLOCKS - 1) * HALO + col
        valid = in_band & (pos >= 0) & (pos < sub_len)
        kb = range(sub * DIL_SUB_BLOCKS, sub * DIL_SUB_BLOCKS + DIL_SUB_BLOCKS + 2)
        lse_all = jnp.zeros((tl, LANES), F32)
        for h in range(DIL_HEADS):
            hs = slice(h * LANES, (h + 1) * LANES)
            qq = jnp.concatenate([q_ref[p, rs, hs] for p, rs in q_runs], axis=0)
            kk = jnp.concatenate([k_refs[n][:, hs] for n in kb], axis=0)
            vv = jnp.concatenate([v_refs[n][:, hs] for n in kb], axis=0)
            s = lax.dot_general(qq, kk, NT_DIMS, preferred_element_type=F32)
            s = jnp.where(valid, s, -jnp.inf)
            m = jnp.max(s, axis=-1, keepdims=True)
            p = jnp.exp2(s - m)
            l = jnp.sum(p, axis=-1, keepdims=True)
            o = jnp.dot(p.astype(BF16), vv, preferred_element_type=F32) / l
            done = 0
            for pc, rs in q_runs:
                o_ref[pc, rs, hs] = o[done:done + rs.stop - rs.start].astype(o_ref.dtype)
                done += rs.stop - rs.start
            lse_all = jnp.where(lane == h, m + jnp.log2(l), lse_all)
        done = 0
        for pc, rs in q_runs:
            lse_ref[pc, rs, :] = lse_all[done:done + rs.stop - rs.start]
            done += rs.stop - rs.start


def _dil_attn(proj, group, batch, seq):
    _, dil = DIL_GROUPS[group]
    sub_len = seq // dil
    n_blocks = sub_len // HALO
    chunk_blocks = PERM_TILE // dil // HALO
    tile_blocks = PERM_TILE // HALO
    wide = DIL_HEADS * DIL_HEAD_DIM

    def row_block(b, r, n):
        n = jnp.clip(n, 0, n_blocks - 1)
        return (b * (seq // HALO) + (n // chunk_blocks) * tile_blocks + r * chunk_blocks + n % chunk_blocks)

    q_blocks = min(DIL_MAX_Q_BLOCKS, n_blocks)
    k_blocks = q_blocks + 2

    def halo_spec(comp, offset):
        return pl.BlockSpec((HALO, wide), lambda b, r, i: (row_block(b, r, q_blocks * i + offset),
                                                          _dil_tile(comp, group)))

    tl = q_blocks * HALO
    chunk = PERM_TILE // dil
    pieces = max(1, tl // chunk)
    piece_rows = tl // pieces
    subs = chunk // piece_rows
    m = batch * seq
    lead = m // (PERM_TILE * pieces)
    per_seq = seq // (PERM_TILE * pieces)

    def query_view(width):
        return (lead, pieces, dil, chunk, width)

    def query_spec(width, col_block):
        return pl.BlockSpec((None, pieces, None, piece_rows, width),
                            lambda b, r, i: (b * per_seq + i // subs, 0, r, i % subs, col_block))

    o, lse = pl.pallas_call(
        functools.partial(_dil_attn_kernel, sub_len=sub_len, q_blocks=q_blocks),
        grid=(batch, dil, n_blocks // q_blocks),
        in_specs=([query_spec(wide, _dil_tile(0, group))]
                  + [halo_spec(1, n - 1) for n in range(k_blocks)]
                  + [halo_spec(2, n - 1) for n in range(k_blocks)]),
        out_specs=[query_spec(wide, 0), query_spec(LANES, 0)],
        out_shape=[jax.ShapeDtypeStruct(query_view(wide), BF16), jax.ShapeDtypeStruct(query_view(LANES), F32)],
        compiler_params=_cparams("parallel", "parallel", "arbitrary"),
        name=f"dil_attn_g{group}",
    )(proj.reshape(query_view(IN_COLS_PAD)), *([proj] * (2 * k_blocks)))
    return o.reshape(m, wide), lse.reshape(m, LANES)


def _combine_kernel(x_ref, a_ref, b_ref, o0_ref, o1_ref, o2_ref, l0_ref, l1_ref, l2_ref, gate_ref,
                    wa_ref, wb_ref, wc_ref, wo_ref, out_ref, c_ref, on_refs, ln_refs):
    tm = x_ref.shape[0]
    o_tok, l_tok = [], []
    for g, (o_ref, l_ref) in enumerate(((o0_ref, l0_ref), (o1_ref, l1_ref), (o2_ref, l2_ref))):
        dil = DIL_GROUPS[g][1]
        if dil == 1:
            o_tok.append(lambda h, o_ref=o_ref: o_ref[0, 0, :, h * LANES:(h + 1) * LANES].astype(F32))
            l_tok.append(l_ref[0, 0])
            continue
        for r in range(dil):
            rows = pl.ds(r, tm // dil, stride=dil)
            ln_refs[g - 1, rows, :] = l_ref[0, r]
            for h in range(DIL_HEADS):
                on_refs[g - 1, h, rows, :] = o_ref[0, r, :, h * LANES:(h + 1) * LANES].astype(F32)
        o_tok.append(lambda h, g=g: on_refs[g - 1, h])
        l_tok.append(ln_refs[g - 1])
    l0, l1, l2 = l_tok
    mx = jnp.maximum(jnp.maximum(l0, l1), l2)
    e0, e1, e2 = jnp.exp2(l0 - mx), jnp.exp2(l1 - mx), jnp.exp2(l2 - mx)
    inv = 1.0 / (e0 + e1 + e2)
    w0, w1, w2 = e0 * inv, e1 * inv, e2 * inv
    for h in range(DIL_HEADS):
        hs = slice(h * LANES, (h + 1) * LANES)
        c = w0[:, h:h + 1] * o_tok[0](h) + w1[:, h:h + 1] * o_tok[1](h) + w2[:, h:h + 1] * o_tok[2](h)
        c_ref[:, hs] = c.astype(BF16)
    ya = jnp.dot(a_ref[...], wa_ref[...], preferred_element_type=F32)
    yb = jnp.dot(b_ref[...], wb_ref[...], preferred_element_type=F32)
    yc = jnp.dot(c_ref[...], wc_ref[...], preferred_element_type=F32)
    d = D_MODEL
    mix = (_sigmoid(gate_ref[:, 0:d].astype(F32)) * ya
           + _sigmoid(gate_ref[:, d:2 * d].astype(F32)) * yb
           + _sigmoid(gate_ref[:, 2 * d:3 * d].astype(F32)) * yc)
    out_ref[...] = x_ref[...] + jnp.dot(mix.astype(BF16), wo_ref[...], preferred_element_type=F32)


def _combine(x, gla_o, mla_o, dil_o, dil_lse, proj, p, batch, seq, tm=256):
    m = x.shape[0]
    d = D_MODEL
    ns = PERM_TILE // tm
    rowblk = lambda i: (i, 0)
    const = lambda i: (0, 0)
    wspec = pl.BlockSpec((d, d), const)

    def residue_major(arr, group, width):
        dil = DIL_GROUPS[group][1]
        view = arr.reshape(m // PERM_TILE, dil, PERM_TILE // dil, width)
        return view, pl.BlockSpec((1, dil, tm // dil, width), lambda i: (i // ns, 0, i % ns, 0))

    o_views, o_specs = zip(*[residue_major(dil_o[g], g, d) for g in range(len(DIL_GROUPS))])
    l_views, l_specs = zip(*[residue_major(dil_lse[g], g, LANES) for g in range(len(DIL_GROUPS))])
    n_perm = len(DIL_GROUPS) - 1
    return pl.pallas_call(
        _combine_kernel,
        grid=(m // tm,),
        in_specs=[pl.BlockSpec((tm, d), rowblk), pl.BlockSpec((tm, d), rowblk), pl.BlockSpec((tm, d), rowblk),
                  *o_specs, *l_specs,
                  pl.BlockSpec((tm, 3 * d), lambda i: (i, COL_GATE // (3 * d))),
                  wspec, wspec, wspec, wspec],
        out_specs=pl.BlockSpec((tm, d), rowblk),
        out_shape=jax.ShapeDtypeStruct((m, d), F32),
        scratch_shapes=[pltpu.VMEM((tm, d), BF16), pltpu.VMEM((n_perm, DIL_HEADS, tm, LANES), F32),
                        pltpu.VMEM((n_perm, tm, LANES), F32)],
        compiler_params=_cparams("parallel"),
        name="combine",
    )(x, gla_o, mla_o, *o_views, *l_views, proj,
      p["w_branch_a"], p["w_branch_b"], p["w_branch_c"], p["w_out"])


def _xattn_kernel(x_ref, g_ref, wq_ref, qn_ref, kv_ref, kn_ref, wo_ref, out_ref, o_ref):
    x = x_ref[...]
    h = _rms(x, g_ref[...]).astype(BF16)
    q = jnp.dot(h, wq_ref[...], preferred_element_type=F32)
    scale = X_HEAD_DIM ** -0.5 * LOG2_E
    for hd in range(X_HEADS):
        hs = slice(hd * X_HEAD_DIM, (hd + 1) * X_HEAD_DIM)
        vs = slice(D_MODEL + hd * X_HEAD_DIM, D_MODEL + (hd + 1) * X_HEAD_DIM)
        qh = (_rms(q[:, hs], qn_ref[...]) * scale).astype(BF16)
        kh = _rms(kv_ref[:, hs].astype(F32), kn_ref[...]).astype(BF16)
        s = lax.dot_general(qh, kh, NT_DIMS, preferred_element_type=F32)
        p = jnp.exp2(s - jnp.max(s, axis=-1, keepdims=True))
        l = jnp.sum(p, axis=-1, keepdims=True)
        o = jnp.dot(p.astype(BF16), kv_ref[:, vs], preferred_element_type=F32) / l
        o_ref[:, hs] = o.astype(BF16)
    out_ref[...] = x + jnp.dot(o_ref[...], wo_ref[...], preferred_element_type=F32)


def _xattn(x, kv, p, seq, tm=1024):
    m = x.shape[0]
    d = D_MODEL
    per_seq = seq // tm
    const = lambda i: (0, 0)
    return pl.pallas_call(
        _xattn_kernel,
        grid=(m // tm,),
        in_specs=[pl.BlockSpec((tm, d), lambda i: (i, 0)),
                  pl.BlockSpec((1, d), const),
                  pl.BlockSpec((d, d), const),
                  pl.BlockSpec((1, X_HEAD_DIM), const),
                  pl.BlockSpec((N_MEM, 2 * d), lambda i: (i // per_seq, 0)),
                  pl.BlockSpec((1, X_HEAD_DIM), const),
                  pl.BlockSpec((d, d), const)],
        out_specs=pl.BlockSpec((tm, d), lambda i: (i, 0)),
        out_shape=jax.ShapeDtypeStruct((m, d), F32),
        scratch_shapes=[pltpu.VMEM((tm, d), BF16)],
        compiler_params=_cparams("parallel"),
        name="xattn",
    )(x, p["norm_xattn"], p["w_xq"], p["xq_norm"], kv, p["xk_norm"], p["w_xo"])


def _ffn_kernel(x_ref, g_ref, w1_ref, w2_ref, out_ref, h_ref, acc_ref):
    j = pl.program_id(1)

    @pl.when(j == 0)
    def _():
        h_ref[...] = _rms(x_ref[...], g_ref[...]).astype(BF16)
        acc_ref[...] = x_ref[...]

    u = jnp.maximum(jnp.dot(h_ref[...], w1_ref[...], preferred_element_type=F32), 0.0)
    acc_ref[...] += jnp.dot((u * u).astype(BF16), w2_ref[...], preferred_element_type=F32)

    @pl.when(j == pl.num_programs(1) - 1)
    def _():
        out_ref[...] = acc_ref[...]


def _ffn(x, p, tm=1024, tf=1024):
    m = x.shape[0]
    d = D_MODEL
    return pl.pallas_call(
        _ffn_kernel,
        grid=(m // tm, D_FF // tf),
        in_specs=[pl.BlockSpec((tm, d), lambda i, j: (i, 0)),
                  pl.BlockSpec((1, d), lambda i, j: (0, 0)),
                  pl.BlockSpec((d, tf), lambda i, j: (0, j)),
                  pl.BlockSpec((tf, d), lambda i, j: (j, 0))],
        out_specs=pl.BlockSpec((tm, d), lambda i, j: (i, 0)),
        out_shape=jax.ShapeDtypeStruct((m, d), F32),
        scratch_shapes=[pltpu.VMEM((tm, d), BF16), pltpu.VMEM((tm, d), F32)],
        compiler_params=_cparams("parallel", "arbitrary"),
        name="ffn",
    )(x, p["norm_ffn"], p["w_ff1"], p["w_ff2"])


def _rope_tables(positions):
    seq = len(positions)
    inv_freq = 1.0 / (ROPE_THETA ** (np.arange(0, 2 * ROPE_HALF, 2, dtype=np.float64) / (2 * ROPE_HALF)))
    ang = np.asarray(positions, np.float64)[:, None] * inv_freq[None, :]
    cos, sin = jnp.asarray(np.cos(ang), F32), jnp.asarray(np.sin(ang), F32)
    gap = ROPE_HI - ROPE_LO - ROPE_HALF
    tail = LANES - ROPE_HI - ROPE_HALF
    fill = lambda n, v: jnp.full((seq, n), v, F32)
    cos_t = jnp.concatenate([fill(ROPE_LO, 1.0), cos, fill(gap, 1.0), cos, fill(tail, 1.0)], axis=1)
    sin_t = jnp.concatenate([fill(ROPE_LO, 0.0), -sin, fill(gap, 0.0), sin, fill(tail, 0.0)], axis=1)
    return cos_t, sin_t


def _lanes_after(*taken):
    used = set(int(i) for t in taken for i in t)
    return [i for i in range(LANES) if i not in used]


_ROPE_LANES = list(range(ROPE_LO, ROPE_LO + ROPE_HALF)) + list(range(ROPE_HI, ROPE_HI + ROPE_HALF))
_MLA_LANE_OF_DIM = np.array(_lanes_after(_ROPE_LANES)[:MLA_NOPE] + _ROPE_LANES, np.int32)
_DIL_LANE_OF_DIM = np.array(_ROPE_LANES + _lanes_after(_ROPE_LANES), np.int32)


def _to_lanes(arr, lane_of_dim):
    dim_of_lane = np.full(LANES, -1)
    dim_of_lane[lane_of_dim] = np.arange(len(lane_of_dim))
    pieces, lane = [], 0
    while lane < LANES:
        run = 1
        while (lane + run < LANES and
               (dim_of_lane[lane + run] == dim_of_lane[lane] + run if dim_of_lane[lane] >= 0
                else dim_of_lane[lane + run] < 0)):
            run += 1
        start = int(dim_of_lane[lane])
        pieces.append(arr[..., start:start + run] if start >= 0
                      else jnp.zeros(arr.shape[:-1] + (run,), arr.dtype))
        lane += run
    return jnp.concatenate(pieces, axis=-1)


def _prep_layer(w):
    d = D_MODEL
    hk = GLA_HEADS * GLA_DK
    hv = GLA_HEADS * GLA_DV
    dil_w = len(DIL_GROUPS) * DIL_HEADS * DIL_HEAD_DIM
    sizes = (hk, hk, hv, hv, GLA_GATE_RANK, GLA_GATE_RANK, MLA_Q_RANK, MLA_KV_RANK, MLA_ROPE,
             dil_w, dil_w, dil_w, 3 * d)
    offs = [0]
    for s in sizes:
        offs.append(offs[-1] + s)
    (a_q, a_k, a_v, a_r, a_gf, a_gb, b_q, b_kv, b_kpe, c_q, c_k, c_v, gates) = [
        w["w_in"][:, offs[i]:offs[i + 1]] for i in range(len(sizes))]
    z = lambda n: jnp.zeros((d, n), F32)
    small_lane_of_dim = np.concatenate([_MLA_LANE_OF_DIM[MLA_NOPE:], SMALL_GF + np.arange(GLA_GATE_RANK),
                                        SMALL_GB + np.arange(GLA_GATE_RANK)])
    small = _to_lanes(jnp.concatenate([b_kpe, a_gf, a_gb], axis=1), small_lane_of_dim)

    def dil_heads(cols):
        per_head = cols.reshape(d, len(DIL_GROUPS) * DIL_HEADS, DIL_HEAD_DIM)
        return _to_lanes(per_head, _DIL_LANE_OF_DIM).reshape(d, dil_w)

    gw = DIL_HEADS * DIL_HEAD_DIM
    dil_tiles = [t[:, g * gw:(g + 1) * gw] for g in range(len(DIL_GROUPS))
                 for t in (dil_heads(c_q), dil_heads(c_k), c_v)]
    w_in = jnp.concatenate(dil_tiles + [gates, a_q, a_k, a_v, a_r, b_q, b_kv, small,
                                        z(IN_COLS_PAD - COL_SMALL - LANES)], axis=1).astype(BF16)

    def gate_w(wg, lane0):
        rows = lambda n: jnp.zeros((n, hk), F32)
        return jnp.concatenate([rows(lane0), wg, rows(LANES - lane0 - GLA_GATE_RANK)], axis=0).astype(BF16)

    wgf = gate_w(w["gla_w_gate_f"], SMALL_GF)
    wgb = gate_w(w["gla_w_gate_b"], SMALL_GB)

    w_uq = _to_lanes(w["mla_w_uq"].reshape(MLA_Q_RANK, MLA_HEADS, MLA_QK), _MLA_LANE_OF_DIM)
    w_uq = w_uq.reshape(MLA_Q_RANK, MLA_HEADS * LANES)
    w_ukv = w["mla_w_ukv"].reshape(MLA_KV_RANK, MLA_HEADS, MLA_NOPE + MLA_V)
    w_uk = _to_lanes(w_ukv[:, :, :MLA_NOPE], _MLA_LANE_OF_DIM[:MLA_NOPE])
    w_uk = w_uk.reshape(MLA_KV_RANK, MLA_HEADS * LANES)
    w_uv = w_ukv[:, :, MLA_NOPE:].reshape(MLA_KV_RANK, MLA_HEADS * MLA_V)
    row = lambda v: v.reshape(1, -1).astype(F32)
    return {
        "norm_mix": row(w["norm_mix"]), "w_in": w_in,
        "gla_wgf": wgf, "gla_bgf": row(w["gla_b_gate_f"]),
        "gla_wgb": wgb, "gla_bgb": row(w["gla_b_gate_b"]),
        "gla_out_norm": row(w["gla_out_norm"]), "w_branch_a": w["w_branch_a"].astype(BF16),
        "mla_q_lat_norm": row(w["mla_q_lat_norm"]), "mla_w_uq": w_uq.astype(BF16),
        "mla_kv_lat_norm": row(w["mla_kv_lat_norm"]), "mla_w_uk": w_uk.astype(BF16),
        "mla_w_uv": w_uv.astype(BF16),
        "mla_q_norm": _to_lanes(w["mla_q_norm"], _MLA_LANE_OF_DIM).reshape(1, LANES),
        "mla_k_norm": _to_lanes(w["mla_k_norm"], _MLA_LANE_OF_DIM).reshape(1, LANES),
        "w_branch_b": w["w_branch_b"].astype(BF16),
        "dil_qk_norm": _to_lanes(jnp.stack([w["dil_q_norm"], w["dil_k_norm"]]),
                                 _DIL_LANE_OF_DIM).reshape(2, 1, LANES),
        "w_branch_c": w["w_branch_c"].astype(BF16), "w_out": w["w_out"].astype(BF16),
        "norm_xattn": row(w["norm_xattn"]), "norm_mem": row(w["norm_mem"]),
        "w_xq": w["w_xq"].astype(BF16), "w_xkv": w["w_xkv"].astype(BF16),
        "xq_norm": row(w["xq_norm"]), "xk_norm": row(w["xk_norm"]), "w_xo": w["w_xo"].astype(BF16),
        "norm_ffn": row(w["norm_ffn"]), "w_ff1": w["w_ff1"].astype(BF16), "w_ff2": w["w_ff2"].astype(BF16),
    }


def _layer(x, mem, p, tables, batch, seq):
    (cos_m, sin_m), dil_tables = tables
    proj = _in_proj(x, p, dil_tables, seq)
    o_fwd = _gla_direction(proj, p["gla_wgf"], p["gla_bgf"], batch, seq, False)
    gla_o = _gla_direction(proj, p["gla_wgb"], p["gla_bgb"], batch, seq, True,
                           o_fwd=o_fwd, g_out=p["gla_out_norm"])
    q_m, k_m, v_m = _mla_prep(proj, p, cos_m, sin_m, seq)
    mla_o = _mla_attn(q_m, k_m, v_m, batch, seq)
    dil = [_dil_attn(proj, g, batch, seq) for g in range(len(DIL_GROUPS))]
    x = _combine(x, gla_o, mla_o, [o for o, _ in dil], [l for _, l in dil], proj, p, batch, seq)
    kv = _norm_matmul(mem, p["norm_mem"], p["w_xkv"], BF16, tm=N_MEM, tn=1024)
    x = _xattn(x, kv, p, seq)
    return _ffn(x, p)


def _trunk(x, mem, layers):
    batch, seq, d = x.shape
    natural = np.arange(seq)
    row_pos = [natural.reshape(seq // PERM_TILE, PERM_TILE // dil, dil).transpose(0, 2, 1).reshape(seq)
               for _, dil in DIL_GROUPS]
    dil_cos, dil_sin = zip(*[_rope_tables(pos) for pos in row_pos])
    tables = (_rope_tables(natural), (jnp.stack(dil_cos), jnp.stack(dil_sin)))
    xf = x.reshape(batch * seq, d)
    memf = mem.reshape(batch * mem.shape[1], d)
    for p in layers:
        xf = _layer(xf, memf, p, tables, batch, seq)
    return xf.reshape(batch, seq, d)


def kernel(x_prompt, x_sample, mem_prompt, mem_sample, norm_mix, w_in, gla_w_gate_f, gla_b_gate_f, gla_w_gate_b, gla_b_gate_b, gla_out_norm, w_branch_a, mla_q_lat_norm, mla_w_uq, mla_kv_lat_norm, mla_w_ukv, mla_q_norm, mla_k_norm, w_branch_b, dil_q_norm, dil_k_norm, w_branch_c, w_out, norm_xattn, norm_mem, w_xq, w_xkv, xq_norm, xk_norm, w_xo, norm_ffn, w_ff1, w_ff2):
    stacked = dict(norm_mix=norm_mix, w_in=w_in, gla_w_gate_f=gla_w_gate_f, gla_b_gate_f=gla_b_gate_f,
                   gla_w_gate_b=gla_w_gate_b, gla_b_gate_b=gla_b_gate_b, gla_out_norm=gla_out_norm,
                   w_branch_a=w_branch_a, mla_q_lat_norm=mla_q_lat_norm, mla_w_uq=mla_w_uq,
                   mla_kv_lat_norm=mla_kv_lat_norm, mla_w_ukv=mla_w_ukv, mla_q_norm=mla_q_norm,
                   mla_k_norm=mla_k_norm, w_branch_b=w_branch_b, dil_q_norm=dil_q_norm, dil_k_norm=dil_k_norm,
                   w_branch_c=w_branch_c, w_out=w_out, norm_xattn=norm_xattn, norm_mem=norm_mem, w_xq=w_xq,
                   w_xkv=w_xkv, xq_norm=xq_norm, xk_norm=xk_norm, w_xo=w_xo, norm_ffn=norm_ffn,
                   w_ff1=w_ff1, w_ff2=w_ff2)
    layers = [_prep_layer({k: v[l] for k, v in stacked.items()}) for l in range(DEPTH)]
    return (_trunk(x_prompt, mem_prompt, layers), _trunk(x_sample, mem_sample, layers))
```

```python
import functools

import jax
import jax.numpy as jnp
import numpy as np
from jax import lax
from jax.experimental import pallas as pl
from jax.experimental.pallas import tpu as pltpu

F32 = jnp.float32
BF16 = jnp.bfloat16

D_MODEL = 1024
DEPTH = 2
N_MEM = 256
ROPE_THETA = 500000.0
NORM_EPS = 1e-6
GLA_HEADS = 4
GLA_DK = 128
GLA_DV = 256
GLA_GATE_RANK = 16
GLA_TAU = 16.0
GLA_CHUNK = 64
MLA_HEADS = 8
MLA_Q_RANK = 256
MLA_KV_RANK = 128
MLA_NOPE = 64
MLA_ROPE = 32
MLA_QK = 96
MLA_V = 128
DIL_GROUPS = ((128, 1), (512, 4), (2048, 16))
DIL_HEADS = 8
DIL_HEAD_DIM = 128
DIL_ROT = 32
DIL_RADIUS = 64
X_HEADS = 4
X_HEAD_DIM = 256
D_FF = 4096

LANES = 128
ROPE_HALF = 16
VMEM_LIMIT_BYTES = 56 * 1024 * 1024

COL_GATE = 9216
PERM_TILE = 1024
HALO = 64
QK_ROWS = 256
COL_AQ, COL_AK, COL_AV, COL_AR = 12288, 12800, 13312, 14336
COL_BQ, COL_BKV, COL_SMALL = 15360, 15616, 15744
IN_COLS_PAD = 16384
SMALL_GF, SMALL_GB = 16, 32
ROPE_LO, ROPE_HI = 0, 64

NT_DIMS = (((1,), (1,)), ((), ()))
LOG2_E = 1.4426950408889634


def _cparams(*sem):
    return pltpu.CompilerParams(dimension_semantics=sem, vmem_limit_bytes=VMEM_LIMIT_BYTES)


def _rms(x, g, n=None):
    ss = jnp.sum(x * x, axis=-1, keepdims=True) * (1.0 / (n or x.shape[-1]))
    return x * lax.rsqrt(ss + NORM_EPS) * g


def _rms_head_mxu(x, g, n):
    ones = jnp.ones((LANES, LANES), BF16)
    ss = jnp.dot((x * x).astype(BF16), ones, preferred_element_type=F32)
    return x * lax.rsqrt(ss * (1.0 / n) + NORM_EPS) * g


def _log_sigmoid(z):
    return jnp.minimum(z, 0.0) - jnp.log(1.0 + jnp.exp(-jnp.abs(z)))


def _sigmoid(z):
    return 1.0 / (1.0 + jnp.exp(-z))


def _norm_matmul_kernel(x_ref, g_ref, w_ref, o_ref, h_ref):
    @pl.when(pl.program_id(1) == 0)
    def _():
        h_ref[...] = _rms(x_ref[...].astype(F32), g_ref[...]).astype(BF16)

    o_ref[...] = jnp.dot(h_ref[...], w_ref[...], preferred_element_type=F32).astype(o_ref.dtype)


def _norm_matmul(x, g, w, out_dtype, tm, tn):
    m, k = x.shape
    n = w.shape[1]
    return pl.pallas_call(
        _norm_matmul_kernel,
        grid=(m // tm, n // tn),
        in_specs=[pl.BlockSpec((tm, k), lambda i, j: (i, 0)),
                  pl.BlockSpec((1, k), lambda i, j: (0, 0)),
                  pl.BlockSpec((k, tn), lambda i, j: (0, j))],
        out_specs=pl.BlockSpec((tm, tn), lambda i, j: (i, j)),
        out_shape=jax.ShapeDtypeStruct((m, n), out_dtype),
        scratch_shapes=[pltpu.VMEM((tm, k), BF16)],
        compiler_params=_cparams("parallel", "arbitrary"),
        name="norm_matmul",
    )(x, g, w)


def _dil_tile(comp, group):
    return len(DIL_GROUPS) * group + comp


def _in_proj_kernel(x_ref, g_ref, w_ref, nw_ref, cos_ref, sin_ref, o_ref, h_ref, hf_ref):
    j = pl.program_id(1)
    n_dil_tiles = 3 * len(DIL_GROUPS)
    wide = DIL_HEADS * DIL_HEAD_DIM
    nc = D_MODEL // LANES

    @pl.when(j == 0)
    def _():
        h = _rms(x_ref[...], g_ref[...])
        h_ref[0] = h.astype(BF16)
        for c in range(nc):
            hf_ref[c] = h[:, c * LANES:(c + 1) * LANES]
        for g, (_, dil) in enumerate(DIL_GROUPS):
            if dil == 1:
                continue
            rows = PERM_TILE // dil
            for r in range(dil):
                for c in range(nc):
                    h_ref[g, r * rows:(r + 1) * rows, c * LANES:(c + 1) * LANES] = (
                        hf_ref[c, pl.ds(r, rows, stride=dil), :].astype(BF16))

    is_dil = j < n_dil_tiles
    group = jnp.where(is_dil, j // 3, 0)
    comp = j % 3
    is_qk = is_dil & (comp < 2)

    @pl.when(is_qk)
    def _():
        nw = nw_ref[0]
        scale = jnp.where(comp == 0, DIL_HEAD_DIM ** -0.5 * LOG2_E, 1.0).astype(F32)
        for c in range(PERM_TILE // QK_ROWS):
            rs = slice(c * QK_ROWS, (c + 1) * QK_ROWS)
            acc = jnp.dot(h_ref[group, rs, :], w_ref[...], preferred_element_type=F32)
            cos = cos_ref[0, rs, :]
            sin = sin_ref[0, rs, :]
            for h in range(wide // LANES):
                hs = slice(h * LANES, (h + 1) * LANES)
                y = _rms_head_mxu(acc[:, hs], nw, DIL_HEAD_DIM)
                o_ref[rs, hs] = (_rope(y, cos, sin) * scale).astype(o_ref.dtype)

    @pl.when(jnp.logical_not(is_qk))
    def _():
        o_ref[...] = jnp.dot(h_ref[group], w_ref[...], preferred_element_type=F32).astype(o_ref.dtype)


def _in_proj(x, p, tables, seq):
    m, k = x.shape
    tn = DIL_HEADS * DIL_HEAD_DIM
    ng = len(DIL_GROUPS)
    per_seq = seq // PERM_TILE
    cos, sin = tables
    group_of = lambda j: jnp.where(j < 3 * ng, j // 3, 0)
    table_spec = pl.BlockSpec((1, PERM_TILE, LANES), lambda i, j: (group_of(j), i % per_seq, 0))
    return pl.pallas_call(
        _in_proj_kernel,
        grid=(m // PERM_TILE, IN_COLS_PAD // tn),
        in_specs=[pl.BlockSpec((PERM_TILE, k), lambda i, j: (i, 0)),
                  pl.BlockSpec((1, k), lambda i, j: (0, 0)),
                  pl.BlockSpec((k, tn), lambda i, j: (0, j)),
                  pl.BlockSpec((1, 1, LANES), lambda i, j: (jnp.minimum(j % 3, 1), 0, 0)),
                  table_spec, table_spec],
        out_specs=pl.BlockSpec((PERM_TILE, tn), lambda i, j: (i, j)),
        out_shape=jax.ShapeDtypeStruct((m, IN_COLS_PAD), BF16),
        scratch_shapes=[pltpu.VMEM((ng, PERM_TILE, k), BF16),
                        pltpu.VMEM((k // LANES, PERM_TILE, LANES), F32)],
        compiler_params=_cparams("parallel", "arbitrary"),
        name="in_proj",
    )(x, p["norm_mix"], p["w_in"], p["dil_qk_norm"], cos, sin)


TN_DIMS = (((0,), (0,)), ((), ()))


def _gla_kernel(*refs, tb, reverse, final):
    if final:
        q_ref, k_ref, v_ref, sm_ref, wg_ref, bg_ref, of_ref, r_ref, gn_ref, o_ref, st_ref = refs
    else:
        q_ref, k_ref, v_ref, sm_ref, wg_ref, bg_ref, o_ref, st_ref = refs

    @pl.when(pl.program_id(1) == 0)
    def _():
        st_ref[...] = jnp.zeros_like(st_ref)

    ck = GLA_CHUNK
    la = _log_sigmoid(jnp.dot(sm_ref[...], wg_ref[...], preferred_element_type=F32) + bg_ref[...]) * (1.0 / GLA_TAU)

    r = lax.broadcasted_iota(jnp.int32, (ck, ck), 0)
    c = lax.broadcasted_iota(jnp.int32, (ck, ck), 1)
    tri_mask = (r <= c) if reverse else (r >= c)
    tri = jnp.where(tri_mask, 1.0, 0.0).astype(BF16)
    end = 0 if reverse else ck - 1
    scale = GLA_DK ** -0.5

    n_chunks = tb // ck
    order = range(n_chunks - 1, -1, -1) if reverse else range(n_chunks)
    for ci in order:
        sl = slice(ci * ck, (ci + 1) * ck)
        la_c = la[sl]
        hi = la_c.astype(BF16)
        lo = (la_c - hi.astype(F32)).astype(BF16)
        b = jnp.dot(tri, hi, preferred_element_type=F32) + jnp.dot(tri, lo, preferred_element_type=F32)
        b_end = b[end:end + 1, :]
        kc = k_ref[sl, :].astype(F32)
        qd = (q_ref[sl, :].astype(F32) * scale * jnp.exp(b)).astype(BF16)
        kinv = (kc * jnp.exp(-b)).astype(BF16)
        ktail = (kc * jnp.exp(b_end - b)).astype(BF16)
        dec = jnp.exp(b_end)
        for h in range(GLA_HEADS):
            ks = slice(h * GLA_DK, (h + 1) * GLA_DK)
            vs = slice(h * GLA_DV, (h + 1) * GLA_DV)
            qh = qd[:, ks]
            att = lax.dot_general(qh, kinv[:, ks], NT_DIMS, preferred_element_type=F32)
            att = jnp.where(tri_mask, att, 0.0).astype(BF16)
            vh = v_ref[sl, vs]
            st = st_ref[h]
            o = (jnp.dot(att, vh, preferred_element_type=F32)
                 + lax.dot_general(qh, st.astype(BF16), NT_DIMS, preferred_element_type=F32))
            st_ref[h] = dec[:, ks] * st + lax.dot_general(vh, ktail[:, ks], TN_DIMS, preferred_element_type=F32)
            if final:
                o = _rms(o + of_ref[sl, vs], gn_ref[...])
                rr = r_ref[sl, vs].astype(F32)
                o = o * (rr * _sigmoid(rr))
            o_ref[sl, vs] = o.astype(o_ref.dtype)


def _gla_direction(proj, wg, bg, batch, seq, reverse, o_fwd=None, g_out=None, tb=256):
    nb = seq // tb
    m = batch * seq
    final = o_fwd is not None

    def row(b, i):
        return b * nb + (nb - 1 - i if reverse else i)

    def col(block_cols, off):
        return lambda b, i: (row(b, i), off // block_cols)

    hk = GLA_HEADS * GLA_DK
    hv = GLA_HEADS * GLA_DV
    const = lambda b, i: (0, 0)
    in_specs = [pl.BlockSpec((tb, hk), col(hk, COL_AQ)),
                pl.BlockSpec((tb, hk), col(hk, COL_AK)),
                pl.BlockSpec((tb, hv), col(hv, COL_AV)),
                pl.BlockSpec((tb, LANES), col(LANES, COL_SMALL)),
                pl.BlockSpec((LANES, hk), const),
                pl.BlockSpec((1, hk), const)]
    args = [proj, proj, proj, proj, wg, bg]
    if final:
        in_specs += [pl.BlockSpec((tb, hv), col(hv, 0)),
                     pl.BlockSpec((tb, hv), col(hv, COL_AR)),
                     pl.BlockSpec((1, GLA_DV), const)]
        args += [o_fwd, proj, g_out]
    return pl.pallas_call(
        functools.partial(_gla_kernel, tb=tb, reverse=reverse, final=final),
        grid=(batch, nb),
        in_specs=in_specs,
        out_specs=pl.BlockSpec((tb, hv), col(hv, 0)),
        out_shape=jax.ShapeDtypeStruct((m, hv), BF16 if final else F32),
        scratch_shapes=[pltpu.VMEM((GLA_HEADS, GLA_DV, GLA_DK), F32)],
        compiler_params=_cparams("parallel", "arbitrary"),
        name="gla_bwd" if reverse else "gla_fwd",
    )(*args)


def _rope(y, cos, sin_signed):
    return y * cos + pltpu.roll(y, LANES // 2, 1) * sin_signed


def _mla_prep_kernel(bq_ref, bkv_ref, sm_ref, qln_ref, wuq_ref, kvln_ref, wuk_ref, wuv_ref,
                     qn_ref, kn_ref, cos_ref, sin_ref, q_out, k_out, v_out):
    hq = _rms(bq_ref[...].astype(F32), qln_ref[...]).astype(BF16)
    hkv = _rms(bkv_ref[...].astype(F32), kvln_ref[...]).astype(BF16)
    q = jnp.dot(hq, wuq_ref[...], preferred_element_type=F32)
    kn = jnp.dot(hkv, wuk_ref[...], preferred_element_type=F32)
    v = jnp.dot(hkv, wuv_ref[...], preferred_element_type=F32)
    tm = q.shape[0]
    lane = lax.broadcasted_iota(jnp.int32, (tm, LANES), 1)
    sm = sm_ref[...].astype(F32)
    in_rope = (lane < ROPE_LO + ROPE_HALF) | ((lane >= ROPE_HI) & (lane < ROPE_HI + ROPE_HALF))
    kpe = jnp.where(in_rope, sm, 0.0)
    cos = cos_ref[...]
    sin = sin_ref[...]
    scale = MLA_QK ** -0.5 * LOG2_E
    ones = jnp.ones((tm, LANES), v_out.dtype)
    for h in range(MLA_HEADS):
        hs = slice(h * LANES, (h + 1) * LANES)
        qh = _rope(_rms_head_mxu(q[:, hs], qn_ref[...], MLA_QK), cos, sin)
        q_out[:, hs] = (qh * scale).astype(q_out.dtype)
        kh = _rope(_rms_head_mxu(kn[:, hs] + kpe, kn_ref[...], MLA_QK), cos, sin)
        k_out[:, hs] = kh.astype(k_out.dtype)
        v_out[:, 2 * h * LANES:(2 * h + 1) * LANES] = v[:, hs].astype(v_out.dtype)
        v_out[:, (2 * h + 1) * LANES:(2 * h + 2) * LANES] = ones


def _mla_prep(proj, p, cos, sin, seq, tm=512):
    m = proj.shape[0]
    ns = seq // tm
    const = lambda i: (0, 0)
    wide = MLA_HEADS * LANES
    out = jax.ShapeDtypeStruct((m, wide), BF16)
    return pl.pallas_call(
        _mla_prep_kernel,
        grid=(m // tm,),
        in_specs=[pl.BlockSpec((tm, MLA_Q_RANK), lambda i: (i, COL_BQ // MLA_Q_RANK)),
                  pl.BlockSpec((tm, MLA_KV_RANK), lambda i: (i, COL_BKV // MLA_KV_RANK)),
                  pl.BlockSpec((tm, LANES), lambda i: (i, COL_SMALL // LANES)),
                  pl.BlockSpec((1, MLA_Q_RANK), const),
                  pl.BlockSpec((MLA_Q_RANK, wide), const),
                  pl.BlockSpec((1, MLA_KV_RANK), const),
                  pl.BlockSpec((MLA_KV_RANK, wide), const),
                  pl.BlockSpec((MLA_KV_RANK, wide), const),
                  pl.BlockSpec((1, LANES), const),
                  pl.BlockSpec((1, LANES), const),
                  pl.BlockSpec((tm, LANES), lambda i: (i % ns, 0)),
                  pl.BlockSpec((tm, LANES), lambda i: (i % ns, 0))],
        out_specs=[pl.BlockSpec((tm, wide), lambda i: (i, 0)), pl.BlockSpec((tm, wide), lambda i: (i, 0)),
                   pl.BlockSpec((tm, 2 * wide), lambda i: (i, 0))],
        out_shape=[out, out, jax.ShapeDtypeStruct((m, 2 * wide), BF16)],
        compiler_params=_cparams("parallel"),
        name="mla_prep",
    )(proj, proj, proj, p["mla_q_lat_norm"], p["mla_w_uq"], p["mla_kv_lat_norm"], p["mla_w_uk"],
      p["mla_w_uv"], p["mla_q_norm"], p["mla_k_norm"], cos, sin)


def _mla_attn_kernel(q_ref, k_ref, v_ref, o_ref, *, tk, unroll, sub):
    nsub = q_ref.shape[0] // sub
    nk = k_ref.shape[0] // tk
    qs = [q_ref[i * sub:(i + 1) * sub, :] for i in range(nsub)]

    def body(t, carry):
        start = pl.multiple_of(t * tk, tk)
        out = []
        for q, (m, acc) in zip(qs, carry):
            s = lax.dot_general(q, k_ref[pl.ds(start, tk), :], NT_DIMS, preferred_element_type=F32)
            m_new = jnp.maximum(m, jnp.max(s, axis=-1, keepdims=True))
            p = jnp.exp2(s - m_new).astype(BF16)
            acc = jnp.exp2(m - m_new) * acc + jnp.dot(p, v_ref[pl.ds(start, tk), :], preferred_element_type=F32)
            out.append((m_new, acc))
        return tuple(out)

    init = tuple((jnp.full((sub, 1), -jnp.inf, F32), jnp.zeros((sub, 2 * LANES), F32)) for _ in range(nsub))
    final = lax.fori_loop(0, nk, body, init, unroll=unroll)
    for i, (_, acc) in enumerate(final):
        o_ref[i * sub:(i + 1) * sub, :] = (acc[:, :LANES] / acc[:, LANES:]).astype(o_ref.dtype)


MLA_KV_TILE = 256
MLA_TILES_IN_FLIGHT = 64
MLA_SUB_ROWS = (256, 512)


def _mla_attn(q, k, v, batch, seq, tk=MLA_KV_TILE):
    unroll = min(MLA_TILES_IN_FLIGHT, seq // tk)
    tq = min(seq, MLA_SUB_ROWS[0] * (MLA_TILES_IN_FLIGHT // unroll))
    sub = min(tq, MLA_SUB_ROWS[unroll < MLA_TILES_IN_FLIGHT])
    nq = seq // tq
    return pl.pallas_call(
        functools.partial(_mla_attn_kernel, tk=tk, unroll=unroll, sub=sub),
        grid=(batch, MLA_HEADS, nq),
        in_specs=[pl.BlockSpec((tq, LANES), lambda b, h, i: (b * nq + i, h)),
                  pl.BlockSpec((seq, LANES), lambda b, h, i: (b, h)),
                  pl.BlockSpec((seq, 2 * LANES), lambda b, h, i: (b, h))],
        out_specs=pl.BlockSpec((tq, LANES), lambda b, h, i: (b * nq + i, h)),
        out_shape=jax.ShapeDtypeStruct(q.shape, BF16),
        compiler_params=_cparams("parallel", "parallel", "arbitrary"),
        name="mla_attn",
    )(q, k, v)


DIL_SUB_BLOCKS = 2
DIL_MAX_Q_BLOCKS = 16


def _dil_attn_kernel(*refs, sub_len, q_blocks):
    k_blocks = q_blocks + 2
    q_ref = refs[0]
    k_refs = refs[1:1 + k_blocks]
    v_refs = refs[1 + k_blocks:1 + 2 * k_blocks]
    o_ref, lse_ref = refs[1 + 2 * k_blocks:]
    piece_rows = q_ref.shape[1]
    tl = DIL_SUB_BLOCKS * HALO
    nkeys = (DIL_SUB_BLOCKS + 2) * HALO

    def row_runs(start, n):
        runs = []
        while n > 0:
            off = start % piece_rows
            take = min(n, piece_rows - off)
            runs.append((start // piece_rows, slice(off, off + take)))
            start, n = start + take, n - take
        return runs

    row = lax.broadcasted_iota(jnp.int32, (tl, nkeys), 0)
    col = lax.broadcasted_iota(jnp.int32, (tl, nkeys), 1)
    rel = col - row
    in_band = (rel >= 0) & (rel <= 2 * DIL_RADIUS)
    lane = lax.broadcasted_iota(jnp.int32, (tl, LANES), 1)
    for sub in range(q_blocks // DIL_SUB_BLOCKS):
        q_runs = row_runs(sub * tl, tl)
        pos = (pl.program_id(2) * q_blocks + sub * DIL_SUB_BLOCKS - 1) * HALO + col
        valid = in_band & (pos >= 0) & (pos < sub_len)
        kb = range(sub * DIL_SUB_BLOCKS, sub * DIL_SUB_BLOCKS + DIL_SUB_BLOCKS + 2)
        lse_all = jnp.zeros((tl, LANES), F32)
        for h in range(DIL_HEADS):
            hs = slice(h * LANES, (h + 1) * LANES)
            qq = jnp.concatenate([q_ref[p, rs, hs] for p, rs in q_runs], axis=0)
            kk = jnp.concatenate([k_refs[n][:, hs] for n in kb], axis=0)
            vv = jnp.concatenate([v_refs[n][:, hs] for n in kb], axis=0)
            s = lax.dot_general(qq, kk, NT_DIMS, preferred_element_type=F32)
            s = jnp.where(valid, s, -jnp.inf)
            m = jnp.max(s, axis=-1, keepdims=True)
            p = jnp.exp2(s - m)
            l = jnp.sum(p, axis=-1, keepdims=True)
            o = jnp.dot(p.astype(BF16), vv, preferred_element_type=F32) / l
            done = 0
            for pc, rs in q_runs:
                o_ref[pc, rs, hs] = o[done:done + rs.stop - rs.start].astype(o_ref.dtype)
                done += rs.stop - rs.start
            lse_all = jnp.where(lane == h, m + jnp.log2(l), lse_all)
        done = 0
        for pc, rs in q_runs:
            lse_ref[pc, rs, :] = lse_all[done:done + rs.stop - rs.start]
            done += rs.stop - rs.start


def _dil_attn(proj, group, batch, seq):
    _, dil = DIL_GROUPS[group]
    sub_len = seq // dil
    n_blocks = sub_len // HALO
    chunk_blocks = PERM_TILE // dil // HALO
    tile_blocks = PERM_TILE // HALO
    wide = DIL_HEADS * DIL_HEAD_DIM

    def row_block(b, r, n):
        n = jnp.clip(n, 0, n_blocks - 1)
        return (b * (seq // HALO) + (n // chunk_blocks) * tile_blocks + r * chunk_blocks + n % chunk_blocks)

    q_blocks = min(DIL_MAX_Q_BLOCKS, n_blocks)
    k_blocks = q_blocks + 2

    def halo_spec(comp, offset):
        return pl.BlockSpec((HALO, wide), lambda b, r, i: (row_block(b, r, q_blocks * i + offset),
                                                          _dil_tile(comp, group)))

    tl = q_blocks * HALO
    chunk = PERM_TILE // dil
    pieces = max(1, tl // chunk)
    piece_rows = tl // pieces
    subs = chunk // piece_rows
    m = batch * seq
    lead = m // (PERM_TILE * pieces)
    per_seq = seq // (PERM_TILE * pieces)

    def query_view(width):
        return (lead, pieces, dil, chunk, width)

    def query_spec(width, col_block):
        return pl.BlockSpec((None, pieces, None, piece_rows, width),
                            lambda b, r, i: (b * per_seq + i // subs, 0, r, i % subs, col_block))

    o, lse = pl.pallas_call(
        functools.partial(_dil_attn_kernel, sub_len=sub_len, q_blocks=q_blocks),
        grid=(batch, dil, n_blocks // q_blocks),
        in_specs=([query_spec(wide, _dil_tile(0, group))]
                  + [halo_spec(1, n - 1) for n in range(k_blocks)]
                  + [halo_spec(2, n - 1) for n in range(k_blocks)]),
        out_specs=[query_spec(wide, 0), query_spec(LANES, 0)],
        out_shape=[jax.ShapeDtypeStruct(query_view(wide), BF16), jax.ShapeDtypeStruct(query_view(LANES), F32)],
        compiler_params=_cparams("parallel", "parallel", "arbitrary"),
        name=f"dil_attn_g{group}",
    )(proj.reshape(query_view(IN_COLS_PAD)), *([proj] * (2 * k_blocks)))
    return o.reshape(m, wide), lse.reshape(m, LANES)


def _combine_kernel(x_ref, a_ref, b_ref, o0_ref, o1_ref, o2_ref, l0_ref, l1_ref, l2_ref, gate_ref,
                    wa_ref, wb_ref, wc_ref, wo_ref, out_ref, c_ref, on_refs, ln_refs):
    tm = x_ref.shape[0]
    o_tok, l_tok = [], []
    for g, (o_ref, l_ref) in enumerate(((o0_ref, l0_ref), (o1_ref, l1_ref), (o2_ref, l2_ref))):
        dil = DIL_GROUPS[g][1]
        if dil == 1:
            o_tok.append(lambda h, o_ref=o_ref: o_ref[0, 0, :, h * LANES:(h + 1) * LANES].astype(F32))
            l_tok.append(l_ref[0, 0])
            continue
        for r in range(dil):
            rows = pl.ds(r, tm // dil, stride=dil)
            ln_refs[g - 1, rows, :] = l_ref[0, r]
            for h in range(DIL_HEADS):
                on_refs[g - 1, h, rows, :] = o_ref[0, r, :, h * LANES:(h + 1) * LANES].astype(F32)
        o_tok.append(lambda h, g=g: on_refs[g - 1, h])
        l_tok.append(ln_refs[g - 1])
    l0, l1, l2 = l_tok
    mx = jnp.maximum(jnp.maximum(l0, l1), l2)
    e0, e1, e2 = jnp.exp2(l0 - mx), jnp.exp2(l1 - mx), jnp.exp2(l2 - mx)
    inv = 1.0 / (e0 + e1 + e2)
    w0, w1, w2 = e0 * inv, e1 * inv, e2 * inv
    for h in range(DIL_HEADS):
        hs = slice(h * LANES, (h + 1) * LANES)
        c = w0[:, h:h + 1] * o_tok[0](h) + w1[:, h:h + 1] * o_tok[1](h) + w2[:, h:h + 1] * o_tok[2](h)
        c_ref[:, hs] = c.astype(BF16)
    ya = jnp.dot(a_ref[...], wa_ref[...], preferred_element_type=F32)
    yb = jnp.dot(b_ref[...], wb_ref[...], preferred_element_type=F32)
    yc = jnp.dot(c_ref[...], wc_ref[...], preferred_element_type=F32)
    d = D_MODEL
    mix = (_sigmoid(gate_ref[:, 0:d].astype(F32)) * ya
           + _sigmoid(gate_ref[:, d:2 * d].astype(F32)) * yb
           + _sigmoid(gate_ref[:, 2 * d:3 * d].astype(F32)) * yc)
    out_ref[...] = x_ref[...] + jnp.dot(mix.astype(BF16), wo_ref[...], preferred_element_type=F32)


def _combine(x, gla_o, mla_o, dil_o, dil_lse, proj, p, batch, seq, tm=256):
    m = x.shape[0]
    d = D_MODEL
    ns = PERM_TILE // tm
    rowblk = lambda i: (i, 0)
    const = lambda i: (0, 0)
    wspec = pl.BlockSpec((d, d), const)

    def residue_major(arr, group, width):
        dil = DIL_GROUPS[group][1]
        view = arr.reshape(m // PERM_TILE, dil, PERM_TILE // dil, width)
        return view, pl.BlockSpec((1, dil, tm // dil, width), lambda i: (i // ns, 0, i % ns, 0))

    o_views, o_specs = zip(*[residue_major(dil_o[g], g, d) for g in range(len(DIL_GROUPS))])
    l_views, l_specs = zip(*[residue_major(dil_lse[g], g, LANES) for g in range(len(DIL_GROUPS))])
    n_perm = len(DIL_GROUPS) - 1
    return pl.pallas_call(
        _combine_kernel,
        grid=(m // tm,),
        in_specs=[pl.BlockSpec((tm, d), rowblk), pl.BlockSpec((tm, d), rowblk), pl.BlockSpec((tm, d), rowblk),
                  *o_specs, *l_specs,
                  pl.BlockSpec((tm, 3 * d), lambda i: (i, COL_GATE // (3 * d))),
                  wspec, wspec, wspec, wspec],
        out_specs=pl.BlockSpec((tm, d), rowblk),
        out_shape=jax.ShapeDtypeStruct((m, d), F32),
        scratch_shapes=[pltpu.VMEM((tm, d), BF16), pltpu.VMEM((n_perm, DIL_HEADS, tm, LANES), F32),
                        pltpu.VMEM((n_perm, tm, LANES), F32)],
        compiler_params=_cparams("parallel"),
        name="combine",
    )(x, gla_o, mla_o, *o_views, *l_views, proj,
      p["w_branch_a"], p["w_branch_b"], p["w_branch_c"], p["w_out"])


def _xattn_residual(x, g_ref, wq_ref, qn_ref, kv_ref, kn_ref, wo_ref, o_ref):
    h = _rms(x, g_ref[...]).astype(BF16)
    q = jnp.dot(h, wq_ref[...], preferred_element_type=F32)
    scale = X_HEAD_DIM ** -0.5 * LOG2_E
    for hd in range(X_HEADS):
        hs = slice(hd * X_HEAD_DIM, (hd + 1) * X_HEAD_DIM)
        vs = slice(D_MODEL + hd * X_HEAD_DIM, D_MODEL + (hd + 1) * X_HEAD_DIM)
        qh = (_rms(q[:, hs], qn_ref[...]) * scale).astype(BF16)
        kh = _rms(kv_ref[:, hs].astype(F32), kn_ref[...]).astype(BF16)
        s = lax.dot_general(qh, kh, NT_DIMS, preferred_element_type=F32)
        p = jnp.exp2(s - jnp.max(s, axis=-1, keepdims=True))
        l = jnp.sum(p, axis=-1, keepdims=True)
        o = jnp.dot(p.astype(BF16), kv_ref[:, vs], preferred_element_type=F32) / l
        o_ref[:, hs] = o.astype(BF16)
    return x + jnp.dot(o_ref[...], wo_ref[...], preferred_element_type=F32)


def _xattn_ffn_kernel(x_ref, gx_ref, wq_ref, qn_ref, kv_ref, kn_ref, wo_ref, gf_ref, w1_ref, w2_ref,
                      out_ref, h_ref, acc_ref, o_ref):
    j = pl.program_id(1)

    @pl.when(j == 0)
    def _():
        x1 = _xattn_residual(x_ref[...], gx_ref, wq_ref, qn_ref, kv_ref, kn_ref, wo_ref, o_ref)
        acc_ref[...] = x1
        h_ref[...] = _rms(x1, gf_ref[...]).astype(BF16)

    u = jnp.maximum(jnp.dot(h_ref[...], w1_ref[...], preferred_element_type=F32), 0.0)
    acc_ref[...] += jnp.dot((u * u).astype(BF16), w2_ref[...], preferred_element_type=F32)

    @pl.when(j == pl.num_programs(1) - 1)
    def _():
        out_ref[...] = acc_ref[...]


def _xattn_ffn(x, kv, p, seq, tm=1024, tf=1024):
    m = x.shape[0]
    d = D_MODEL
    per_seq = seq // tm
    const = lambda i, j: (0, 0)
    resident = functools.partial(pl.BlockSpec, index_map=const, pipeline_mode=pl.Buffered(1))
    return pl.pallas_call(
        _xattn_ffn_kernel,
        grid=(m // tm, D_FF // tf),
        in_specs=[pl.BlockSpec((tm, d), lambda i, j: (i, 0)),
                  pl.BlockSpec((1, d), const),
                  resident((d, d)),
                  pl.BlockSpec((1, X_HEAD_DIM), const),
                  pl.BlockSpec((N_MEM, 2 * d), lambda i, j: (i // per_seq, 0)),
                  pl.BlockSpec((1, X_HEAD_DIM), const),
                  resident((d, d)),
                  pl.BlockSpec((1, d), const),
                  pl.BlockSpec((d, tf), lambda i, j: (0, j)),
                  pl.BlockSpec((tf, d), lambda i, j: (j, 0))],
        out_specs=pl.BlockSpec((tm, d), lambda i, j: (i, 0)),
        out_shape=jax.ShapeDtypeStruct((m, d), F32),
        scratch_shapes=[pltpu.VMEM((tm, d), BF16), pltpu.VMEM((tm, d), F32), pltpu.VMEM((tm, d), BF16)],
        compiler_params=_cparams("parallel", "arbitrary"),
        name="xattn_ffn",
    )(x, p["norm_xattn"], p["w_xq"], p["xq_norm"], kv, p["xk_norm"], p["w_xo"],
      p["norm_ffn"], p["w_ff1"], p["w_ff2"])


def _rope_tables(positions):
    seq = len(positions)
    inv_freq = 1.0 / (ROPE_THETA ** (np.arange(0, 2 * ROPE_HALF, 2, dtype=np.float64) / (2 * ROPE_HALF)))
    ang = np.asarray(positions, np.float64)[:, None] * inv_freq[None, :]
    cos, sin = jnp.asarray(np.cos(ang), F32), jnp.asarray(np.sin(ang), F32)
    gap = ROPE_HI - ROPE_LO - ROPE_HALF
    tail = LANES - ROPE_HI - ROPE_HALF
    fill = lambda n, v: jnp.full((seq, n), v, F32)
    cos_t = jnp.concatenate([fill(ROPE_LO, 1.0), cos, fill(gap, 1.0), cos, fill(tail, 1.0)], axis=1)
    sin_t = jnp.concatenate([fill(ROPE_LO, 0.0), -sin, fill(gap, 0.0), sin, fill(tail, 0.0)], axis=1)
    return cos_t, sin_t


def _lanes_after(*taken):
    used = set(int(i) for t in taken for i in t)
    return [i for i in range(LANES) if i not in used]


_ROPE_LANES = list(range(ROPE_LO, ROPE_LO + ROPE_HALF)) + list(range(ROPE_HI, ROPE_HI + ROPE_HALF))
_MLA_LANE_OF_DIM = np.array(_lanes_after(_ROPE_LANES)[:MLA_NOPE] + _ROPE_LANES, np.int32)
_DIL_LANE_OF_DIM = np.array(_ROPE_LANES + _lanes_after(_ROPE_LANES), np.int32)


def _to_lanes(arr, lane_of_dim):
    dim_of_lane = np.full(LANES, -1)
    dim_of_lane[lane_of_dim] = np.arange(len(lane_of_dim))
    pieces, lane = [], 0
    while lane < LANES:
        run = 1
        while (lane + run < LANES and
               (dim_of_lane[lane + run] == dim_of_lane[lane] + run if dim_of_lane[lane] >= 0
                else dim_of_lane[lane + run] < 0)):
            run += 1
        start = int(dim_of_lane[lane])
        pieces.append(arr[..., start:start + run] if start >= 0
                      else jnp.zeros(arr.shape[:-1] + (run,), arr.dtype))
        lane += run
    return jnp.concatenate(pieces, axis=-1)


def _prep_layer(w):
    d = D_MODEL
    hk = GLA_HEADS * GLA_DK
    hv = GLA_HEADS * GLA_DV
    dil_w = len(DIL_GROUPS) * DIL_HEADS * DIL_HEAD_DIM
    sizes = (hk, hk, hv, hv, GLA_GATE_RANK, GLA_GATE_RANK, MLA_Q_RANK, MLA_KV_RANK, MLA_ROPE,
             dil_w, dil_w, dil_w, 3 * d)
    offs = [0]
    for s in sizes:
        offs.append(offs[-1] + s)
    (a_q, a_k, a_v, a_r, a_gf, a_gb, b_q, b_kv, b_kpe, c_q, c_k, c_v, gates) = [
        w["w_in"][:, offs[i]:offs[i + 1]] for i in range(len(sizes))]
    z = lambda n: jnp.zeros((d, n), F32)
    small_lane_of_dim = np.concatenate([_MLA_LANE_OF_DIM[MLA_NOPE:], SMALL_GF + np.arange(GLA_GATE_RANK),
                                        SMALL_GB + np.arange(GLA_GATE_RANK)])
    small = _to_lanes(jnp.concatenate([b_kpe, a_gf, a_gb], axis=1), small_lane_of_dim)

    def dil_heads(cols):
        per_head = cols.reshape(d, len(DIL_GROUPS) * DIL_HEADS, DIL_HEAD_DIM)
        return _to_lanes(per_head, _DIL_LANE_OF_DIM).reshape(d, dil_w)

    gw = DIL_HEADS * DIL_HEAD_DIM
    dil_tiles = [t[:, g * gw:(g + 1) * gw] for g in range(len(DIL_GROUPS))
                 for t in (dil_heads(c_q), dil_heads(c_k), c_v)]
    w_in = jnp.concatenate(dil_tiles + [gates, a_q, a_k, a_v, a_r, b_q, b_kv, small,
                                        z(IN_COLS_PAD - COL_SMALL - LANES)], axis=1).astype(BF16)

    def gate_w(wg, lane0):
        rows = lambda n: jnp.zeros((n, hk), F32)
        return jnp.concatenate([rows(lane0), wg, rows(LANES - lane0 - GLA_GATE_RANK)], axis=0).astype(BF16)

    wgf = gate_w(w["gla_w_gate_f"], SMALL_GF)
    wgb = gate_w(w["gla_w_gate_b"], SMALL_GB)

    w_uq = _to_lanes(w["mla_w_uq"].reshape(MLA_Q_RANK, MLA_HEADS, MLA_QK), _MLA_LANE_OF_DIM)
    w_uq = w_uq.reshape(MLA_Q_RANK, MLA_HEADS * LANES)
    w_ukv = w["mla_w_ukv"].reshape(MLA_KV_RANK, MLA_HEADS, MLA_NOPE + MLA_V)
    w_uk = _to_lanes(w_ukv[:, :, :MLA_NOPE], _MLA_LANE_OF_DIM[:MLA_NOPE])
    w_uk = w_uk.reshape(MLA_KV_RANK, MLA_HEADS * LANES)
    w_uv = w_ukv[:, :, MLA_NOPE:].reshape(MLA_KV_RANK, MLA_HEADS * MLA_V)
    row = lambda v: v.reshape(1, -1).astype(F32)
    return {
        "norm_mix": row(w["norm_mix"]), "w_in": w_in,
        "gla_wgf": wgf, "gla_bgf": row(w["gla_b_gate_f"]),
        "gla_wgb": wgb, "gla_bgb": row(w["gla_b_gate_b"]),
        "gla_out_norm": row(w["gla_out_norm"]), "w_branch_a": w["w_branch_a"].astype(BF16),
        "mla_q_lat_norm": row(w["mla_q_lat_norm"]), "mla_w_uq": w_uq.astype(BF16),
        "mla_kv_lat_norm": row(w["mla_kv_lat_norm"]), "mla_w_uk": w_uk.astype(BF16),
        "mla_w_uv": w_uv.astype(BF16),
        "mla_q_norm": _to_lanes(w["mla_q_norm"], _MLA_LANE_OF_DIM).reshape(1, LANES),
        "mla_k_norm": _to_lanes(w["mla_k_norm"], _MLA_LANE_OF_DIM).reshape(1, LANES),
        "w_branch_b": w["w_branch_b"].astype(BF16),
        "dil_qk_norm": _to_lanes(jnp.stack([w["dil_q_norm"], w["dil_k_norm"]]),
                                 _DIL_LANE_OF_DIM).reshape(2, 1, LANES),
        "w_branch_c": w["w_branch_c"].astype(BF16), "w_out": w["w_out"].astype(BF16),
        "norm_xattn": row(w["norm_xattn"]), "norm_mem": row(w["norm_mem"]),
        "w_xq": w["w_xq"].astype(BF16), "w_xkv": w["w_xkv"].astype(BF16),
        "xq_norm": row(w["xq_norm"]), "xk_norm": row(w["xk_norm"]), "w_xo": w["w_xo"].astype(BF16),
        "norm_ffn": row(w["norm_ffn"]), "w_ff1": w["w_ff1"].astype(BF16), "w_ff2": w["w_ff2"].astype(BF16),
    }


def _layer(x, mem, p, tables, batch, seq):
    (cos_m, sin_m), dil_tables = tables
    proj = _in_proj(x, p, dil_tables, seq)
    o_fwd = _gla_direction(proj, p["gla_wgf"], p["gla_bgf"], batch, seq, False)
    gla_o = _gla_direction(proj, p["gla_wgb"], p["gla_bgb"], batch, seq, True,
                           o_fwd=o_fwd, g_out=p["gla_out_norm"])
    q_m, k_m, v_m = _mla_prep(proj, p, cos_m, sin_m, seq)
    mla_o = _mla_attn(q_m, k_m, v_m, batch, seq)
    dil = [_dil_attn(proj, g, batch, seq) for g in range(len(DIL_GROUPS))]
    x = _combine(x, gla_o, mla_o, [o for o, _ in dil], [l for _, l in dil], proj, p, batch, seq)
    kv = _norm_matmul(mem, p["norm_mem"], p["w_xkv"], BF16, tm=N_MEM, tn=1024)
    return _xattn_ffn(x, kv, p, seq)


def _trunk(x, mem, layers):
    batch, seq, d = x.shape
    natural = np.arange(seq)
    row_pos = [natural.reshape(seq // PERM_TILE, PERM_TILE // dil, dil).transpose(0, 2, 1).reshape(seq)
               for _, dil in DIL_GROUPS]
    dil_cos, dil_sin = zip(*[_rope_tables(pos) for pos in row_pos])
    tables = (_rope_tables(natural), (jnp.stack(dil_cos), jnp.stack(dil_sin)))
    xf = x.reshape(batch * seq, d)
    memf = mem.reshape(batch * mem.shape[1], d)
    for p in layers:
        xf = _layer(xf, memf, p, tables, batch, seq)
    return xf.reshape(batch, seq, d)


def kernel(x_prompt, x_sample, mem_prompt, mem_sample, norm_mix, w_in, gla_w_gate_f, gla_b_gate_f, gla_w_gate_b, gla_b_gate_b, gla_out_norm, w_branch_a, mla_q_lat_norm, mla_w_uq, mla_kv_lat_norm, mla_w_ukv, mla_q_norm, mla_k_norm, w_branch_b, dil_q_norm, dil_k_norm, w_branch_c, w_out, norm_xattn, norm_mem, w_xq, w_xkv, xq_norm, xk_norm, w_xo, norm_ffn, w_ff1, w_ff2):
    stacked = dict(norm_mix=norm_mix, w_in=w_in, gla_w_gate_f=gla_w_gate_f, gla_b_gate_f=gla_b_gate_f,
                   gla_w_gate_b=gla_w_gate_b, gla_b_gate_b=gla_b_gate_b, gla_out_norm=gla_out_norm,
                   w_branch_a=w_branch_a, mla_q_lat_norm=mla_q_lat_norm, mla_w_uq=mla_w_uq,
                   mla_kv_lat_norm=mla_kv_lat_norm, mla_w_ukv=mla_w_ukv, mla_q_norm=mla_q_norm,
                   mla_k_norm=mla_k_norm, w_branch_b=w_branch_b, dil_q_norm=dil_q_norm, dil_k_norm=dil_k_norm,
                   w_branch_c=w_branch_c, w_out=w_out, norm_xattn=norm_xattn, norm_mem=norm_mem, w_xq=w_xq,
                   w_xkv=w_xkv, xq_norm=xq_norm, xk_norm=xk_norm, w_xo=w_xo, norm_ffn=norm_ffn,
                   w_ff1=w_ff1, w_ff2=w_ff2)
    layers = [_prep_layer({k: v[l] for k, v in stacked.items()}) for l in range(DEPTH)]
    return (_trunk(x_prompt, mem_prompt, layers), _trunk(x_sample, mem_sample, layers))
```
